```python
import jax, jax.numpy as jnp
from jax import lax
import numpy as np

D_MODEL = 1024
BATCH = 16
SEQ = 4096
DEPTH = 2
DEC_BATCH = 4
DEC_SEQ = 8192
PAST_LEN = 128

HEAD_DIM = 64
FNET_GROUPS = 4
FNET_WIDTH = FNET_GROUPS * HEAD_DIM
RWKV_HEADS = 12
RWKV_WIDTH = RWKV_HEADS * HEAD_DIM
DECAY_LORA = 64
ICLR_LORA = 64
GATE_LORA = 128
CONV_WIDTH = 3
N_BRANCHES = 2
RWKV_COLS = 3 * RWKV_WIDTH + 2 * DECAY_LORA + ICLR_LORA + GATE_LORA
IN_COLS = FNET_WIDTH + RWKV_COLS + N_BRANCHES * D_MODEL
D_FF = 2816
N_EXPERTS = 8
TOP_K = 2
D_EXPERT = 3584
EXPERT_BLOCK = 256
N_MOD = 6
N_DENSE = (DEPTH + 1) // 2
N_MOE = DEPTH // 2
RMS_EPS = 1e-6
GN_EPS = 64e-5

kernel_name = "cond_fnet_rwkv7_moe_encoder"


def rms_norm(x, g):
    xf = x.astype(jnp.float32)
    y = xf * lax.rsqrt(jnp.mean(xf * xf, axis=-1, keepdims=True) + RMS_EPS)
    return (y * g.astype(jnp.float32)).astype(x.dtype)


def modulate(h, shift, scale):
    return h * (1 + scale[:, None, :]) + shift[:, None, :]


def centred_conv(u, w):
    T = u.shape[1]
    up = jnp.pad(u, ((0, 0), (1, 1), (0, 0)))
    return w[0] * up[:, :T] + w[1] * up[:, 1:T + 1] + w[2] * up[:, 2:]


def fnet_branch(u, w_out):
    B, T, _ = u.shape
    z = u.astype(jnp.float32).reshape(B, T, FNET_GROUPS, HEAD_DIM)
    z = jnp.fft.fft2(z, axes=(1, 3), norm="ortho").real
    return z.reshape(B, T, FNET_WIDTH).astype(u.dtype) @ w_out


def wkv_scan(r, w, k, v, a, b, reverse):
    B, T, H, N = r.shape
    xs = tuple(jnp.moveaxis(t, 1, 0) for t in (r, w, k, v, a, b))

    def step(S, inp):
        r_t, w_t, k_t, v_t, a_t, b_t = inp
        sa = jnp.einsum("bhij,bhj->bhi", S, a_t)
        S = (S * w_t[:, :, None, :] + sa[..., None] * b_t[:, :, None, :]
             + v_t[..., None] * k_t[:, :, None, :])
        return S, jnp.einsum("bhij,bhj->bhi", S, r_t)

    S0 = jnp.zeros((B, H, N, N), jnp.float32)
    _, o = lax.scan(step, S0, xs, reverse=reverse)
    return jnp.moveaxis(o, 0, 1)


def rwkv7_branch(u, conv_w, decay_w0, decay_w2, iclr_a0, iclr_a2, gate_g2, k_k, k_a, r_k,
                 lnx_w, lnx_b, w_out):
    dtype = u.dtype
    B, T, _ = u.shape
    u = centred_conv(u, conv_w).astype(jnp.float32)
    splits = np.cumsum([RWKV_WIDTH, RWKV_WIDTH, RWKV_WIDTH, DECAY_LORA, DECAY_LORA, ICLR_LORA]).tolist()
    r, k, v, xw_f, xw_b, xa, xg = jnp.split(u, splits, axis=-1)

    def heads(t):
        return t.reshape(B, T, RWKV_HEADS, HEAD_DIM)

    def decay(xw, w0, w2):
        logw = -jax.nn.softplus(-(w0 + jnp.tanh(xw) @ w2)) - 0.5
        return jnp.exp(-jnp.exp(logw))

    w_f = heads(decay(xw_f, decay_w0[0], decay_w2[0]))
    w_b = heads(decay(xw_b, decay_w0[1], decay_w2[1]))
    a = jax.nn.sigmoid(iclr_a0 + xa @ iclr_a2)
    g = jax.nn.sigmoid(xg) @ gate_g2
    kk = heads(k * k_k)
    kk = kk * lax.rsqrt(jnp.sum(kk * kk, axis=-1, keepdims=True) + 1e-12)
    k = k * (1 + (a - 1) * k_a)
    r_h, k_h, v_h, a_h = heads(r), heads(k), heads(v), heads(a)
    neg_kk, kk_a = -kk, kk * a_h
    o = (wkv_scan(r_h, w_f, k_h, v_h, neg_kk, kk_a, False)
         + wkv_scan(r_h, w_b, k_h, v_h, neg_kk, kk_a, True))
    mu = jnp.mean(o, axis=-1, keepdims=True)
    var = jnp.mean(jnp.square(o - mu), axis=-1, keepdims=True)
    o = ((o - mu) * lax.rsqrt(var + GN_EPS) * lnx_w.reshape(RWKV_HEADS, HEAD_DIM)
         + lnx_b.reshape(RWKV_HEADS, HEAD_DIM))
    o = o + jnp.sum(r_h * k_h * r_k, axis=-1, keepdims=True) * v_h
    y = (o.reshape(B, T, RWKV_WIDTH) * g).astype(dtype)
    return y @ w_out


def token_mixer(h, w_in, conv_w, decay_w0, decay_w2, iclr_a0, iclr_a2, gate_g2, k_k, k_a, r_k,
                lnx_w, lnx_b, w_fnet_out, w_rwkv_out, w_o):
    proj = h @ w_in
    u_a = proj[..., :FNET_WIDTH]
    u_b = proj[..., FNET_WIDTH:FNET_WIDTH + RWKV_COLS]
    gates = jax.nn.sigmoid(proj[..., FNET_WIDTH + RWKV_COLS:].astype(jnp.float32)).astype(h.dtype)
    y_a = fnet_branch(u_a, w_fnet_out)
    y_b = rwkv7_branch(u_b, conv_w, decay_w0, decay_w2, iclr_a0, iclr_a2, gate_g2, k_k, k_a,
                       r_k, lnx_w, lnx_b, w_rwkv_out)
    merged = gates[..., :D_MODEL] * y_a + gates[..., D_MODEL:] * y_b
    return merged @ w_o


def swiglu(h, w_gate, w_up, w_down):
    return (jax.nn.silu(h @ w_gate) * (h @ w_up)) @ w_down


def moe_swiglu(h, router_w, w_gate, w_up, w_down):
    B, T, D = h.shape
    N = B * T
    xf = h.reshape(N, D)
    logits = (xf @ router_w).astype(jnp.float32)
    top_logit, top_idx = lax.top_k(logits, TOP_K)
    top_w = jax.nn.softmax(top_logit, axis=-1).astype(h.dtype)
    flat_e = top_idx.reshape(-1).astype(jnp.int32)
    flat_tok = jnp.repeat(jnp.arange(N, dtype=jnp.int32), TOP_K)
    flat_w = top_w.reshape(-1)
    order = jnp.argsort(flat_e)
    sorted_e = flat_e[order]
    counts = jnp.zeros((N_EXPERTS,), jnp.int32).at[flat_e].add(1)
    padded = ((counts + EXPERT_BLOCK - 1) // EXPERT_BLOCK) * EXPERT_BLOCK
    start = jnp.cumsum(counts) - counts
    padded_end = jnp.cumsum(padded)
    padded_start = padded_end - padded
    dest = padded_start[sorted_e] + jnp.arange(N * TOP_K, dtype=jnp.int32) - start[sorted_e]
    n_rows = (-(-(N * TOP_K) // EXPERT_BLOCK) + N_EXPERTS) * EXPERT_BLOCK
    n_blocks = n_rows // EXPERT_BLOCK
    row_tok = jnp.full((n_rows,), N, jnp.int32).at[dest].set(flat_tok[order])
    row_w = jnp.zeros((n_rows,), h.dtype).at[dest].set(flat_w[order])
    block_e = jnp.minimum(
        jnp.searchsorted(padded_end, jnp.arange(n_blocks, dtype=jnp.int32) * EXPERT_BLOCK, side="right"),
        N_EXPERTS - 1)
    x_rows = jnp.concatenate([xf, jnp.zeros((1, D), h.dtype)], axis=0)[row_tok]
    x_rows = x_rows.reshape(n_blocks, EXPERT_BLOCK, D)

    def expert_block(args):
        xb, e = args
        return (jax.nn.silu(xb @ w_gate[e]) * (xb @ w_up[e])) @ w_down[e]

    y_rows = lax.map(expert_block, (x_rows, block_e)).reshape(n_rows, D) * row_w[:, None]
    y = jnp.zeros((N + 1, D), h.dtype).at[row_tok].add(y_rows)[:N]
    return y.reshape(B, T, D)


def encoder(x, c, w_ada, b_ada, ln_mix, w_in, conv_w, decay_w0, decay_w2, iclr_a0, iclr_a2,
            gate_g2, k_k, k_a, r_k, lnx_w, lnx_b, w_fnet_out, w_rwkv_out, w_o, ln_ffn,
            ff_w_gate, ff_w_up, ff_w_down, router_w, moe_w_gate, moe_w_up, moe_w_down, ln_final):
    c_act = jax.nn.silu(c)
    for i in range(DEPTH):
        mod = c_act @ w_ada[i] + b_ada[i]
        sh1, sc1, gt1, sh2, sc2, gt2 = jnp.split(mod, N_MOD, axis=-1)
        h = modulate(rms_norm(x, ln_mix[i]), sh1, sc1)
        mix = token_mixer(h, w_in[i], conv_w[i], decay_w0[i], decay_w2[i], iclr_a0[i],
                          iclr_a2[i], gate_g2[i], k_k[i], k_a[i], r_k[i], lnx_w[i], lnx_b[i],
                          w_fnet_out[i], w_rwkv_out[i], w_o[i])
        x = x + gt1[:, None, :] * mix
        h = modulate(rms_norm(x, ln_ffn[i]), sh2, sc2)
        j = i // 2
        if i % 2 == 0:
            f = swiglu(h, ff_w_gate[j], ff_w_up[j], ff_w_down[j])
        else:
            f = moe_swiglu(h, router_w[j], moe_w_gate[j], moe_w_up[j], moe_w_down[j])
        x = x + gt2[:, None, :] * f
    return rms_norm(x, ln_final)


def setup_inputs(seed: int = 0) -> dict:
    key = jax.random.key(seed)
    ks = iter(jax.random.split(key, 40))
    L, D = DEPTH, D_MODEL

    def nrm(shape, scale):
        return jax.random.normal(next(ks), shape, jnp.float32) * scale

    def gain(shape):
        return 1.0 + nrm(shape, 0.05)

    conv_base = jnp.array([0.25, 0.5, 0.25], jnp.float32).reshape(1, CONV_WIDTH, 1)
    return {
        "x_prompt": nrm((BATCH, SEQ, D), 1.0),
        "x_sample": nrm((DEC_BATCH, DEC_SEQ, D), 1.0),
        "c_prompt": nrm((BATCH, D), 1.0),
        "c_sample": nrm((DEC_BATCH, D), 1.0),
        "w_ada": nrm((L, D, N_MOD * D), D ** -0.5),
        "b_ada": nrm((L, N_MOD * D), 0.02),
        "ln_mix": gain((L, D)),
        "w_in": nrm((L, D, IN_COLS), D ** -0.5),
        "conv_w": conv_base + nrm((L, CONV_WIDTH, RWKV_COLS), 0.1),
        "decay_w0": jax.random.uniform(next(ks), (L, 2, RWKV_WIDTH), jnp.float32, -6.0, 1.0),
        "decay_w2": nrm((L, 2, DECAY_LORA, RWKV_WIDTH), DECAY_LORA ** -0.5),
        "iclr_a0": nrm((L, RWKV_WIDTH), 0.1),
        "iclr_a2": nrm((L, ICLR_LORA, RWKV_WIDTH), ICLR_LORA ** -0.5),
        "gate_g2": nrm((L, GATE_LORA, RWKV_WIDTH), GATE_LORA ** -0.5),
        "k_k": 0.85 + nrm((L, RWKV_WIDTH), 0.05),
        "k_a": gain((L, RWKV_WIDTH)),
        "r_k": nrm((L, RWKV_HEADS, HEAD_DIM), 0.1),
        "lnx_w": gain((L, RWKV_WIDTH)),
        "lnx_b": nrm((L, RWKV_WIDTH), 0.02),
        "w_fnet_out": nrm((L, FNET_WIDTH, D), FNET_WIDTH ** -0.5),
        "w_rwkv_out": nrm((L, RWKV_WIDTH, D), RWKV_WIDTH ** -0.5),
        "w_o": nrm((L, D, D), D ** -0.5),
        "ln_ffn": gain((L, D)),
        "ff_w_gate": nrm((N_DENSE, D, D_FF), D ** -0.5),
        "ff_w_up": nrm((N_DENSE, D, D_FF), D ** -0.5),
        "ff_w_down": nrm((N_DENSE, D_FF, D), D_FF ** -0.5),
        "router_w": nrm((N_MOE, D, N_EXPERTS), D ** -0.5),
        "moe_w_gate": nrm((N_MOE, N_EXPERTS, D, D_EXPERT), D ** -0.5),
        "moe_w_up": nrm((N_MOE, N_EXPERTS, D, D_EXPERT), D ** -0.5),
        "moe_w_down": nrm((N_MOE, N_EXPERTS, D_EXPERT, D), D_EXPERT ** -0.5),
        "ln_final": gain((D,)),
    }


def reference(x_prompt, x_sample, c_prompt, c_sample, w_ada, b_ada, ln_mix, w_in, conv_w,
              decay_w0, decay_w2, iclr_a0, iclr_a2, gate_g2, k_k, k_a, r_k, lnx_w, lnx_b,
              w_fnet_out, w_rwkv_out, w_o, ln_ffn, ff_w_gate, ff_w_up, ff_w_down, router_w,
              moe_w_gate, moe_w_up, moe_w_down, ln_final):
    weights = (w_ada, b_ada, ln_mix, w_in, conv_w, decay_w0, decay_w2, iclr_a0, iclr_a2,
               gate_g2, k_k, k_a, r_k, lnx_w, lnx_b, w_fnet_out, w_rwkv_out, w_o, ln_ffn,
               ff_w_gate, ff_w_up, ff_w_down, router_w, moe_w_gate, moe_w_up, moe_w_down,
               ln_final)
    y_prompt = encoder(x_prompt, c_prompt, *weights)
    y_sample = encoder(x_sample, c_sample, *weights)
    return (y_prompt, y_sample)
```

```python
import functools
import math

import jax
import jax.numpy as jnp
from jax import lax
from jax.experimental import pallas as pl
from jax.experimental.pallas import tpu as pltpu

F32 = jnp.float32
BF16 = jnp.bfloat16
HIGHEST = lax.Precision.HIGHEST

HEAD_DIM = 64
LANES = 128
FNET_WIDTH = 256
RWKV_WIDTH = 768
LORA_TAIL = 384
N_MOD = 6
N_EXPERTS = 8
RMS_EPS = 1e-6
GN_EPS = 64e-5
WKV_CHUNK = 128
MOE_ROW_BLOCK = 512
VMEM_LIMIT = 56 * 1024 * 1024


def _params(sem, vmem=VMEM_LIMIT):
    return pltpu.CompilerParams(dimension_semantics=sem, vmem_limit_bytes=vmem)


def _const_spec(shape):
    nd = len(shape)
    return pl.BlockSpec(shape, lambda *_: (0,) * nd, pipeline_mode=pl.Buffered(1))


def _dot(a, b):
    return jnp.dot(a.astype(BF16), b.astype(BF16), preferred_element_type=F32)


def _dot_nt(a, b):
    return lax.dot_general(a.astype(BF16), b.astype(BF16), (((1,), (1,)), ((), ())),
                           preferred_element_type=F32)


def _dot_tn(a, b):
    return lax.dot_general(a.astype(BF16), b.astype(BF16), (((0,), (0,)), ((), ())),
                           preferred_element_type=F32)


def _split_bf16(x, parts):
    out, rem = [], x
    for _ in range(parts):
        p = rem.astype(BF16)
        out.append(p)
        rem = rem - p.astype(F32)
    return out


def _dot_exact_rhs(x, m, parts):
    acc = None
    for p in _split_bf16(x, parts):
        t = jnp.dot(p, m, preferred_element_type=F32)
        acc = t if acc is None else acc + t
    return acc


def _dot_exact_lhs(m, x, parts):
    acc = None
    for p in _split_bf16(x, parts):
        t = jnp.dot(m, p, preferred_element_type=F32)
        acc = t if acc is None else acc + t
    return acc


def _dot_f32(a, b):
    return jnp.dot(a, b, precision=HIGHEST, preferred_element_type=F32)


def _norm_mod(x, gain, shift, scale):
    ms = jnp.mean(x * x, axis=-1, keepdims=True)
    return x * lax.rsqrt(ms + RMS_EPS) * gain * (1.0 + scale) + shift


def _silu(x):
    return x * jax.nn.sigmoid(x)


def _head_reduce(x, m):
    cols = [_dot_exact_rhs(x[:, j:j + LANES], m, 2) for j in range(0, x.shape[1], LANES)]
    return jnp.concatenate(cols, axis=1)


def _row_tile(t, want):
    tm = min(t, want)
    assert t % tm == 0
    return tm


def _ada_kernel(c_ref, w_ref, b_ref, o_ref):
    o_ref[0] = _dot_f32(_silu(c_ref[...]), w_ref[0]) + b_ref[0]


def _ada(c_all, w_ada, b_ada):
    depth, d, n = w_ada.shape
    bc = c_all.shape[0]
    tn = n // 4
    return pl.pallas_call(
        _ada_kernel,
        grid=(depth, n // tn),
        in_specs=[pl.BlockSpec((bc, d), lambda l, j: (0, 0)),
                  pl.BlockSpec((1, d, tn), lambda l, j: (l, 0, j)),
                  pl.BlockSpec((1, 1, tn), lambda l, j: (l, 0, j))],
        out_specs=pl.BlockSpec((1, bc, tn), lambda l, j: (l, 0, j)),
        out_shape=jax.ShapeDtypeStruct((depth, bc, n), F32),
        compiler_params=_params(("parallel", "parallel")),
        name="ada",
    )(c_all, w_ada, b_ada.reshape(depth, 1, n))


def _inproj_kernel(x_ref, mod_ref, ln_ref, wa_ref, wm_ref, wt_ref, wg_ref, cs_ref,
                   zc_ref, zs_ref, um_ref, ut_ref, g_ref):
    h = _norm_mod(x_ref[0], ln_ref[...], mod_ref[0, 0:1, :], mod_ref[0, 1:2, :]).astype(BF16)
    ua = jnp.dot(h, wa_ref[...], preferred_element_type=F32)
    zz = jnp.dot(ua.astype(BF16), cs_ref[...], preferred_element_type=F32)
    zc_ref[0] = zz[:, :FNET_WIDTH].astype(BF16)
    zs_ref[0] = zz[:, FNET_WIDTH:].astype(BF16)
    step = 384
    for c0 in range(0, wm_ref.shape[1], step):
        um_ref[0, :, c0:c0 + step] = jnp.dot(
            h, wm_ref[:, c0:c0 + step], preferred_element_type=F32).astype(BF16)
    ut_ref[0] = jnp.dot(h, wt_ref[...], preferred_element_type=F32)
    step = 512
    for c0 in range(0, wg_ref.shape[1], step):
        g_ref[0, :, c0:c0 + step] = jax.nn.sigmoid(
            jnp.dot(h, wg_ref[:, c0:c0 + step], preferred_element_type=F32)).astype(BF16)


def _inproj(x, mod, ln, wa, wm, wt, wg, cs):
    b, t, d = x.shape
    tm = _row_tile(t, 512)
    row = lambda w: pl.BlockSpec((1, tm, w), lambda i, j: (i, j, 0))
    outs = [(FNET_WIDTH, BF16), (FNET_WIDTH, BF16), (wm.shape[1], BF16), (wt.shape[1], F32),
            (wg.shape[1], BF16)]
    return pl.pallas_call(
        _inproj_kernel,
        grid=(b, t // tm),
        in_specs=[row(d), pl.BlockSpec((1, N_MOD, d), lambda i, j: (i, 0, 0)),
                  _const_spec(ln.shape), _const_spec(wa.shape), _const_spec(wm.shape),
                  _const_spec(wt.shape), _const_spec(wg.shape), _const_spec(cs.shape)],
        out_specs=[row(w) for w, _ in outs],
        out_shape=[jax.ShapeDtypeStruct((b, t, w), dt) for w, dt in outs],
        compiler_params=_params(("parallel", "parallel")),
        name="inproj",
    )(x, mod, ln, wa, wm, wt, wg, cs)


def _fnet_kernel(ct_ref, st_ref, zc_ref, zs_ref, o_ref, *, scale):
    y = (jnp.dot(ct_ref[...], zc_ref[0], preferred_element_type=F32)
         - jnp.dot(st_ref[...], zs_ref[0], preferred_element_type=F32))
    o_ref[0] = (y * scale).astype(BF16)


def _dft_matrices(t):
    k = lax.iota(jnp.int32, t)
    ang = ((k[:, None] * k[None, :]) % t).astype(F32) * (2.0 * math.pi / t)
    return jnp.cos(ang).astype(BF16), jnp.sin(ang).astype(BF16)


def _fnet(zc, zs, ct, st):
    b, t, w = zc.shape
    tm = _row_tile(t, max(128, (2 * 1024 * 1024) // t))
    return pl.pallas_call(
        functools.partial(_fnet_kernel, scale=1.0 / math.sqrt(t * HEAD_DIM)),
        grid=(b, t // tm),
        in_specs=[pl.BlockSpec((tm, t), lambda i, j: (j, 0)),
                  pl.BlockSpec((tm, t), lambda i, j: (j, 0)),
                  pl.BlockSpec((1, t, w), lambda i, j: (i, 0, 0)),
                  pl.BlockSpec((1, t, w), lambda i, j: (i, 0, 0))],
        out_specs=pl.BlockSpec((1, tm, w), lambda i, j: (i, j, 0)),
        out_shape=jax.ShapeDtypeStruct((b, t, w), BF16),
        compiler_params=_params(("parallel", "arbitrary")),
        name="fnet",
    )(ct, st, zc, zs)


def _conv3(main, prev_row, next_row, w):
    tm = main.shape[0]
    ri = lax.broadcasted_iota(jnp.int32, main.shape, 0)
    up = jnp.where(ri == 0, prev_row, pltpu.roll(main, 1, 0))
    dn = jnp.where(ri == tm - 1, next_row, pltpu.roll(main, tm - 1, 0))
    return w[0:1] * up + w[1:2] * main + w[2:3] * dn


def _prep_kernel(um_ref, ump_ref, umn_ref, ut_ref, utp_ref, utn_ref, cwm_ref, cwt_ref,
                 w0_ref, w2_ref, a0_ref, a2_ref, g2_ref, kk_ref, ka_ref, rk_ref, hs_ref,
                 r_ref, k_ref, v_ref, a_ref, b_ref, ef_ref, eb_ref, g_ref, bonus_ref):
    j = pl.program_id(1)
    first = j == 0
    last = j == pl.num_programs(1) - 1
    hp = ump_ref.shape[1]

    def conv_main(c0, c1):
        prev = jnp.where(first, 0.0, ump_ref[0, hp - 1:hp, c0:c1].astype(F32))
        nxt = jnp.where(last, 0.0, umn_ref[0, 0:1, c0:c1].astype(F32))
        return _conv3(um_ref[0, :, c0:c1].astype(F32), prev, nxt, cwm_ref[:, c0:c1])

    w = RWKV_WIDTH
    r = conv_main(0, w)
    k = conv_main(w, 2 * w)
    v = conv_main(2 * w, 3 * w)
    tp = utp_ref.shape[1]
    tail = _conv3(ut_ref[0], jnp.where(first, 0.0, utp_ref[0, tp - 1:tp, :]),
                  jnp.where(last, 0.0, utn_ref[0, 0:1, :]), cwt_ref[...])
    xw, xa, xg = tail[:, 0:LANES], tail[:, LANES:2 * LANES], tail[:, 2 * LANES:3 * LANES]

    e = jax.nn.sigmoid(w0_ref[...] + _dot_f32(jnp.tanh(xw), w2_ref[...])) * math.exp(-0.5)
    ef_ref[0] = e[:, :w]
    eb_ref[0] = e[:, w:]
    a = jax.nn.sigmoid(a0_ref[...] + _dot_f32(xa, a2_ref[...]))
    g_ref[0] = _dot_f32(jax.nn.sigmoid(xg), g2_ref[...]).astype(BF16)

    kk = k * kk_ref[...]
    kk = kk * lax.rsqrt(_head_reduce(kk * kk, hs_ref[...]) + 1e-12)
    k = k * (1.0 + (a - 1.0) * ka_ref[...])
    r_ref[0] = r.astype(BF16)
    k_ref[0] = k.astype(BF16)
    v_ref[0] = v.astype(BF16)
    a_ref[0] = (-kk).astype(BF16)
    b_ref[0] = (kk * a).astype(BF16)
    bonus_ref[0] = (_head_reduce(r * k * rk_ref[...], hs_ref[...]) * v).astype(BF16)


def _prep(um, ut, p):
    b, t, wm = um.shape
    wt = ut.shape[2]
    tm = _row_tile(t, 256)
    hm, ht = 16, 8
    nm, nt = tm // hm, tm // ht
    main = lambda w: pl.BlockSpec((1, tm, w), lambda i, j: (i, j, 0))
    prev = lambda h, n, w: pl.BlockSpec((1, h, w), lambda i, j: (i, jnp.maximum(j * n - 1, 0), 0))
    nxt = lambda h, n, w: pl.BlockSpec(
        (1, h, w), lambda i, j: (i, jnp.minimum((j + 1) * n, t // h - 1), 0))
    consts = [p["conv_main"], p["conv_tail"], p["decay_w0"], p["decay_w2"], p["iclr_a0"],
              p["iclr_a2"], p["gate_g2"], p["k_k"], p["k_a"], p["r_k"], p["head_sum"]]
    outs = [BF16] * 5 + [F32, F32, BF16, BF16]
    return pl.pallas_call(
        _prep_kernel,
        grid=(b, t // tm),
        in_specs=[main(wm), prev(hm, nm, wm), nxt(hm, nm, wm),
                  main(wt), prev(ht, nt, wt), nxt(ht, nt, wt)]
                 + [_const_spec(c.shape) for c in consts],
        out_specs=[main(RWKV_WIDTH) for _ in outs],
        out_shape=[jax.ShapeDtypeStruct((b, t, RWKV_WIDTH), dt) for dt in outs],
        compiler_params=_params(("parallel", "parallel")),
        name="rwkv_prep",
    )(um, um, um, ut, ut, ut, *consts)


def _tri_inverse(l, row, col):
    t = jnp.where(row == col, 1.0, 0.0) + jnp.where((row >> 1) == (col >> 1), l, 0.0)
    for sh in range(1, (l.shape[0] - 1).bit_length()):
        joins = ((row >> (sh + 1)) == (col >> (sh + 1))) & ((row >> sh) != (col >> sh))
        t = t + _dot(t, _dot(jnp.where(joins, l, 0.0), t))
    return t


def _wkv_kernel(r_ref, k_ref, v_ref, a_ref, b_ref, e_ref, o_ref, s_ref, *, reverse):
    c = r_ref.shape[1]

    @pl.when(pl.program_id(2) == 0)
    def _():
        s_ref[...] = jnp.zeros_like(s_ref)

    r = r_ref[0].astype(F32)
    k = k_ref[0].astype(F32)
    vb = v_ref[0]
    a = a_ref[0].astype(F32)
    b = b_ref[0].astype(F32)
    e = e_ref[0]

    row = lax.broadcasted_iota(jnp.int32, (c, c), 0)
    col = lax.broadcasted_iota(jnp.int32, (c, c), 1)
    if reverse:
        incl, strict = col >= row, col > row
    else:
        incl, strict = col <= row, col < row
    cum = _dot_exact_lhs(jnp.where(incl, 1.0, 0.0).astype(BF16), e, 3)
    end = 0 if reverse else c - 1
    total = cum[end:end + 1, :]
    ref = cum[c // 2:c // 2 + 1, :]
    dl = cum - ref
    g_inv = jnp.exp(dl)
    at = a * jnp.exp(e - dl)
    rt = r * jnp.exp(-dl)
    bt = b * g_inv
    kt = k * g_inv
    g_end = jnp.exp(ref - total)
    bh = bt * g_end
    kh = kt * g_end
    btb = bt.astype(BF16)
    ktb = kt.astype(BF16)

    lane = lax.broadcasted_iota(jnp.int32, (1, LANES), 1)
    per_head = []
    for h in range(LANES // HEAD_DIM):
        mh = (lane >= h * HEAD_DIM) & (lane < (h + 1) * HEAD_DIM)
        ath = jnp.where(mh, at, 0.0).astype(BF16)
        rth = jnp.where(mh, rt, 0.0).astype(BF16)
        l_ab = jnp.where(strict, _dot_nt(ath, btb), 0.0)
        l_ak = jnp.where(strict, _dot_nt(ath, ktb), 0.0)
        m_rb = jnp.where(incl, _dot_nt(rth, btb), 0.0)
        m_rk = jnp.where(incl, _dot_nt(rth, ktb), 0.0)
        x = _dot(_tri_inverse(l_ab, row, col), jnp.concatenate([at, _dot(l_ak, vb)], axis=1))
        y = _dot(m_rb, x)
        per_head.append((mh, x[:, :LANES], x[:, LANES:], rt + y[:, :LANES],
                         y[:, LANES:] + _dot(m_rk, vb)))

    def pick(i):
        out = per_head[0][i]
        for h in range(1, len(per_head)):
            out = jnp.where(per_head[h][0], per_head[h][i], out)
        return out

    a_solved, u, r_hat, o_intra = pick(1), pick(2), pick(3), pick(4)
    s = s_ref[...]
    s_hat = s * jnp.exp(-ref)
    o_ref[0] = _dot_nt(r_hat, s_hat) + o_intra
    hr = lax.broadcasted_iota(jnp.int32, (LANES, LANES), 0) // HEAD_DIM
    hc = lax.broadcasted_iota(jnp.int32, (LANES, LANES), 1) // HEAD_DIM
    same = hr == hc
    g_t = jnp.where(same, _dot_tn(a_solved, bh), 0.0)
    h_t = jnp.where(same, _dot_tn(u, bh) + _dot_tn(vb, kh), 0.0)
    s_ref[...] = s * jnp.exp(-total) + _dot(s_hat, g_t) + h_t


def _wkv(r, k, v, a, b, e, reverse):
    bsz, t, w = r.shape
    c = min(WKV_CHUNK, t)
    nc = t // c
    if reverse:
        idx = lambda i, p, j: (i, nc - 1 - j, p)
    else:
        idx = lambda i, p, j: (i, j, p)
    spec = pl.BlockSpec((1, c, LANES), idx)
    return pl.pallas_call(
        functools.partial(_wkv_kernel, reverse=reverse),
        grid=(bsz, w // LANES, nc),
        in_specs=[spec] * 6,
        out_specs=spec,
        out_shape=jax.ShapeDtypeStruct((bsz, t, w), F32),
        scratch_shapes=[pltpu.VMEM((LANES, LANES), F32)],
        compiler_params=_params(("parallel", "parallel", "arbitrary")),
        name="wkv_bwd" if reverse else "wkv_fwd",
    )(r, k, v, a, b, e)


def _mixout_kernel(of_ref, ob_ref, bonus_ref, g_ref, yf_ref, gates_ref, x_ref, mod_ref,
                   lnw_ref, lnb_ref, hm_ref, wfo_ref, wro_ref, wo_ref, out_ref):
    d = x_ref.shape[2]
    o = of_ref[0] + ob_ref[0]
    dev = o - _head_reduce(o, hm_ref[...])
    var = _head_reduce(dev * dev, hm_ref[...])
    o = dev * lax.rsqrt(var + GN_EPS) * lnw_ref[...] + lnb_ref[...]
    y = (o + bonus_ref[0].astype(F32)) * g_ref[0].astype(F32)
    y_b = _dot(y, wro_ref[...])
    y_a = jnp.dot(yf_ref[0], wfo_ref[...], preferred_element_type=F32)
    gates = gates_ref[0].astype(F32)
    merged = gates[:, :d] * y_a + gates[:, d:] * y_b
    out_ref[0] = x_ref[0] + mod_ref[0, 2:3, :] * _dot(merged, wo_ref[...])


def _mixout(o_f, o_b, bonus, g, yf, gates, x, mod, p):
    b, t, d = x.shape
    tm = _row_tile(t, 512)
    row = lambda a: pl.BlockSpec((1, tm, a.shape[2]), lambda i, j: (i, j, 0))
    consts = [p["lnx_w"], p["lnx_b"], p["head_mean"], p["w_fnet_out"], p["w_rwkv_out"], p["w_o"]]
    acts = [o_f, o_b, bonus, g, yf, gates, x]
    return pl.pallas_call(
        _mixout_kernel,
        grid=(b, t // tm),
        in_specs=[row(a) for a in acts] + [pl.BlockSpec((1, N_MOD, d), lambda i, j: (i, 0, 0))]
                 + [_const_spec(c.shape) for c in consts],
        out_specs=pl.BlockSpec((1, tm, d), lambda i, j: (i, j, 0)),
        out_shape=jax.ShapeDtypeStruct((b, t, d), F32),
        compiler_params=_params(("parallel", "parallel")),
        name="mixout",
    )(*acts, mod, *consts)


def _ffn_kernel(x_ref, mod_ref, ln_ref, lnf_ref, wg_ref, wu_ref, wd_ref, out_ref, *, final_norm):
    x = x_ref[0]
    h = _norm_mod(x, ln_ref[...], mod_ref[0, 3:4, :], mod_ref[0, 4:5, :]).astype(BF16)
    f = wg_ref.shape[1]
    step = f // 4
    acc = None
    for c0 in range(0, f, step):
        gate = jnp.dot(h, wg_ref[:, c0:c0 + step], preferred_element_type=F32)
        up = jnp.dot(h, wu_ref[:, c0:c0 + step], preferred_element_type=F32)
        part = _dot(_silu(gate) * up, wd_ref[c0:c0 + step, :])
        acc = part if acc is None else acc + part
    out = x + mod_ref[0, 5:6, :] * acc
    if final_norm:
        out = _norm_mod(out, lnf_ref[...], 0.0, 0.0)
    out_ref[0] = out


def _ffn_dense(x, mod, ln, ln_final, wg, wu, wd, final_norm):
    b, t, d = x.shape
    tm = _row_tile(t, 512)
    consts = [ln, ln_final, wg, wu, wd]
    return pl.pallas_call(
        functools.partial(_ffn_kernel, final_norm=final_norm),
        grid=(b, t // tm),
        in_specs=[pl.BlockSpec((1, tm, d), lambda i, j: (i, j, 0)),
                  pl.BlockSpec((1, N_MOD, d), lambda i, j: (i, 0, 0))]
                 + [_const_spec(c.shape) for c in consts],
        out_specs=pl.BlockSpec((1, tm, d), lambda i, j: (i, j, 0)),
        out_shape=jax.ShapeDtypeStruct((b, t, d), F32),
        compiler_params=_params(("parallel", "parallel")),
        name="ffn_dense",
    )(x, mod, *consts)


def _route_kernel(x_ref, mod_ref, ln_ref, rw_ref, h_ref, route_ref, cnt_ref):
    h = _norm_mod(x_ref[0], ln_ref[...], mod_ref[0, 3:4, :], mod_ref[0, 4:5, :])
    h_ref[0] = h
    tm = h.shape[0]
    logits = lax.dot_general(rw_ref[...], h, (((1,), (1,)), ((), ())), precision=HIGHEST,
                             preferred_element_type=F32)
    ne = logits.shape[0]
    ei = lax.broadcasted_iota(jnp.int32, logits.shape, 0)
    m1 = jnp.max(logits, axis=0, keepdims=True)
    i1 = jnp.min(jnp.where(logits == m1, ei, ne), axis=0, keepdims=True)
    rest = jnp.where(ei == i1, -jnp.inf, logits)
    m2 = jnp.max(rest, axis=0, keepdims=True)
    i2 = jnp.min(jnp.where(rest == m2, ei, ne), axis=0, keepdims=True)
    t2 = jnp.exp(m2 - m1)
    w1 = 1.0 / (1.0 + t2)
    w2 = t2 / (1.0 + t2)
    oh1 = jnp.where(ei == i1, 1.0, 0.0)
    oh2 = jnp.where(ei == i2, 1.0, 0.0)
    before = (lax.broadcasted_iota(jnp.int32, (tm, tm), 0)
              < lax.broadcasted_iota(jnp.int32, (tm, tm), 1))
    before = jnp.where(before, 1.0, 0.0).astype(BF16)
    cs1 = jnp.dot(oh1.astype(BF16), before, preferred_element_type=F32)
    cs2 = jnp.dot(oh2.astype(BF16), before, preferred_element_type=F32)
    n1 = jnp.sum(oh1, axis=1, keepdims=True)
    n2 = jnp.sum(oh2, axis=1, keepdims=True)
    rank1 = jnp.sum(oh1 * cs1, axis=0, keepdims=True)
    rank2 = jnp.sum(oh2 * (cs2 + n1), axis=0, keepdims=True)
    zero = jnp.zeros_like(w1)
    route_ref[0, 0] = jnp.concatenate(
        [i1.astype(F32), i2.astype(F32), w1, w2, rank1, rank2, zero, zero], axis=0)
    cnt_ref[0, 0] = jnp.broadcast_to(n1 + n2, (ne, LANES))


def _route(x, mod, ln, router_t):
    b, t, d = x.shape
    tm = _row_tile(t, 256)
    nt = t // tm
    ne = router_t.shape[0]
    return pl.pallas_call(
        _route_kernel,
        grid=(b, nt),
        in_specs=[pl.BlockSpec((1, tm, d), lambda i, j: (i, j, 0)),
                  pl.BlockSpec((1, N_MOD, d), lambda i, j: (i, 0, 0)),
                  _const_spec(ln.shape), _const_spec(router_t.shape)],
        out_specs=[pl.BlockSpec((1, tm, d), lambda i, j: (i, j, 0)),
                   pl.BlockSpec((1, 1, 8, tm), lambda i, j: (i, j, 0, 0)),
                   pl.BlockSpec((1, 1, ne, LANES), lambda i, j: (i, j, 0, 0))],
        out_shape=[jax.ShapeDtypeStruct((b, t, d), F32),
                   jax.ShapeDtypeStruct((b, nt, 8, tm), F32),
                   jax.ShapeDtypeStruct((b, nt, ne, LANES), F32)],
        compiler_params=_params(("parallel", "parallel")),
        name="moe_route",
    )(x, mod, ln, router_t)


def _row_copy(src, dst, sem):
    return pltpu.make_async_copy(src, dst, sem)


def _dispatch_kernel(dest_ref, h_ref, rows_in_ref, rows_ref, sem):
    del rows_in_ref
    tm = h_ref.shape[0]

    def start(i, _):
        for s in range(2):
            _row_copy(h_ref.at[pl.ds(i, 1)], rows_ref.at[pl.ds(dest_ref[0, 0, 2 * i + s], 1)],
                      sem.at[s]).start()
        return 0

    def wait(i, _):
        for s in range(2):
            _row_copy(h_ref.at[pl.ds(i, 1)], rows_ref.at[pl.ds(0, 1)], sem.at[s]).wait()
        return 0

    lax.fori_loop(0, tm, start, 0)
    lax.fori_loop(0, tm, wait, 0)


def _dispatch(h, dest, n_rows):
    n, d = h.shape
    tm = dest.shape[2] // 2
    rows = jnp.zeros((n_rows, d), h.dtype)
    return pl.pallas_call(
        _dispatch_kernel,
        grid=(n // tm,),
        in_specs=[pl.BlockSpec((1, 1, 2 * tm), lambda i: (i, 0, 0), memory_space=pltpu.SMEM),
                  pl.BlockSpec((tm, d), lambda i: (i, 0)),
                  pl.BlockSpec(memory_space=pl.ANY)],
        out_specs=pl.BlockSpec(memory_space=pl.ANY),
        out_shape=jax.ShapeDtypeStruct((n_rows, d), h.dtype),
        scratch_shapes=[pltpu.SemaphoreType.DMA((2,))],
        input_output_aliases={2: 0},
        compiler_params=pltpu.CompilerParams(dimension_semantics=("arbitrary",),
                                             vmem_limit_bytes=VMEM_LIMIT, has_side_effects=True),
        name="moe_dispatch",
    )(dest, h, rows)


def _experts_kernel(be_ref, nb_ref, x_ref, wg_ref, wu_ref, wd_ref, y_ref):
    del be_ref

    @pl.when(pl.program_id(0) < nb_ref[0])
    def _():
        x = x_ref[...].astype(BF16)
        f = wg_ref.shape[2]
        step = f // 7
        acc = None
        for c0 in range(0, f, step):
            gate = jnp.dot(x, wg_ref[0, :, c0:c0 + step], preferred_element_type=F32)
            up = jnp.dot(x, wu_ref[0, :, c0:c0 + step], preferred_element_type=F32)
            part = _dot(_silu(gate) * up, wd_ref[0, c0:c0 + step, :])
            acc = part if acc is None else acc + part
        y_ref[...] = acc

    @pl.when(pl.program_id(0) >= nb_ref[0])
    def _():
        y_ref[...] = jnp.zeros_like(y_ref)


def _experts(rows, block_e, n_used, wg, wu, wd):
    n_rows, d = rows.shape
    rb = MOE_ROW_BLOCK
    f = wg.shape[2]
    wspec = lambda shape: pl.BlockSpec((1,) + shape, lambda i, be, nb: (be[i], 0, 0),
                                       pipeline_mode=pl.Buffered(1))
    return pl.pallas_call(
        _experts_kernel,
        grid_spec=pltpu.PrefetchScalarGridSpec(
            num_scalar_prefetch=2,
            grid=(n_rows // rb,),
            in_specs=[pl.BlockSpec((rb, d), lambda i, be, nb: (i, 0)),
                      wspec((d, f)), wspec((d, f)), wspec((f, d))],
            out_specs=pl.BlockSpec((rb, d), lambda i, be, nb: (i, 0))),
        out_shape=jax.ShapeDtypeStruct((n_rows, d), F32),
        compiler_params=_params(("arbitrary",)),
        name="moe_experts",
    )(block_e, n_used, rows, wg, wu, wd)


def _combine_kernel(dest_ref, y_hbm, x_ref, wts_ref, mod_ref, lnf_ref, out_ref, buf, sem,
                    *, final_norm):
    tm = x_ref.shape[1]

    def start(i, _):
        for s in range(2):
            _row_copy(y_hbm.at[pl.ds(dest_ref[0, 0, 2 * i + s], 1)], buf.at[s, pl.ds(i, 1)],
                      sem.at[s]).start()
        return 0

    def wait(i, _):
        for s in range(2):
            _row_copy(y_hbm.at[pl.ds(0, 1)], buf.at[s, pl.ds(i, 1)], sem.at[s]).wait()
        return 0

    lax.fori_loop(0, tm, start, 0)
    lax.fori_loop(0, tm, wait, 0)
    f = wts_ref[0, :, 0:1] * buf[0] + wts_ref[0, :, 1:2] * buf[1]
    out = x_ref[0] + mod_ref[0, 5:6, :] * f
    if final_norm:
        out = _norm_mod(out, lnf_ref[...], 0.0, 0.0)
    out_ref[0] = out


def _combine(y_rows, dest, wts, x, mod, ln_final, final_norm):
    b, t, d = x.shape
    tm = dest.shape[2] // 2
    nt = t // tm
    return pl.pallas_call(
        functools.partial(_combine_kernel, final_norm=final_norm),
        grid=(b, nt),
        in_specs=[pl.BlockSpec((1, 1, 2 * tm), lambda i, j: (i * nt + j, 0, 0),
                               memory_space=pltpu.SMEM),
                  pl.BlockSpec(memory_space=pl.ANY),
                  pl.BlockSpec((1, tm, d), lambda i, j: (i, j, 0)),
                  pl.BlockSpec((1, tm, 2), lambda i, j: (i, j, 0)),
                  pl.BlockSpec((1, N_MOD, d), lambda i, j: (i, 0, 0)),
                  _const_spec(ln_final.shape)],
        out_specs=pl.BlockSpec((1, tm, d), lambda i, j: (i, j, 0)),
        out_shape=jax.ShapeDtypeStruct((b, t, d), F32),
        scratch_shapes=[pltpu.VMEM((2, tm, d), F32), pltpu.SemaphoreType.DMA((2,))],
        compiler_params=_params(("arbitrary", "arbitrary")),
        name="moe_combine",
    )(dest, y_rows, x, wts, mod, ln_final)


def _ffn_moe(x, mod, ln, ln_final, router_t, wg, wu, wd, final_norm):
    b, t, d = x.shape
    n = b * t
    ne = router_t.shape[0]
    rb = MOE_ROW_BLOCK
    h, route, cnt = _route(x, mod, ln, router_t)
    nt, tm = route.shape[1], route.shape[3]
    tile_cnt = cnt[..., 0].reshape(b * nt, ne).astype(jnp.int32)
    totals = jnp.sum(tile_cnt, axis=0)
    padded = ((totals + rb - 1) // rb) * rb
    padded_end = jnp.cumsum(padded)
    base = (padded_end - padded)[None, :] + jnp.cumsum(tile_cnt, axis=0) - tile_cnt
    route = route.reshape(b * nt, 8, tm)
    experts = route[:, 0:2, :].astype(jnp.int32)
    rank = route[:, 4:6, :].astype(jnp.int32)
    onehot = experts[..., None] == jnp.arange(ne, dtype=jnp.int32)
    dest = jnp.sum(jnp.where(onehot, base[:, None, None, :], 0), axis=-1) + rank
    dest = jnp.swapaxes(dest, 1, 2).reshape(b * nt, 1, 2 * tm)
    wts = jnp.swapaxes(route[:, 2:4, :], 1, 2).reshape(b, t, 2)
    n_blocks = -(-(n * 2) // rb) + ne
    block_e = jnp.minimum(
        jnp.searchsorted(padded_end, jnp.arange(n_blocks, dtype=jnp.int32) * rb, side="right"),
        ne - 1).astype(jnp.int32)
    n_used = (padded_end[-1:] // rb).astype(jnp.int32)
    rows = _dispatch(h.reshape(n, d), dest, n_blocks * rb)
    y_rows = _experts(rows, block_e, n_used, wg, wu, wd)
    return _combine(y_rows, dest, wts, x, mod, ln_final, final_norm)


def _block_diag(m, n):
    return jnp.kron(jnp.eye(n, dtype=m.dtype), m)


def _layer_params(i, w_in, conv_w, decay_w0, decay_w2, iclr_a0, iclr_a2, gate_g2, k_k, k_a, r_k,
                  lnx_w, lnx_b, w_fnet_out, w_rwkv_out, w_o):
    w = RWKV_WIDTH
    lora = decay_w2.shape[2]
    tail0 = FNET_WIDTH + 3 * w

    def reorder_tail(m):
        pad = jnp.zeros(m.shape[:-1] + (LANES - lora,), m.dtype)
        return jnp.concatenate([m[..., :3 * lora], pad, m[..., 3 * lora:]], axis=-1)

    row = lambda v: v.reshape(1, -1)
    wi = w_in[i]
    rwkv_cols = conv_w.shape[2]
    zeros = jnp.zeros((lora, w), F32)
    return {
        "w_a": wi[:, :FNET_WIDTH].astype(BF16),
        "w_main": wi[:, FNET_WIDTH:tail0].astype(BF16),
        "w_tail": reorder_tail(wi[:, tail0:FNET_WIDTH + rwkv_cols]).astype(BF16),
        "w_gates": wi[:, FNET_WIDTH + rwkv_cols:].astype(BF16),
        "conv_main": conv_w[i][:, :3 * w],
        "conv_tail": reorder_tail(conv_w[i][:, 3 * w:]),
        "decay_w0": decay_w0[i].reshape(1, 2 * w),
        "decay_w2": jnp.concatenate(
            [jnp.concatenate([decay_w2[i, 0], zeros], axis=1),
             jnp.concatenate([zeros, decay_w2[i, 1]], axis=1)], axis=0),
        "iclr_a0": row(iclr_a0[i]),
        "iclr_a2": jnp.concatenate([iclr_a2[i], jnp.zeros((LANES - lora, w), F32)], axis=0),
        "gate_g2": gate_g2[i],
        "k_k": row(k_k[i]), "k_a": row(k_a[i]), "r_k": row(r_k[i]),
        "lnx_w": row(lnx_w[i]), "lnx_b": row(lnx_b[i]),
        "head_sum": _block_diag(jnp.ones((HEAD_DIM, HEAD_DIM), BF16), LANES // HEAD_DIM),
        "head_mean": _block_diag(jnp.full((HEAD_DIM, HEAD_DIM), 1.0 / HEAD_DIM, BF16),
                                 LANES // HEAD_DIM),
        "w_fnet_out": w_fnet_out[i].astype(BF16),
        "w_rwkv_out": w_rwkv_out[i].astype(BF16),
        "w_o": w_o[i].astype(BF16),
    }


def _channel_dft():
    k = jnp.arange(HEAD_DIM, dtype=jnp.int32)
    ang = ((k[:, None] * k[None, :]) % HEAD_DIM).astype(F32) * (2.0 * math.pi / HEAD_DIM)
    groups = FNET_WIDTH // HEAD_DIM
    return jnp.concatenate([_block_diag(jnp.cos(ang), groups), _block_diag(jnp.sin(ang), groups)],
                           axis=1).astype(BF16)


def _token_mixer(x, mod, ln, p, cs, ct, st):
    zc, zs, um, ut, gates = _inproj(x, mod, ln, p["w_a"], p["w_main"], p["w_tail"], p["w_gates"], cs)
    yf = _fnet(zc, zs, ct, st)
    r, k, v, a, b, e_f, e_b, g, bonus = _prep(um, ut, p)
    o_f = _wkv(r, k, v, a, b, e_f, False)
    o_b = _wkv(r, k, v, a, b, e_b, True)
    return _mixout(o_f, o_b, bonus, g, yf, gates, x, mod, p)


def kernel(x_prompt, x_sample, c_prompt, c_sample, w_ada, b_ada, ln_mix, w_in, conv_w, decay_w0, decay_w2, iclr_a0, iclr_a2, gate_g2, k_k, k_a, r_k, lnx_w, lnx_b, w_fnet_out, w_rwkv_out, w_o, ln_ffn, ff_w_gate, ff_w_up, ff_w_down, router_w, moe_w_gate, moe_w_up, moe_w_down, ln_final):
    depth, d = ln_mix.shape
    streams = [x_prompt, x_sample]
    nb = [x.shape[0] for x in streams]
    mod_all = _ada(jnp.concatenate([c_prompt, c_sample], axis=0), w_ada, b_ada)
    cs = _channel_dft()
    dfts = {}
    for x in streams:
        if x.shape[1] not in dfts:
            dfts[x.shape[1]] = _dft_matrices(x.shape[1])
    lnf = ln_final.reshape(1, d)
    for i in range(depth):
        p = _layer_params(i, w_in, conv_w, decay_w0, decay_w2, iclr_a0, iclr_a2, gate_g2, k_k, k_a,
                          r_k, lnx_w, lnx_b, w_fnet_out, w_rwkv_out, w_o)
        j = i // 2
        final = i == depth - 1
        if i % 2 == 0:
            ffw = (ff_w_gate[j].astype(BF16), ff_w_up[j].astype(BF16), ff_w_down[j].astype(BF16))
        else:
            ffw = (moe_w_gate[j].astype(BF16), moe_w_up[j].astype(BF16), moe_w_down[j].astype(BF16))
            router_t = router_w[j].T
        off = 0
        for s, x in enumerate(streams):
            mod = mod_all[i, off:off + nb[s]].reshape(nb[s], N_MOD, d)
            off += nb[s]
            x = _token_mixer(x, mod, ln_mix[i].reshape(1, d), p, cs, *dfts[x.shape[1]])
            ln2 = ln_ffn[i].reshape(1, d)
            if i % 2 == 0:
                x = _ffn_dense(x, mod, ln2, lnf, *ffw, final)
            else:
                x = _ffn_moe(x, mod, ln2, lnf, router_t, *ffw, final)
            streams[s] = x
    return tuple(streams)
```

```python
import functools
import math

import jax
import jax.numpy as jnp
from jax import lax
from jax.experimental import pallas as pl
from jax.experimental.pallas import tpu as pltpu

F32 = jnp.float32
BF16 = jnp.bfloat16
HIGHEST = lax.Precision.HIGHEST

HEAD_DIM = 64
LANES = 128
FNET_WIDTH = 256
RWKV_WIDTH = 768
LORA_TAIL = 384
N_MOD = 6
N_EXPERTS = 8
RMS_EPS = 1e-6
GN_EPS = 64e-5
WKV_CHUNK = 128
MOE_ROW_BLOCK = 512
VMEM_LIMIT = 56 * 1024 * 1024


def _params(sem, vmem=VMEM_LIMIT):
    return pltpu.CompilerParams(dimension_semantics=sem, vmem_limit_bytes=vmem)


def _const_spec(shape):
    nd = len(shape)
    return pl.BlockSpec(shape, lambda *_: (0,) * nd, pipeline_mode=pl.Buffered(1))


def _dot(a, b):
    return jnp.dot(a.astype(BF16), b.astype(BF16), preferred_element_type=F32)


def _dot_nt(a, b):
    return lax.dot_general(a.astype(BF16), b.astype(BF16), (((1,), (1,)), ((), ())),
                           preferred_element_type=F32)


def _dot_tn(a, b):
    return lax.dot_general(a.astype(BF16), b.astype(BF16), (((0,), (0,)), ((), ())),
                           preferred_element_type=F32)


def _split_bf16(x, parts):
    out, rem = [], x
    for _ in range(parts):
        p = rem.astype(BF16)
        out.append(p)
        rem = rem - p.astype(F32)
    return out


def _dot_exact_rhs(x, m, parts):
    acc = None
    for p in _split_bf16(x, parts):
        t = jnp.dot(p, m, preferred_element_type=F32)
        acc = t if acc is None else acc + t
    return acc


def _dot_exact_lhs(m, x, parts):
    acc = None
    for p in _split_bf16(x, parts):
        t = jnp.dot(m, p, preferred_element_type=F32)
        acc = t if acc is None else acc + t
    return acc


def _dot_f32(a, b):
    return jnp.dot(a, b, precision=HIGHEST, preferred_element_type=F32)


def _norm_mod(x, gain, shift, scale):
    ms = jnp.mean(x * x, axis=-1, keepdims=True)
    return x * lax.rsqrt(ms + RMS_EPS) * gain * (1.0 + scale) + shift


def _silu(x):
    return x * jax.nn.sigmoid(x)


def _head_reduce(x, m):
    cols = [_dot_exact_rhs(x[:, j:j + LANES], m, 2) for j in range(0, x.shape[1], LANES)]
    return jnp.concatenate(cols, axis=1)


def _row_tile(t, want):
    tm = min(t, want)
    assert t % tm == 0
    return tm


def _ada_kernel(c_ref, w_ref, b_ref, o_ref):
    o_ref[0] = _dot_f32(_silu(c_ref[...]), w_ref[0]) + b_ref[0]


def _ada(c_all, w_ada, b_ada):
    depth, d, n = w_ada.shape
    bc = c_all.shape[0]
    tn = n // 4
    return pl.pallas_call(
        _ada_kernel,
        grid=(depth, n // tn),
        in_specs=[pl.BlockSpec((bc, d), lambda l, j: (0, 0)),
                  pl.BlockSpec((1, d, tn), lambda l, j: (l, 0, j)),
                  pl.BlockSpec((1, 1, tn), lambda l, j: (l, 0, j))],
        out_specs=pl.BlockSpec((1, bc, tn), lambda l, j: (l, 0, j)),
        out_shape=jax.ShapeDtypeStruct((depth, bc, n), F32),
        compiler_params=_params(("parallel", "parallel")),
        name="ada",
    )(c_all, w_ada, b_ada.reshape(depth, 1, n))


def _inproj_kernel(x_ref, mod_ref, ln_ref, wa_ref, wm_ref, wt_ref, wg_ref, cs_ref,
                   zc_ref, zs_ref, um_ref, ut_ref, g_ref):
    h = _norm_mod(x_ref[0], ln_ref[...], mod_ref[0, 0:1, :], mod_ref[0, 1:2, :]).astype(BF16)
    ua = jnp.dot(h, wa_ref[...], preferred_element_type=F32)
    zz = jnp.dot(ua.astype(BF16), cs_ref[...], preferred_element_type=F32)
    zc_ref[0] = zz[:, :FNET_WIDTH].astype(BF16)
    zs_ref[0] = zz[:, FNET_WIDTH:].astype(BF16)
    step = 384
    for c0 in range(0, wm_ref.shape[1], step):
        um_ref[0, :, c0:c0 + step] = jnp.dot(
            h, wm_ref[:, c0:c0 + step], preferred_element_type=F32).astype(BF16)
    ut_ref[0] = jnp.dot(h, wt_ref[...], preferred_element_type=F32)
    step = 512
    for c0 in range(0, wg_ref.shape[1], step):
        g_ref[0, :, c0:c0 + step] = jax.nn.sigmoid(
            jnp.dot(h, wg_ref[:, c0:c0 + step], preferred_element_type=F32)).astype(BF16)


def _inproj(x, mod, ln, wa, wm, wt, wg, cs):
    b, t, d = x.shape
    tm = _row_tile(t, 512)
    row = lambda w: pl.BlockSpec((1, tm, w), lambda i, j: (i, j, 0))
    outs = [(FNET_WIDTH, BF16), (FNET_WIDTH, BF16), (wm.shape[1], BF16), (wt.shape[1], F32),
            (wg.shape[1], BF16)]
    return pl.pallas_call(
        _inproj_kernel,
        grid=(b, t // tm),
        in_specs=[row(d), pl.BlockSpec((1, N_MOD, d), lambda i, j: (i, 0, 0)),
                  _const_spec(ln.shape), _const_spec(wa.shape), _const_spec(wm.shape),
                  _const_spec(wt.shape), _const_spec(wg.shape), _const_spec(cs.shape)],
        out_specs=[row(w) for w, _ in outs],
        out_shape=[jax.ShapeDtypeStruct((b, t, w), dt) for w, dt in outs],
        compiler_params=_params(("parallel", "parallel")),
        name="inproj",
    )(x, mod, ln, wa, wm, wt, wg, cs)


def _fnet_kernel(ct_ref, st_ref, zc_ref, zs_ref, o_ref, *, scale):
    y = (jnp.dot(ct_ref[...], zc_ref[0], preferred_element_type=F32)
         - jnp.dot(st_ref[...], zs_ref[0], preferred_element_type=F32))
    o_ref[0] = (y * scale).astype(BF16)


def _dft_matrices(t):
    k = lax.iota(jnp.int32, t)
    ang = ((k[:, None] * k[None, :]) % t).astype(F32) * (2.0 * math.pi / t)
    return jnp.cos(ang).astype(BF16), jnp.sin(ang).astype(BF16)


def _fnet(zc, zs, ct, st):
    b, t, w = zc.shape
    tm = _row_tile(t, max(128, (2 * 1024 * 1024) // t))
    return pl.pallas_call(
        functools.partial(_fnet_kernel, scale=1.0 / math.sqrt(t * HEAD_DIM)),
        grid=(b, t // tm),
        in_specs=[pl.BlockSpec((tm, t), lambda i, j: (j, 0)),
                  pl.BlockSpec((tm, t), lambda i, j: (j, 0)),
                  pl.BlockSpec((1, t, w), lambda i, j: (i, 0, 0)),
                  pl.BlockSpec((1, t, w), lambda i, j: (i, 0, 0))],
        out_specs=pl.BlockSpec((1, tm, w), lambda i, j: (i, j, 0)),
        out_shape=jax.ShapeDtypeStruct((b, t, w), BF16),
        compiler_params=_params(("parallel", "arbitrary")),
        name="fnet",
    )(ct, st, zc, zs)


def _conv3(main, prev_row, next_row, w):
    tm = main.shape[0]
    ri = lax.broadcasted_iota(jnp.int32, main.shape, 0)
    up = jnp.where(ri == 0, prev_row, pltpu.roll(main, 1, 0))
    dn = jnp.where(ri == tm - 1, next_row, pltpu.roll(main, tm - 1, 0))
    return w[0:1] * up + w[1:2] * main + w[2:3] * dn


def _prep_kernel(um_ref, ump_ref, umn_ref, ut_ref, utp_ref, utn_ref, cwm_ref, cwt_ref,
                 w0_ref, w2_ref, a0_ref, a2_ref, g2_ref, kk_ref, ka_ref, rk_ref, hs_ref,
                 r_ref, k_ref, v_ref, a_ref, b_ref, ef_ref, eb_ref, g_ref, bonus_ref):
    j = pl.program_id(1)
    first = j == 0
    last = j == pl.num_programs(1) - 1
    hp = ump_ref.shape[1]

    def conv_main(c0, c1):
        prev = jnp.where(first, 0.0, ump_ref[0, hp - 1:hp, c0:c1].astype(F32))
        nxt = jnp.where(last, 0.0, umn_ref[0, 0:1, c0:c1].astype(F32))
        return _conv3(um_ref[0, :, c0:c1].astype(F32), prev, nxt, cwm_ref[:, c0:c1])

    w = RWKV_WIDTH
    r = conv_main(0, w)
    k = conv_main(w, 2 * w)
    v = conv_main(2 * w, 3 * w)
    tp = utp_ref.shape[1]
    tail = _conv3(ut_ref[0], jnp.where(first, 0.0, utp_ref[0, tp - 1:tp, :]),
                  jnp.where(last, 0.0, utn_ref[0, 0:1, :]), cwt_ref[...])
    xw, xa, xg = tail[:, 0:LANES], tail[:, LANES:2 * LANES], tail[:, 2 * LANES:3 * LANES]

    e = jax.nn.sigmoid(w0_ref[...] + _dot_f32(jnp.tanh(xw), w2_ref[...])) * math.exp(-0.5)
    ef_ref[0] = e[:, :w]
    eb_ref[0] = e[:, w:]
    a = jax.nn.sigmoid(a0_ref[...] + _dot_f32(xa, a2_ref[...]))
    g_ref[0] = _dot_f32(jax.nn.sigmoid(xg), g2_ref[...]).astype(BF16)

    kk = k * kk_ref[...]
    kk = kk * lax.rsqrt(_head_reduce(kk * kk, hs_ref[...]) + 1e-12)
    k = k * (1.0 + (a - 1.0) * ka_ref[...])
    r_ref[0] = r.astype(BF16)
    k_ref[0] = k.astype(BF16)
    v_ref[0] = v.astype(BF16)
    a_ref[0] = (-kk).astype(BF16)
    b_ref[0] = (kk * a).astype(BF16)
    bonus_ref[0] = (_head_reduce(r * k * rk_ref[...], hs_ref[...]) * v).astype(BF16)


def _prep(um, ut, p):
    b, t, wm = um.shape
    wt = ut.shape[2]
    tm = _row_tile(t, 256)
    hm, ht = 16, 8
    nm, nt = tm // hm, tm // ht
    main = lambda w: pl.BlockSpec((1, tm, w), lambda i, j: (i, j, 0))
    prev = lambda h, n, w: pl.BlockSpec((1, h, w), lambda i, j: (i, jnp.maximum(j * n - 1, 0), 0))
    nxt = lambda h, n, w: pl.BlockSpec(
        (1, h, w), lambda i, j: (i, jnp.minimum((j + 1) * n, t // h - 1), 0))
    consts = [p["conv_main"], p["conv_tail"], p["decay_w0"], p["decay_w2"], p["iclr_a0"],
              p["iclr_a2"], p["gate_g2"], p["k_k"], p["k_a"], p["r_k"], p["head_sum"]]
    outs = [BF16] * 5 + [F32, F32, BF16, BF16]
    return pl.pallas_call(
        _prep_kernel,
        grid=(b, t // tm),
        in_specs=[main(wm), prev(hm, nm, wm), nxt(hm, nm, wm),
                  main(wt), prev(ht, nt, wt), nxt(ht, nt, wt)]
                 + [_const_spec(c.shape) for c in consts],
        out_specs=[main(RWKV_WIDTH) for _ in outs],
        out_shape=[jax.ShapeDtypeStruct((b, t, RWKV_WIDTH), dt) for dt in outs],
        compiler_params=_params(("parallel", "parallel")),
        name="rwkv_prep",
    )(um, um, um, ut, ut, ut, *consts)


def _wkv_kernel(r_ref, k_ref, v_ref, a_ref, b_ref, e_ref, o_ref, s_ref, *, reverse):
    c = r_ref.shape[1]
    pairs = range(s_ref.shape[0])
    heads = range(LANES // HEAD_DIM)
    chains = [(p, h) for p in pairs for h in heads]

    @pl.when(pl.program_id(1) == 0)
    def _():
        s_ref[...] = jnp.zeros_like(s_ref)

    row = lax.broadcasted_iota(jnp.int32, (c, c), 0)
    col = lax.broadcasted_iota(jnp.int32, (c, c), 1)
    if reverse:
        incl, strict = col >= row, col > row
    else:
        incl, strict = col <= row, col < row
    tri = jnp.where(incl, 1.0, 0.0).astype(BF16)
    end = 0 if reverse else c - 1
    lane = lax.broadcasted_iota(jnp.int32, (1, LANES), 1)
    in_head = [(lane >= h * HEAD_DIM) & (lane < (h + 1) * HEAD_DIM) for h in heads]
    same_head = (lax.broadcasted_iota(jnp.int32, (LANES, LANES), 0) // HEAD_DIM
                 == lax.broadcasted_iota(jnp.int32, (LANES, LANES), 1) // HEAD_DIM)

    def tile(ref, p):
        return ref[0, :, p * LANES:(p + 1) * LANES]

    cum = [_dot_exact_lhs(tri, tile(e_ref, p), 3) for p in pairs]
    total = [x[end:end + 1, :] for x in cum]
    ref = [x[c // 2:c // 2 + 1, :] for x in cum]
    dl = [cum[p] - ref[p] for p in pairs]
    g_inv = [jnp.exp(dl[p]) for p in pairs]
    at = [tile(a_ref, p).astype(F32) * jnp.exp(tile(e_ref, p) - dl[p]) for p in pairs]
    rt = [tile(r_ref, p).astype(F32) * jnp.exp(-dl[p]) for p in pairs]
    bt = [tile(b_ref, p).astype(F32) * g_inv[p] for p in pairs]
    kt = [tile(k_ref, p).astype(F32) * g_inv[p] for p in pairs]
    g_end = [jnp.exp(ref[p] - total[p]) for p in pairs]
    bh = [(bt[p] * g_end[p]).astype(BF16) for p in pairs]
    kh = [(kt[p] * g_end[p]).astype(BF16) for p in pairs]
    bk = [jnp.concatenate([bt[p], kt[p]], axis=0).astype(BF16) for p in pairs]
    ar = [jnp.concatenate([at[p], rt[p]], axis=0) for p in pairs]
    vb = [tile(v_ref, p) for p in pairs]

    prod = [_dot_nt(jnp.where(in_head[h], ar[p], 0.0), bk[p]) for p, h in chains]
    l_ab = [jnp.where(strict, x[:c, :c], 0.0) for x in prod]
    l_ak = [jnp.where(strict, x[:c, c:], 0.0) for x in prod]
    m_rb = [jnp.where(incl, x[c:, :c], 0.0).astype(BF16) for x in prod]
    m_rk = [jnp.where(incl, x[c:, c:], 0.0).astype(BF16) for x in prod]
    del prod
    eye = jnp.where(row == col, 1.0, 0.0)
    t = [eye + jnp.where((row >> 1) == (col >> 1), x, 0.0) for x in l_ab]
    for sh in range(1, (c - 1).bit_length()):
        joins = ((row >> (sh + 1)) == (col >> (sh + 1))) & ((row >> sh) != (col >> sh))
        et = [_dot(jnp.where(joins, l_ab[i], 0.0), t[i]) for i in range(len(chains))]
        t = [t[i] + _dot(t[i], et[i]) for i in range(len(chains))]
    lakv = [_dot(l_ak[i], vb[p]) for i, (p, h) in enumerate(chains)]
    x = [_dot(t[i], jnp.concatenate([at[p], lakv[i]], axis=1)) for i, (p, h) in enumerate(chains)]
    y = [jnp.dot(m_rb[i], x[i].astype(BF16), preferred_element_type=F32) for i in range(len(chains))]
    mrkv = [jnp.dot(m_rk[i], vb[p], preferred_element_type=F32) for i, (p, h) in enumerate(chains)]

    def pick(vals, p):
        out = vals[p * len(heads)]
        for h in heads[1:]:
            out = jnp.where(in_head[h], vals[p * len(heads) + h], out)
        return out

    a_solved = [pick([v[:, :LANES] for v in x], p) for p in pairs]
    u = [pick([v[:, LANES:] for v in x], p) for p in pairs]
    r_hat = [rt[p] + pick([v[:, :LANES] for v in y], p) for p in pairs]
    o_intra = [pick([y[i][:, LANES:] + mrkv[i] for i in range(len(chains))], p) for p in pairs]
    s = [s_ref[p] for p in pairs]
    s_hat = [(s[p] * jnp.exp(-ref[p])).astype(BF16) for p in pairs]
    out = [_dot_nt(r_hat[p], s_hat[p]) + o_intra[p] for p in pairs]
    g_t = [jnp.where(same_head, _dot_tn(a_solved[p], bh[p]), 0.0) for p in pairs]
    h_t = [jnp.where(same_head, _dot_tn(u[p], bh[p]) + _dot_tn(vb[p], kh[p]), 0.0) for p in pairs]
    s_new = [s[p] * jnp.exp(-total[p]) + _dot(s_hat[p], g_t[p]) + h_t[p] for p in pairs]
    for p in pairs:
        o_ref[0, :, p * LANES:(p + 1) * LANES] = out[p]
        s_ref[p] = s_new[p]


def _wkv(r, k, v, a, b, e, reverse):
    bsz, t, w = r.shape
    c = min(WKV_CHUNK, t)
    nc = t // c
    if reverse:
        idx = lambda i, j: (i, nc - 1 - j, 0)
    else:
        idx = lambda i, j: (i, j, 0)
    spec = pl.BlockSpec((1, c, w), idx)
    return pl.pallas_call(
        functools.partial(_wkv_kernel, reverse=reverse),
        grid=(bsz, nc),
        in_specs=[spec] * 6,
        out_specs=spec,
        out_shape=jax.ShapeDtypeStruct((bsz, t, w), F32),
        scratch_shapes=[pltpu.VMEM((w // LANES, LANES, LANES), F32)],
        compiler_params=_params(("parallel", "arbitrary")),
        name="wkv_bwd" if reverse else "wkv_fwd",
    )(r, k, v, a, b, e)


def _mixout_kernel(of_ref, ob_ref, bonus_ref, g_ref, yf_ref, gates_ref, x_ref, mod_ref,
                   lnw_ref, lnb_ref, hm_ref, wfo_ref, wro_ref, wo_ref, out_ref):
    d = x_ref.shape[2]
    o = of_ref[0] + ob_ref[0]
    dev = o - _head_reduce(o, hm_ref[...])
    var = _head_reduce(dev * dev, hm_ref[...])
    o = dev * lax.rsqrt(var + GN_EPS) * lnw_ref[...] + lnb_ref[...]
    y = (o + bonus_ref[0].astype(F32)) * g_ref[0].astype(F32)
    y_b = _dot(y, wro_ref[...])
    y_a = jnp.dot(yf_ref[0], wfo_ref[...], preferred_element_type=F32)
    gates = gates_ref[0].astype(F32)
    merged = gates[:, :d] * y_a + gates[:, d:] * y_b
    out_ref[0] = x_ref[0] + mod_ref[0, 2:3, :] * _dot(merged, wo_ref[...])


def _mixout(o_f, o_b, bonus, g, yf, gates, x, mod, p):
    b, t, d = x.shape
    tm = _row_tile(t, 512)
    row = lambda a: pl.BlockSpec((1, tm, a.shape[2]), lambda i, j: (i, j, 0))
    consts = [p["lnx_w"], p["lnx_b"], p["head_mean"], p["w_fnet_out"], p["w_rwkv_out"], p["w_o"]]
    acts = [o_f, o_b, bonus, g, yf, gates, x]
    return pl.pallas_call(
        _mixout_kernel,
        grid=(b, t // tm),
        in_specs=[row(a) for a in acts] + [pl.BlockSpec((1, N_MOD, d), lambda i, j: (i, 0, 0))]
                 + [_const_spec(c.shape) for c in consts],
        out_specs=pl.BlockSpec((1, tm, d), lambda i, j: (i, j, 0)),
        out_shape=jax.ShapeDtypeStruct((b, t, d), F32),
        compiler_params=_params(("parallel", "parallel")),
        name="mixout",
    )(*acts, mod, *consts)


def _ffn_kernel(x_ref, mod_ref, ln_ref, lnf_ref, wg_ref, wu_ref, wd_ref, out_ref, *, final_norm):
    x = x_ref[0]
    h = _norm_mod(x, ln_ref[...], mod_ref[0, 3:4, :], mod_ref[0, 4:5, :]).astype(BF16)
    f = wg_ref.shape[1]
    step = f // 4
    acc = None
    for c0 in range(0, f, step):
        gate = jnp.dot(h, wg_ref[:, c0:c0 + step], preferred_element_type=F32)
        up = jnp.dot(h, wu_ref[:, c0:c0 + step], preferred_element_type=F32)
        part = _dot(_silu(gate) * up, wd_ref[c0:c0 + step, :])
        acc = part if acc is None else acc + part
    out = x + mod_ref[0, 5:6, :] * acc
    if final_norm:
        out = _norm_mod(out, lnf_ref[...], 0.0, 0.0)
    out_ref[0] = out


def _ffn_dense(x, mod, ln, ln_final, wg, wu, wd, final_norm):
    b, t, d = x.shape
    tm = _row_tile(t, 512)
    consts = [ln, ln_final, wg, wu, wd]
    return pl.pallas_call(
        functools.partial(_ffn_kernel, final_norm=final_norm),
        grid=(b, t // tm),
        in_specs=[pl.BlockSpec((1, tm, d), lambda i, j: (i, j, 0)),
                  pl.BlockSpec((1, N_MOD, d), lambda i, j: (i, 0, 0))]
                 + [_const_spec(c.shape) for c in consts],
        out_specs=pl.BlockSpec((1, tm, d), lambda i, j: (i, j, 0)),
        out_shape=jax.ShapeDtypeStruct((b, t, d), F32),
        compiler_params=_params(("parallel", "parallel")),
        name="ffn_dense",
    )(x, mod, *consts)


def _route_kernel(x_ref, mod_ref, ln_ref, rw_ref, h_ref, route_ref, cnt_ref):
    h = _norm_mod(x_ref[0], ln_ref[...], mod_ref[0, 3:4, :], mod_ref[0, 4:5, :])
    h_ref[0] = h
    tm = h.shape[0]
    logits = lax.dot_general(rw_ref[...], h, (((1,), (1,)), ((), ())), precision=HIGHEST,
                             preferred_element_type=F32)
    ne = logits.shape[0]
    ei = lax.broadcasted_iota(jnp.int32, logits.shape, 0)
    m1 = jnp.max(logits, axis=0, keepdims=True)
    i1 = jnp.min(jnp.where(logits == m1, ei, ne), axis=0, keepdims=True)
    rest = jnp.where(ei == i1, -jnp.inf, logits)
    m2 = jnp.max(rest, axis=0, keepdims=True)
    i2 = jnp.min(jnp.where(rest == m2, ei, ne), axis=0, keepdims=True)
    t2 = jnp.exp(m2 - m1)
    w1 = 1.0 / (1.0 + t2)
    w2 = t2 / (1.0 + t2)
    oh1 = jnp.where(ei == i1, 1.0, 0.0)
    oh2 = jnp.where(ei == i2, 1.0, 0.0)
    before = (lax.broadcasted_iota(jnp.int32, (tm, tm), 0)
              < lax.broadcasted_iota(jnp.int32, (tm, tm), 1))
    before = jnp.where(before, 1.0, 0.0).astype(BF16)
    cs1 = jnp.dot(oh1.astype(BF16), before, preferred_element_type=F32)
    cs2 = jnp.dot(oh2.astype(BF16), before, preferred_element_type=F32)
    n1 = jnp.sum(oh1, axis=1, keepdims=True)
    n2 = jnp.sum(oh2, axis=1, keepdims=True)
    rank1 = jnp.sum(oh1 * cs1, axis=0, keepdims=True)
    rank2 = jnp.sum(oh2 * (cs2 + n1), axis=0, keepdims=True)
    zero = jnp.zeros_like(w1)
    route_ref[0, 0] = jnp.concatenate(
        [i1.astype(F32), i2.astype(F32), w1, w2, rank1, rank2, zero, zero], axis=0)
    cnt_ref[0, 0] = jnp.broadcast_to(n1 + n2, (ne, LANES))


def _route(x, mod, ln, router_t):
    b, t, d = x.shape
    tm = _row_tile(t, 256)
    nt = t // tm
    ne = router_t.shape[0]
    return pl.pallas_call(
        _route_kernel,
        grid=(b, nt),
        in_specs=[pl.BlockSpec((1, tm, d), lambda i, j: (i, j, 0)),
                  pl.BlockSpec((1, N_MOD, d), lambda i, j: (i, 0, 0)),
                  _const_spec(ln.shape), _const_spec(router_t.shape)],
        out_specs=[pl.BlockSpec((1, tm, d), lambda i, j: (i, j, 0)),
                   pl.BlockSpec((1, 1, 8, tm), lambda i, j: (i, j, 0, 0)),
                   pl.BlockSpec((1, 1, ne, LANES), lambda i, j: (i, j, 0, 0))],
        out_shape=[jax.ShapeDtypeStruct((b, t, d), F32),
                   jax.ShapeDtypeStruct((b, nt, 8, tm), F32),
                   jax.ShapeDtypeStruct((b, nt, ne, LANES), F32)],
        compiler_params=_params(("parallel", "parallel")),
        name="moe_route",
    )(x, mod, ln, router_t)


def _row_copy(src, dst, sem):
    return pltpu.make_async_copy(src, dst, sem)


def _dispatch_kernel(dest_ref, h_ref, rows_in_ref, rows_ref, sem):
    del rows_in_ref
    tm = h_ref.shape[0]

    def start(i, _):
        for s in range(2):
            _row_copy(h_ref.at[pl.ds(i, 1)], rows_ref.at[pl.ds(dest_ref[0, 0, 2 * i + s], 1)],
                      sem.at[s]).start()
        return 0

    def wait(i, _):
        for s in range(2):
            _row_copy(h_ref.at[pl.ds(i, 1)], rows_ref.at[pl.ds(0, 1)], sem.at[s]).wait()
        return 0

    lax.fori_loop(0, tm, start, 0)
    lax.fori_loop(0, tm, wait, 0)


def _dispatch(h, dest, n_rows):
    n, d = h.shape
    tm = dest.shape[2] // 2
    rows = jnp.zeros((n_rows, d), h.dtype)
    return pl.pallas_call(
        _dispatch_kernel,
        grid=(n // tm,),
        in_specs=[pl.BlockSpec((1, 1, 2 * tm), lambda i: (i, 0, 0), memory_space=pltpu.SMEM),
                  pl.BlockSpec((tm, d), lambda i: (i, 0)),
                  pl.BlockSpec(memory_space=pl.ANY)],
        out_specs=pl.BlockSpec(memory_space=pl.ANY),
        out_shape=jax.ShapeDtypeStruct((n_rows, d), h.dtype),
        scratch_shapes=[pltpu.SemaphoreType.DMA((2,))],
        input_output_aliases={2: 0},
        compiler_params=pltpu.CompilerParams(dimension_semantics=("arbitrary",),
                                             vmem_limit_bytes=VMEM_LIMIT, has_side_effects=True),
        name="moe_dispatch",
    )(dest, h, rows)


def _experts_kernel(be_ref, nb_ref, x_ref, wg_ref, wu_ref, wd_ref, y_ref):
    del be_ref

    @pl.when(pl.program_id(0) < nb_ref[0])
    def _():
        x = x_ref[...].astype(BF16)
        f = wg_ref.shape[2]
        step = f // 7
        acc = None
        for c0 in range(0, f, step):
            gate = jnp.dot(x, wg_ref[0, :, c0:c0 + step], preferred_element_type=F32)
            up = jnp.dot(x, wu_ref[0, :, c0:c0 + step], preferred_element_type=F32)
            part = _dot(_silu(gate) * up, wd_ref[0, c0:c0 + step, :])
            acc = part if acc is None else acc + part
        y_ref[...] = acc

    @pl.when(pl.program_id(0) >= nb_ref[0])
    def _():
        y_ref[...] = jnp.zeros_like(y_ref)


def _experts(rows, block_e, n_used, wg, wu, wd):
    n_rows, d = rows.shape
    rb = MOE_ROW_BLOCK
    f = wg.shape[2]
    wspec = lambda shape: pl.BlockSpec((1,) + shape, lambda i, be, nb: (be[i], 0, 0),
                                       pipeline_mode=pl.Buffered(1))
    return pl.pallas_call(
        _experts_kernel,
        grid_spec=pltpu.PrefetchScalarGridSpec(
            num_scalar_prefetch=2,
            grid=(n_rows // rb,),
            in_specs=[pl.BlockSpec((rb, d), lambda i, be, nb: (i, 0)),
                      wspec((d, f)), wspec((d, f)), wspec((f, d))],
            out_specs=pl.BlockSpec((rb, d), lambda i, be, nb: (i, 0))),
        out_shape=jax.ShapeDtypeStruct((n_rows, d), F32),
        compiler_params=_params(("arbitrary",)),
        name="moe_experts",
    )(block_e, n_used, rows, wg, wu, wd)


def _combine_kernel(dest_ref, y_hbm, x_ref, wts_ref, mod_ref, lnf_ref, out_ref, buf, sem,
                    *, final_norm):
    tm = x_ref.shape[1]

    def start(i, _):
        for s in range(2):
            _row_copy(y_hbm.at[pl.ds(dest_ref[0, 0, 2 * i + s], 1)], buf.at[s, pl.ds(i, 1)],
                      sem.at[s]).start()
        return 0

    def wait(i, _):
        for s in range(2):
            _row_copy(y_hbm.at[pl.ds(0, 1)], buf.at[s, pl.ds(i, 1)], sem.at[s]).wait()
        return 0

    lax.fori_loop(0, tm, start, 0)
    lax.fori_loop(0, tm, wait, 0)
    f = wts_ref[0, :, 0:1] * buf[0] + wts_ref[0, :, 1:2] * buf[1]
    out = x_ref[0] + mod_ref[0, 5:6, :] * f
    if final_norm:
        out = _norm_mod(out, lnf_ref[...], 0.0, 0.0)
    out_ref[0] = out


def _combine(y_rows, dest, wts, x, mod, ln_final, final_norm):
    b, t, d = x.shape
    tm = dest.shape[2] // 2
    nt = t // tm
    return pl.pallas_call(
        functools.partial(_combine_kernel, final_norm=final_norm),
        grid=(b, nt),
        in_specs=[pl.BlockSpec((1, 1, 2 * tm), lambda i, j: (i * nt + j, 0, 0),
                               memory_space=pltpu.SMEM),
                  pl.BlockSpec(memory_space=pl.ANY),
                  pl.BlockSpec((1, tm, d), lambda i, j: (i, j, 0)),
                  pl.BlockSpec((1, tm, 2), lambda i, j: (i, j, 0)),
                  pl.BlockSpec((1, N_MOD, d), lambda i, j: (i, 0, 0)),
                  _const_spec(ln_final.shape)],
        out_specs=pl.BlockSpec((1, tm, d), lambda i, j: (i, j, 0)),
        out_shape=jax.ShapeDtypeStruct((b, t, d), F32),
        scratch_shapes=[pltpu.VMEM((2, tm, d), F32), pltpu.SemaphoreType.DMA((2,))],
        compiler_params=_params(("arbitrary", "arbitrary")),
        name="moe_combine",
    )(dest, y_rows, x, wts, mod, ln_final)


def _ffn_moe(x, mod, ln, ln_final, router_t, wg, wu, wd, final_norm):
    b, t, d = x.shape
    n = b * t
    ne = router_t.shape[0]
    rb = MOE_ROW_BLOCK
    h, route, cnt = _route(x, mod, ln, router_t)
    nt, tm = route.shape[1], route.shape[3]
    tile_cnt = cnt[..., 0].reshape(b * nt, ne).astype(jnp.int32)
    totals = jnp.sum(tile_cnt, axis=0)
    padded = ((totals + rb - 1) // rb) * rb
    padded_end = jnp.cumsum(padded)
    base = (padded_end - padded)[None, :] + jnp.cumsum(tile_cnt, axis=0) - tile_cnt
    route = route.reshape(b * nt, 8, tm)
    experts = route[:, 0:2, :].astype(jnp.int32)
    rank = route[:, 4:6, :].astype(jnp.int32)
    onehot = experts[..., None] == jnp.arange(ne, dtype=jnp.int32)
    dest = jnp.sum(jnp.where(onehot, base[:, None, None, :], 0), axis=-1) + rank
    dest = jnp.swapaxes(dest, 1, 2).reshape(b * nt, 1, 2 * tm)
    wts = jnp.swapaxes(route[:, 2:4, :], 1, 2).reshape(b, t, 2)
    n_blocks = -(-(n * 2) // rb) + ne
    block_e = jnp.minimum(
        jnp.searchsorted(padded_end, jnp.arange(n_blocks, dtype=jnp.int32) * rb, side="right"),
        ne - 1).astype(jnp.int32)
    n_used = (padded_end[-1:] // rb).astype(jnp.int32)
    rows = _dispatch(h.reshape(n, d), dest, n_blocks * rb)
    y_rows = _experts(rows, block_e, n_used, wg, wu, wd)
    return _combine(y_rows, dest, wts, x, mod, ln_final, final_norm)


def _block_diag(m, n):
    return jnp.kron(jnp.eye(n, dtype=m.dtype), m)


def _layer_params(i, w_in, conv_w, decay_w0, decay_w2, iclr_a0, iclr_a2, gate_g2, k_k, k_a, r_k,
                  lnx_w, lnx_b, w_fnet_out, w_rwkv_out, w_o):
    w = RWKV_WIDTH
    lora = decay_w2.shape[2]
    tail0 = FNET_WIDTH + 3 * w

    def reorder_tail(m):
        pad = jnp.zeros(m.shape[:-1] + (LANES - lora,), m.dtype)
        return jnp.concatenate([m[..., :3 * lora], pad, m[..., 3 * lora:]], axis=-1)

    row = lambda v: v.reshape(1, -1)
    wi = w_in[i]
    rwkv_cols = conv_w.shape[2]
    zeros = jnp.zeros((lora, w), F32)
    return {
        "w_a": wi[:, :FNET_WIDTH].astype(BF16),
        "w_main": wi[:, FNET_WIDTH:tail0].astype(BF16),
        "w_tail": reorder_tail(wi[:, tail0:FNET_WIDTH + rwkv_cols]).astype(BF16),
        "w_gates": wi[:, FNET_WIDTH + rwkv_cols:].astype(BF16),
        "conv_main": conv_w[i][:, :3 * w],
        "conv_tail": reorder_tail(conv_w[i][:, 3 * w:]),
        "decay_w0": decay_w0[i].reshape(1, 2 * w),
        "decay_w2": jnp.concatenate(
            [jnp.concatenate([decay_w2[i, 0], zeros], axis=1),
             jnp.concatenate([zeros, decay_w2[i, 1]], axis=1)], axis=0),
        "iclr_a0": row(iclr_a0[i]),
        "iclr_a2": jnp.concatenate([iclr_a2[i], jnp.zeros((LANES - lora, w), F32)], axis=0),
        "gate_g2": gate_g2[i],
        "k_k": row(k_k[i]), "k_a": row(k_a[i]), "r_k": row(r_k[i]),
        "lnx_w": row(lnx_w[i]), "lnx_b": row(lnx_b[i]),
        "head_sum": _block_diag(jnp.ones((HEAD_DIM, HEAD_DIM), BF16), LANES // HEAD_DIM),
        "head_mean": _block_diag(jnp.full((HEAD_DIM, HEAD_DIM), 1.0 / HEAD_DIM, BF16),
                                 LANES // HEAD_DIM),
        "w_fnet_out": w_fnet_out[i].astype(BF16),
        "w_rwkv_out": w_rwkv_out[i].astype(BF16),
        "w_o": w_o[i].astype(BF16),
    }


def _channel_dft():
    k = jnp.arange(HEAD_DIM, dtype=jnp.int32)
    ang = ((k[:, None] * k[None, :]) % HEAD_DIM).astype(F32) * (2.0 * math.pi / HEAD_DIM)
    groups = FNET_WIDTH // HEAD_DIM
    return jnp.concatenate([_block_diag(jnp.cos(ang), groups), _block_diag(jnp.sin(ang), groups)],
                           axis=1).astype(BF16)


def _token_mixer(x, mod, ln, p, cs, ct, st):
    zc, zs, um, ut, gates = _inproj(x, mod, ln, p["w_a"], p["w_main"], p["w_tail"], p["w_gates"], cs)
    yf = _fnet(zc, zs, ct, st)
    r, k, v, a, b, e_f, e_b, g, bonus = _prep(um, ut, p)
    o_f = _wkv(r, k, v, a, b, e_f, False)
    o_b = _wkv(r, k, v, a, b, e_b, True)
    return _mixout(o_f, o_b, bonus, g, yf, gates, x, mod, p)


def kernel(x_prompt, x_sample, c_prompt, c_sample, w_ada, b_ada, ln_mix, w_in, conv_w, decay_w0, decay_w2, iclr_a0, iclr_a2, gate_g2, k_k, k_a, r_k, lnx_w, lnx_b, w_fnet_out, w_rwkv_out, w_o, ln_ffn, ff_w_gate, ff_w_up, ff_w_down, router_w, moe_w_gate, moe_w_up, moe_w_down, ln_final):
    depth, d = ln_mix.shape
    streams = [x_prompt, x_sample]
    nb = [x.shape[0] for x in streams]
    mod_all = _ada(jnp.concatenate([c_prompt, c_sample], axis=0), w_ada, b_ada)
    cs = _channel_dft()
    dfts = {}
    for x in streams:
        if x.shape[1] not in dfts:
            dfts[x.shape[1]] = _dft_matrices(x.shape[1])
    lnf = ln_final.reshape(1, d)
    for i in range(depth):
        p = _layer_params(i, w_in, conv_w, decay_w0, decay_w2, iclr_a0, iclr_a2, gate_g2, k_k, k_a,
                          r_k, lnx_w, lnx_b, w_fnet_out, w_rwkv_out, w_o)
        j = i // 2
        final = i == depth - 1
        if i % 2 == 0:
            ffw = (ff_w_gate[j].astype(BF16), ff_w_up[j].astype(BF16), ff_w_down[j].astype(BF16))
        else:
            ffw = (moe_w_gate[j].astype(BF16), moe_w_up[j].astype(BF16), moe_w_down[j].astype(BF16))
            router_t = router_w[j].T
        off = 0
        for s, x in enumerate(streams):
            mod = mod_all[i, off:off + nb[s]].reshape(nb[s], N_MOD, d)
            off += nb[s]
            x = _token_mixer(x, mod, ln_mix[i].reshape(1, d), p, cs, *dfts[x.shape[1]])
            ln2 = ln_ffn[i].reshape(1, d)
            if i % 2 == 0:
                x = _ffn_dense(x, mod, ln2, lnf, *ffw, final)
            else:
                x = _ffn_moe(x, mod, ln2, lnf, router_t, *ffw, final)
            streams[s] = x
    return tuple(streams)
```

```python
import functools
import math

import jax
import jax.numpy as jnp
from jax import lax
from jax.experimental import pallas as pl
from jax.experimental.pallas import tpu as pltpu

F32 = jnp.float32
BF16 = jnp.bfloat16
HIGHEST = lax.Precision.HIGHEST

HEAD_DIM = 64
LANES = 128
FNET_WIDTH = 256
RWKV_WIDTH = 768
LORA_TAIL = 384
N_MOD = 6
N_EXPERTS = 8
RMS_EPS = 1e-6
GN_EPS = 64e-5
WKV_CHUNK = 128
MOE_ROW_BLOCK = 512
VMEM_LIMIT = 56 * 1024 * 1024


def _params(sem, vmem=VMEM_LIMIT):
    return pltpu.CompilerParams(dimension_semantics=sem, vmem_limit_bytes=vmem)


def _const_spec(shape):
    nd = len(shape)
    return pl.BlockSpec(shape, lambda *_: (0,) * nd, pipeline_mode=pl.Buffered(1))


def _dot(a, b):
    return jnp.dot(a.astype(BF16), b.astype(BF16), preferred_element_type=F32)


def _dot_nt(a, b):
    return lax.dot_general(a.astype(BF16), b.astype(BF16), (((1,), (1,)), ((), ())),
                           preferred_element_type=F32)


def _dot_tn(a, b):
    return lax.dot_general(a.astype(BF16), b.astype(BF16), (((0,), (0,)), ((), ())),
                           preferred_element_type=F32)


def _split_bf16(x, parts):
    out, rem = [], x
    for _ in range(parts):
        p = rem.astype(BF16)
        out.append(p)
        rem = rem - p.astype(F32)
    return out


def _dot_exact_rhs(x, m, parts):
    acc = None
    for p in _split_bf16(x, parts):
        t = jnp.dot(p, m, preferred_element_type=F32)
        acc = t if acc is None else acc + t
    return acc


def _dot_exact_lhs(m, x, parts):
    acc = None
    for p in _split_bf16(x, parts):
        t = jnp.dot(m, p, preferred_element_type=F32)
        acc = t if acc is None else acc + t
    return acc


def _dot_f32(a, b):
    return jnp.dot(a, b, precision=HIGHEST, preferred_element_type=F32)


def _dot_3pass(a, w_ref):
    ah, al = _split_bf16(a, 2)
    d = lambda x, y: jnp.dot(x, y, preferred_element_type=F32)
    return d(ah, w_ref[0]) + (d(al, w_ref[0]) + d(ah, w_ref[1]))


def _norm_mod(x, gain, shift, scale):
    ms = jnp.mean(x * x, axis=-1, keepdims=True)
    return x * lax.rsqrt(ms + RMS_EPS) * gain * (1.0 + scale) + shift


def _silu(x):
    return x * jax.nn.sigmoid(x)


def _head_reduce(x, m):
    cols = [_dot_exact_rhs(x[:, j:j + LANES], m, 2) for j in range(0, x.shape[1], LANES)]
    return jnp.concatenate(cols, axis=1)


def _row_tile(t, want):
    tm = min(t, want)
    assert t % tm == 0
    return tm


def _ada_kernel(c_ref, w_ref, b_ref, o_ref):
    o_ref[0] = _dot_f32(_silu(c_ref[...]), w_ref[0]) + b_ref[0]


def _ada(c_all, w_ada, b_ada):
    depth, d, n = w_ada.shape
    bc = c_all.shape[0]
    tn = n // 4
    return pl.pallas_call(
        _ada_kernel,
        grid=(depth, n // tn),
        in_specs=[pl.BlockSpec((bc, d), lambda l, j: (0, 0)),
                  pl.BlockSpec((1, d, tn), lambda l, j: (l, 0, j)),
                  pl.BlockSpec((1, 1, tn), lambda l, j: (l, 0, j))],
        out_specs=pl.BlockSpec((1, bc, tn), lambda l, j: (l, 0, j)),
        out_shape=jax.ShapeDtypeStruct((depth, bc, n), F32),
        compiler_params=_params(("parallel", "parallel")),
        name="ada",
    )(c_all, w_ada, b_ada.reshape(depth, 1, n))


def _inproj_kernel(x_ref, mod_ref, ln_ref, wa_ref, wm_ref, wt_ref, wg_ref, cs_ref,
                   zc_ref, zs_ref, um_ref, ut_ref, g_ref):
    h = _norm_mod(x_ref[0], ln_ref[...], mod_ref[0, 0:1, :], mod_ref[0, 1:2, :]).astype(BF16)
    ua = jnp.dot(h, wa_ref[...], preferred_element_type=F32)
    zz = jnp.dot(ua.astype(BF16), cs_ref[...], preferred_element_type=F32)
    zc_ref[0] = zz[:, :FNET_WIDTH].astype(BF16)
    zs_ref[0] = zz[:, FNET_WIDTH:].astype(BF16)
    step = 384
    for c0 in range(0, wm_ref.shape[1], step):
        um_ref[0, :, c0:c0 + step] = jnp.dot(
            h, wm_ref[:, c0:c0 + step], preferred_element_type=F32).astype(BF16)
    ut_ref[0] = jnp.dot(h, wt_ref[...], preferred_element_type=F32)
    step = 512
    for c0 in range(0, wg_ref.shape[1], step):
        g_ref[0, :, c0:c0 + step] = jax.nn.sigmoid(
            jnp.dot(h, wg_ref[:, c0:c0 + step], preferred_element_type=F32)).astype(BF16)


def _inproj(x, mod, ln, wa, wm, wt, wg, cs):
    b, t, d = x.shape
    tm = _row_tile(t, 512)
    row = lambda w: pl.BlockSpec((1, tm, w), lambda i, j: (i, j, 0))
    outs = [(FNET_WIDTH, BF16), (FNET_WIDTH, BF16), (wm.shape[1], BF16), (wt.shape[1], F32),
            (wg.shape[1], BF16)]
    return pl.pallas_call(
        _inproj_kernel,
        grid=(b, t // tm),
        in_specs=[row(d), pl.BlockSpec((1, N_MOD, d), lambda i, j: (i, 0, 0)),
                  _const_spec(ln.shape), _const_spec(wa.shape), _const_spec(wm.shape),
                  _const_spec(wt.shape), _const_spec(wg.shape), _const_spec(cs.shape)],
        out_specs=[row(w) for w, _ in outs],
        out_shape=[jax.ShapeDtypeStruct((b, t, w), dt) for w, dt in outs],
        compiler_params=_params(("parallel", "parallel")),
        name="inproj",
    )(x, mod, ln, wa, wm, wt, wg, cs)


def _fnet_kernel(ct_ref, st_ref, zc_ref, zs_ref, o_ref, *, scale):
    y = (jnp.dot(ct_ref[...], zc_ref[0], preferred_element_type=F32)
         - jnp.dot(st_ref[...], zs_ref[0], preferred_element_type=F32))
    o_ref[0] = (y * scale).astype(BF16)


def _dft_matrices(t):
    rr = 64
    k = lax.iota(jnp.int32, t)

    def table(rows):
        ang = ((rows[:, None] * k[None, :]) % t).astype(F32) * (2.0 * math.pi / t)
        return jnp.cos(ang), jnp.sin(ang)

    ca, sa = table(lax.iota(jnp.int32, t // rr) * rr)
    cb, sb = table(lax.iota(jnp.int32, rr))
    ca, sa, cb, sb = ca[:, None, :], sa[:, None, :], cb[None], sb[None]
    return ((ca * cb - sa * sb).reshape(t, t).astype(BF16),
            (sa * cb + ca * sb).reshape(t, t).astype(BF16))


def _fnet(zc, zs, ct, st):
    b, t, w = zc.shape
    tm = _row_tile(t, max(128, (2 * 1024 * 1024) // t))
    return pl.pallas_call(
        functools.partial(_fnet_kernel, scale=1.0 / math.sqrt(t * HEAD_DIM)),
        grid=(b, t // tm),
        in_specs=[pl.BlockSpec((tm, t), lambda i, j: (j, 0)),
                  pl.BlockSpec((tm, t), lambda i, j: (j, 0)),
                  pl.BlockSpec((1, t, w), lambda i, j: (i, 0, 0)),
                  pl.BlockSpec((1, t, w), lambda i, j: (i, 0, 0))],
        out_specs=pl.BlockSpec((1, tm, w), lambda i, j: (i, j, 0)),
        out_shape=jax.ShapeDtypeStruct((b, t, w), BF16),
        compiler_params=_params(("parallel", "arbitrary")),
        name="fnet",
    )(ct, st, zc, zs)


def _conv3(main, prev_row, next_row, w):
    tm = main.shape[0]
    ri = lax.broadcasted_iota(jnp.int32, main.shape, 0)
    up = jnp.where(ri == 0, prev_row, pltpu.roll(main, 1, 0))
    dn = jnp.where(ri == tm - 1, next_row, pltpu.roll(main, tm - 1, 0))
    return w[0:1] * up + w[1:2] * main + w[2:3] * dn


def _prep_kernel(um_ref, ump_ref, umn_ref, ut_ref, utp_ref, utn_ref, cwm_ref, cwt_ref,
                 w0_ref, w2_ref, a0_ref, a2_ref, g2_ref, kk_ref, ka_ref, rk_ref, hs_ref,
                 r_ref, k_ref, v_ref, a_ref, b_ref, ef_ref, eb_ref, g_ref, bonus_ref):
    j = pl.program_id(1)
    first = j == 0
    last = j == pl.num_programs(1) - 1
    hp = ump_ref.shape[1]

    def conv_main(c0, c1):
        prev = jnp.where(first, 0.0, ump_ref[0, hp - 1:hp, c0:c1].astype(F32))
        nxt = jnp.where(last, 0.0, umn_ref[0, 0:1, c0:c1].astype(F32))
        return _conv3(um_ref[0, :, c0:c1].astype(F32), prev, nxt, cwm_ref[:, c0:c1])

    w = RWKV_WIDTH
    r = conv_main(0, w)
    k = conv_main(w, 2 * w)
    v = conv_main(2 * w, 3 * w)
    tp = utp_ref.shape[1]
    tail = _conv3(ut_ref[0], jnp.where(first, 0.0, utp_ref[0, tp - 1:tp, :]),
                  jnp.where(last, 0.0, utn_ref[0, 0:1, :]), cwt_ref[...])
    xw, xa, xg = tail[:, 0:LANES], tail[:, LANES:2 * LANES], tail[:, 2 * LANES:3 * LANES]

    e = jax.nn.sigmoid(w0_ref[...] + _dot_3pass(jnp.tanh(xw), w2_ref)) * math.exp(-0.5)
    ef_ref[0] = e[:, :w]
    eb_ref[0] = e[:, w:]
    a = jax.nn.sigmoid(a0_ref[...] + _dot_3pass(xa, a2_ref))
    g_ref[0] = _dot_3pass(jax.nn.sigmoid(xg), g2_ref).astype(BF16)

    kk = k * kk_ref[...]
    kk = kk * lax.rsqrt(_head_reduce(kk * kk, hs_ref[...]) + 1e-12)
    k = k * (1.0 + (a - 1.0) * ka_ref[...])
    r_ref[0] = r.astype(BF16)
    k_ref[0] = k.astype(BF16)
    v_ref[0] = v.astype(BF16)
    a_ref[0] = (-kk).astype(BF16)
    b_ref[0] = (kk * a).astype(BF16)
    bonus_ref[0] = (_head_reduce(r * k * rk_ref[...], hs_ref[...]) * v).astype(BF16)


def _prep(um, ut, p):
    b, t, wm = um.shape
    wt = ut.shape[2]
    tm = _row_tile(t, 256)
    hm, ht = 16, 8
    nm, nt = tm // hm, tm // ht
    main = lambda w: pl.BlockSpec((1, tm, w), lambda i, j: (i, j, 0))
    prev = lambda h, n, w: pl.BlockSpec((1, h, w), lambda i, j: (i, jnp.maximum(j * n - 1, 0), 0))
    nxt = lambda h, n, w: pl.BlockSpec(
        (1, h, w), lambda i, j: (i, jnp.minimum((j + 1) * n, t // h - 1), 0))
    consts = [p["conv_main"], p["conv_tail"], p["decay_w0"], p["decay_w2"], p["iclr_a0"],
              p["iclr_a2"], p["gate_g2"], p["k_k"], p["k_a"], p["r_k"], p["head_sum"]]
    outs = [BF16] * 5 + [F32, F32, BF16, BF16]
    return pl.pallas_call(
        _prep_kernel,
        grid=(b, t // tm),
        in_specs=[main(wm), prev(hm, nm, wm), nxt(hm, nm, wm),
                  main(wt), prev(ht, nt, wt), nxt(ht, nt, wt)]
                 + [_const_spec(c.shape) for c in consts],
        out_specs=[main(RWKV_WIDTH) for _ in outs],
        out_shape=[jax.ShapeDtypeStruct((b, t, RWKV_WIDTH), dt) for dt in outs],
        compiler_params=_params(("parallel", "parallel")),
        name="rwkv_prep",
    )(um, um, um, ut, ut, ut, *consts)


def _wkv_kernel(r_ref, k_ref, v_ref, a_ref, b_ref, e_ref, o_ref, s_ref, *, reverse):
    c = r_ref.shape[1]
    pairs = range(s_ref.shape[0])
    heads = range(LANES // HEAD_DIM)
    chains = [(p, h) for p in pairs for h in heads]

    @pl.when(pl.program_id(1) == 0)
    def _():
        s_ref[...] = jnp.zeros_like(s_ref)

    row = lax.broadcasted_iota(jnp.int32, (c, c), 0)
    col = lax.broadcasted_iota(jnp.int32, (c, c), 1)
    if reverse:
        incl, strict = col >= row, col > row
    else:
        incl, strict = col <= row, col < row
    tri = jnp.where(incl, 1.0, 0.0).astype(BF16)
    end = 0 if reverse else c - 1
    lane = lax.broadcasted_iota(jnp.int32, (1, LANES), 1)
    in_head = [(lane >= h * HEAD_DIM) & (lane < (h + 1) * HEAD_DIM) for h in heads]
    same_head = (lax.broadcasted_iota(jnp.int32, (LANES, LANES), 0) // HEAD_DIM
                 == lax.broadcasted_iota(jnp.int32, (LANES, LANES), 1) // HEAD_DIM)

    per_row = r_ref.shape[2] // LANES

    def tile(ref, p):
        return ref[p // per_row, :, (p % per_row) * LANES:(p % per_row + 1) * LANES]

    cum = [_dot_exact_lhs(tri, tile(e_ref, p), 3) for p in pairs]
    total = [x[end:end + 1, :] for x in cum]
    ref = [x[c // 2:c // 2 + 1, :] for x in cum]
    dl = [cum[p] - ref[p] for p in pairs]
    g_inv = [jnp.exp(dl[p]) for p in pairs]
    at = [tile(a_ref, p).astype(F32) * jnp.exp(tile(e_ref, p) - dl[p]) for p in pairs]
    rt = [tile(r_ref, p).astype(F32) * jnp.exp(-dl[p]) for p in pairs]
    bt = [tile(b_ref, p).astype(F32) * g_inv[p] for p in pairs]
    kt = [tile(k_ref, p).astype(F32) * g_inv[p] for p in pairs]
    g_end = [jnp.exp(ref[p] - total[p]) for p in pairs]
    bh = [(bt[p] * g_end[p]).astype(BF16) for p in pairs]
    kh = [(kt[p] * g_end[p]).astype(BF16) for p in pairs]
    bk = [jnp.concatenate([bt[p], kt[p]], axis=0).astype(BF16) for p in pairs]
    ar = [jnp.concatenate([at[p], rt[p]], axis=0) for p in pairs]
    vb = [tile(v_ref, p) for p in pairs]

    prod = [_dot_nt(jnp.where(in_head[h], ar[p], 0.0), bk[p]) for p, h in chains]
    l_ab = [jnp.where(strict, x[:c, :c], 0.0) for x in prod]
    l_ak = [jnp.where(strict, x[:c, c:], 0.0) for x in prod]
    m_rb = [jnp.where(incl, x[c:, :c], 0.0).astype(BF16) for x in prod]
    m_rk = [jnp.where(incl, x[c:, c:], 0.0).astype(BF16) for x in prod]
    del prod
    eye = jnp.where(row == col, 1.0, 0.0)
    t = [eye + jnp.where((row >> 1) == (col >> 1), x, 0.0) for x in l_ab]
    for sh in range(1, (c - 1).bit_length()):
        joins = ((row >> (sh + 1)) == (col >> (sh + 1))) & ((row >> sh) != (col >> sh))
        et = [_dot(jnp.where(joins, l_ab[i], 0.0), t[i]) for i in range(len(chains))]
        t = [t[i] + _dot(t[i], et[i]) for i in range(len(chains))]
    lmv = [jnp.dot(jnp.concatenate([l_ak[i].astype(BF16), m_rk[i]], axis=0), vb[p],
                   preferred_element_type=F32) for i, (p, h) in enumerate(chains)]
    lakv = [v[:c] for v in lmv]
    mrkv = [v[c:] for v in lmv]
    x = [_dot(t[i], jnp.concatenate([at[p], lakv[i]], axis=1)) for i, (p, h) in enumerate(chains)]
    y = [jnp.dot(m_rb[i], x[i].astype(BF16), preferred_element_type=F32) for i in range(len(chains))]

    def pick(vals, p):
        out = vals[p * len(heads)]
        for h in heads[1:]:
            out = jnp.where(in_head[h], vals[p * len(heads) + h], out)
        return out

    a_solved = [pick([v[:, :LANES] for v in x], p) for p in pairs]
    u = [pick([v[:, LANES:] for v in x], p) for p in pairs]
    r_hat = [rt[p] + pick([v[:, :LANES] for v in y], p) for p in pairs]
    o_intra = [pick([y[i][:, LANES:] + mrkv[i] for i in range(len(chains))], p) for p in pairs]
    s = [s_ref[p] for p in pairs]
    s_hat = [(s[p] * jnp.exp(-ref[p])).astype(BF16) for p in pairs]
    out = [_dot_nt(r_hat[p], s_hat[p]) + o_intra[p] for p in pairs]
    g_t = [jnp.where(same_head, _dot_tn(a_solved[p], bh[p]), 0.0) for p in pairs]
    h_t = [jnp.where(same_head, _dot_tn(u[p], bh[p]) + _dot_tn(vb[p], kh[p]), 0.0) for p in pairs]
    s_new = [s[p] * jnp.exp(-total[p]) + _dot(s_hat[p], g_t[p]) + h_t[p] for p in pairs]
    for p in pairs:
        o_ref[p // per_row, :, (p % per_row) * LANES:(p % per_row + 1) * LANES] = out[p]
        s_ref[p] = s_new[p]


def _wkv(r, k, v, a, b, e, reverse):
    bsz, t, w = r.shape
    c = min(WKV_CHUNK, t)
    nc = t // c
    rows = 2 if bsz % 2 == 0 else 1
    if reverse:
        idx = lambda i, j: (i, nc - 1 - j, 0)
    else:
        idx = lambda i, j: (i, j, 0)
    spec = pl.BlockSpec((rows, c, w), idx)
    return pl.pallas_call(
        functools.partial(_wkv_kernel, reverse=reverse),
        grid=(bsz // rows, nc),
        in_specs=[spec] * 6,
        out_specs=spec,
        out_shape=jax.ShapeDtypeStruct((bsz, t, w), F32),
        scratch_shapes=[pltpu.VMEM((rows * (w // LANES), LANES, LANES), F32)],
        compiler_params=_params(("parallel", "arbitrary")),
        name="wkv_bwd" if reverse else "wkv_fwd",
    )(r, k, v, a, b, e)


def _mixout_kernel(of_ref, ob_ref, bonus_ref, g_ref, yf_ref, gates_ref, x_ref, mod_ref,
                   lnw_ref, lnb_ref, hm_ref, wfo_ref, wro_ref, wo_ref, out_ref):
    d = x_ref.shape[2]
    o = of_ref[0] + ob_ref[0]
    dev = o - _head_reduce(o, hm_ref[...])
    var = _head_reduce(dev * dev, hm_ref[...])
    o = dev * lax.rsqrt(var + GN_EPS) * lnw_ref[...] + lnb_ref[...]
    y = (o + bonus_ref[0].astype(F32)) * g_ref[0].astype(F32)
    y_b = _dot(y, wro_ref[...])
    y_a = jnp.dot(yf_ref[0], wfo_ref[...], preferred_element_type=F32)
    gates = gates_ref[0].astype(F32)
    merged = gates[:, :d] * y_a + gates[:, d:] * y_b
    out_ref[0] = x_ref[0] + mod_ref[0, 2:3, :] * _dot(merged, wo_ref[...])


def _mixout(o_f, o_b, bonus, g, yf, gates, x, mod, p):
    b, t, d = x.shape
    tm = _row_tile(t, 512)
    row = lambda a: pl.BlockSpec((1, tm, a.shape[2]), lambda i, j: (i, j, 0))
    consts = [p["lnx_w"], p["lnx_b"], p["head_mean"], p["w_fnet_out"], p["w_rwkv_out"], p["w_o"]]
    acts = [o_f, o_b, bonus, g, yf, gates, x]
    return pl.pallas_call(
        _mixout_kernel,
        grid=(b, t // tm),
        in_specs=[row(a) for a in acts] + [pl.BlockSpec((1, N_MOD, d), lambda i, j: (i, 0, 0))]
                 + [_const_spec(c.shape) for c in consts],
        out_specs=pl.BlockSpec((1, tm, d), lambda i, j: (i, j, 0)),
        out_shape=jax.ShapeDtypeStruct((b, t, d), F32),
        compiler_params=_params(("parallel", "parallel")),
        name="mixout",
    )(*acts, mod, *consts)


def _ffn_kernel(x_ref, mod_ref, ln_ref, lnf_ref, wg_ref, wu_ref, wd_ref, out_ref, *, final_norm):
    x = x_ref[0]
    h = _norm_mod(x, ln_ref[...], mod_ref[0, 3:4, :], mod_ref[0, 4:5, :]).astype(BF16)
    f = wg_ref.shape[1]
    step = f // 4
    acc = None
    for c0 in range(0, f, step):
        gate = jnp.dot(h, wg_ref[:, c0:c0 + step], preferred_element_type=F32)
        up = jnp.dot(h, wu_ref[:, c0:c0 + step], preferred_element_type=F32)
        part = _dot(_silu(gate) * up, wd_ref[c0:c0 + step, :])
        acc = part if acc is None else acc + part
    out = x + mod_ref[0, 5:6, :] * acc
    if final_norm:
        out = _norm_mod(out, lnf_ref[...], 0.0, 0.0)
    out_ref[0] = out


def _ffn_dense(x, mod, ln, ln_final, wg, wu, wd, final_norm):
    b, t, d = x.shape
    tm = _row_tile(t, 512)
    consts = [ln, ln_final, wg, wu, wd]
    return pl.pallas_call(
        functools.partial(_ffn_kernel, final_norm=final_norm),
        grid=(b, t // tm),
        in_specs=[pl.BlockSpec((1, tm, d), lambda i, j: (i, j, 0)),
                  pl.BlockSpec((1, N_MOD, d), lambda i, j: (i, 0, 0))]
                 + [_const_spec(c.shape) for c in consts],
        out_specs=pl.BlockSpec((1, tm, d), lambda i, j: (i, j, 0)),
        out_shape=jax.ShapeDtypeStruct((b, t, d), F32),
        compiler_params=_params(("parallel", "parallel")),
        name="ffn_dense",
    )(x, mod, *consts)


def _route_kernel(x_ref, mod_ref, ln_ref, rw_ref, h_ref, route_ref, cnt_ref):
    h = _norm_mod(x_ref[0], ln_ref[...], mod_ref[0, 3:4, :], mod_ref[0, 4:5, :])
    h_ref[0] = h
    tm = h.shape[0]
    logits = lax.dot_general(rw_ref[...], h, (((1,), (1,)), ((), ())), precision=HIGHEST,
                             preferred_element_type=F32)
    ne = logits.shape[0]
    ei = lax.broadcasted_iota(jnp.int32, logits.shape, 0)
    m1 = jnp.max(logits, axis=0, keepdims=True)
    i1 = jnp.min(jnp.where(logits == m1, ei, ne), axis=0, keepdims=True)
    rest = jnp.where(ei == i1, -jnp.inf, logits)
    m2 = jnp.max(rest, axis=0, keepdims=True)
    i2 = jnp.min(jnp.where(rest == m2, ei, ne), axis=0, keepdims=True)
    t2 = jnp.exp(m2 - m1)
    w1 = 1.0 / (1.0 + t2)
    w2 = t2 / (1.0 + t2)
    oh1 = jnp.where(ei == i1, 1.0, 0.0)
    oh2 = jnp.where(ei == i2, 1.0, 0.0)
    before = (lax.broadcasted_iota(jnp.int32, (tm, tm), 0)
              < lax.broadcasted_iota(jnp.int32, (tm, tm), 1))
    before = jnp.where(before, 1.0, 0.0).astype(BF16)
    cs1 = jnp.dot(oh1.astype(BF16), before, preferred_element_type=F32)
    cs2 = jnp.dot(oh2.astype(BF16), before, preferred_element_type=F32)
    n1 = jnp.sum(oh1, axis=1, keepdims=True)
    n2 = jnp.sum(oh2, axis=1, keepdims=True)
    rank1 = jnp.sum(oh1 * cs1, axis=0, keepdims=True)
    rank2 = jnp.sum(oh2 * (cs2 + n1), axis=0, keepdims=True)
    zero = jnp.zeros_like(w1)
    route_ref[0, 0] = jnp.concatenate(
        [i1.astype(F32), i2.astype(F32), w1, w2, rank1, rank2, zero, zero], axis=0)
    cnt_ref[0, 0] = jnp.broadcast_to(n1 + n2, (ne, LANES))


def _route(x, mod, ln, router_t):
    b, t, d = x.shape
    tm = _row_tile(t, 256)
    nt = t // tm
    ne = router_t.shape[0]
    return pl.pallas_call(
        _route_kernel,
        grid=(b, nt),
        in_specs=[pl.BlockSpec((1, tm, d), lambda i, j: (i, j, 0)),
                  pl.BlockSpec((1, N_MOD, d), lambda i, j: (i, 0, 0)),
                  _const_spec(ln.shape), _const_spec(router_t.shape)],
        out_specs=[pl.BlockSpec((1, tm, d), lambda i, j: (i, j, 0)),
                   pl.BlockSpec((1, 1, 8, tm), lambda i, j: (i, j, 0, 0)),
                   pl.BlockSpec((1, 1, ne, LANES), lambda i, j: (i, j, 0, 0))],
        out_shape=[jax.ShapeDtypeStruct((b, t, d), F32),
                   jax.ShapeDtypeStruct((b, nt, 8, tm), F32),
                   jax.ShapeDtypeStruct((b, nt, ne, LANES), F32)],
        compiler_params=_params(("parallel", "parallel")),
        name="moe_route",
    )(x, mod, ln, router_t)


def _row_copy(src, dst, sem):
    return pltpu.make_async_copy(src, dst, sem)


def _dispatch_kernel(dest_ref, h_ref, rows_in_ref, rows_ref, sem):
    del rows_in_ref
    tm = h_ref.shape[0]

    def start(i, _):
        for s in range(2):
            _row_copy(h_ref.at[pl.ds(i, 1)], rows_ref.at[pl.ds(dest_ref[0, 0, 2 * i + s], 1)],
                      sem.at[s]).start()
        return 0

    def wait(i, _):
        for s in range(2):
            _row_copy(h_ref.at[pl.ds(i, 1)], rows_ref.at[pl.ds(0, 1)], sem.at[s]).wait()
        return 0

    lax.fori_loop(0, tm, start, 0)
    lax.fori_loop(0, tm, wait, 0)


def _dispatch(h, dest, n_rows):
    n, d = h.shape
    tm = dest.shape[2] // 2
    rows = jnp.zeros((n_rows, d), h.dtype)
    return pl.pallas_call(
        _dispatch_kernel,
        grid=(n // tm,),
        in_specs=[pl.BlockSpec((1, 1, 2 * tm), lambda i: (i, 0, 0), memory_space=pltpu.SMEM),
                  pl.BlockSpec((tm, d), lambda i: (i, 0)),
                  pl.BlockSpec(memory_space=pl.ANY)],
        out_specs=pl.BlockSpec(memory_space=pl.ANY),
        out_shape=jax.ShapeDtypeStruct((n_rows, d), h.dtype),
        scratch_shapes=[pltpu.SemaphoreType.DMA((2,))],
        input_output_aliases={2: 0},
        compiler_params=pltpu.CompilerParams(dimension_semantics=("arbitrary",),
                                             vmem_limit_bytes=VMEM_LIMIT, has_side_effects=True),
        name="moe_dispatch",
    )(dest, h, rows)


def _experts_kernel(be_ref, nb_ref, x_ref, wg_ref, wu_ref, wd_ref, y_ref):
    del be_ref

    @pl.when(pl.program_id(0) < nb_ref[0])
    def _():
        x = x_ref[...].astype(BF16)
        f = wg_ref.shape[2]
        step = f // 7
        acc = None
        for c0 in range(0, f, step):
            gate = jnp.dot(x, wg_ref[0, :, c0:c0 + step], preferred_element_type=F32)
            up = jnp.dot(x, wu_ref[0, :, c0:c0 + step], preferred_element_type=F32)
            part = _dot(_silu(gate) * up, wd_ref[0, c0:c0 + step, :])
            acc = part if acc is None else acc + part
        y_ref[...] = acc

    @pl.when(pl.program_id(0) >= nb_ref[0])
    def _():
        y_ref[...] = jnp.zeros_like(y_ref)


def _experts(rows, block_e, n_used, wg, wu, wd):
    n_rows, d = rows.shape
    rb = MOE_ROW_BLOCK
    f = wg.shape[2]
    wspec = lambda shape: pl.BlockSpec((1,) + shape, lambda i, be, nb: (be[i], 0, 0),
                                       pipeline_mode=pl.Buffered(1))
    return pl.pallas_call(
        _experts_kernel,
        grid_spec=pltpu.PrefetchScalarGridSpec(
            num_scalar_prefetch=2,
            grid=(n_rows // rb,),
            in_specs=[pl.BlockSpec((rb, d), lambda i, be, nb: (i, 0)),
                      wspec((d, f)), wspec((d, f)), wspec((f, d))],
            out_specs=pl.BlockSpec((rb, d), lambda i, be, nb: (i, 0))),
        out_shape=jax.ShapeDtypeStruct((n_rows, d), F32),
        compiler_params=_params(("arbitrary",)),
        name="moe_experts",
    )(block_e, n_used, rows, wg, wu, wd)


def _combine_kernel(dest_ref, y_hbm, x_ref, wts_ref, mod_ref, lnf_ref, out_ref, buf, sem,
                    *, final_norm):
    tm = x_ref.shape[1]

    def start(i, _):
        for s in range(2):
            _row_copy(y_hbm.at[pl.ds(dest_ref[0, 0, 2 * i + s], 1)], buf.at[s, pl.ds(i, 1)],
                      sem.at[s]).start()
        return 0

    def wait(i, _):
        for s in range(2):
            _row_copy(y_hbm.at[pl.ds(0, 1)], buf.at[s, pl.ds(i, 1)], sem.at[s]).wait()
        return 0

    lax.fori_loop(0, tm, start, 0)
    lax.fori_loop(0, tm, wait, 0)
    f = wts_ref[0, :, 0:1] * buf[0] + wts_ref[0, :, 1:2] * buf[1]
    out = x_ref[0] + mod_ref[0, 5:6, :] * f
    if final_norm:
        out = _norm_mod(out, lnf_ref[...], 0.0, 0.0)
    out_ref[0] = out


def _combine(y_rows, dest, wts, x, mod, ln_final, final_norm):
    b, t, d = x.shape
    tm = dest.shape[2] // 2
    nt = t // tm
    return pl.pallas_call(
        functools.partial(_combine_kernel, final_norm=final_norm),
        grid=(b, nt),
        in_specs=[pl.BlockSpec((1, 1, 2 * tm), lambda i, j: (i * nt + j, 0, 0),
                               memory_space=pltpu.SMEM),
                  pl.BlockSpec(memory_space=pl.ANY),
                  pl.BlockSpec((1, tm, d), lambda i, j: (i, j, 0)),
                  pl.BlockSpec((1, tm, 2), lambda i, j: (i, j, 0)),
                  pl.BlockSpec((1, N_MOD, d), lambda i, j: (i, 0, 0)),
                  _const_spec(ln_final.shape)],
        out_specs=pl.BlockSpec((1, tm, d), lambda i, j: (i, j, 0)),
        out_shape=jax.ShapeDtypeStruct((b, t, d), F32),
        scratch_shapes=[pltpu.VMEM((2, tm, d), F32), pltpu.SemaphoreType.DMA((2,))],
        compiler_params=_params(("arbitrary", "arbitrary")),
        name="moe_combine",
    )(dest, y_rows, x, wts, mod, ln_final)


def _ffn_moe(x, mod, ln, ln_final, router_t, wg, wu, wd, final_norm):
    b, t, d = x.shape
    n = b * t
    ne = router_t.shape[0]
    rb = MOE_ROW_BLOCK
    h, route, cnt = _route(x, mod, ln, router_t)
    nt, tm = route.shape[1], route.shape[3]
    tile_cnt = cnt[..., 0].reshape(b * nt, ne).astype(jnp.int32)
    totals = jnp.sum(tile_cnt, axis=0)
    padded = ((totals + rb - 1) // rb) * rb
    padded_end = jnp.cumsum(padded)
    base = (padded_end - padded)[None, :] + jnp.cumsum(tile_cnt, axis=0) - tile_cnt
    route = route.reshape(b * nt, 8, tm)
    experts = route[:, 0:2, :].astype(jnp.int32)
    rank = route[:, 4:6, :].astype(jnp.int32)
    onehot = experts[..., None] == jnp.arange(ne, dtype=jnp.int32)
    dest = jnp.sum(jnp.where(onehot, base[:, None, None, :], 0), axis=-1) + rank
    dest = jnp.swapaxes(dest, 1, 2).reshape(b * nt, 1, 2 * tm)
    wts = jnp.swapaxes(route[:, 2:4, :], 1, 2).reshape(b, t, 2)
    n_blocks = -(-(n * 2) // rb) + ne
    block_e = jnp.minimum(
        jnp.searchsorted(padded_end, jnp.arange(n_blocks, dtype=jnp.int32) * rb, side="right"),
        ne - 1).astype(jnp.int32)
    n_used = (padded_end[-1:] // rb).astype(jnp.int32)
    rows = _dispatch(h.reshape(n, d), dest, n_blocks * rb)
    y_rows = _experts(rows, block_e, n_used, wg, wu, wd)
    return _combine(y_rows, dest, wts, x, mod, ln_final, final_norm)


def _block_diag(m, n):
    return jnp.kron(jnp.eye(n, dtype=m.dtype), m)


def _layer_params(i, w_in, conv_w, decay_w0, decay_w2, iclr_a0, iclr_a2, gate_g2, k_k, k_a, r_k,
                  lnx_w, lnx_b, w_fnet_out, w_rwkv_out, w_o):
    w = RWKV_WIDTH
    lora = decay_w2.shape[2]
    tail0 = FNET_WIDTH + 3 * w

    def reorder_tail(m):
        pad = jnp.zeros(m.shape[:-1] + (LANES - lora,), m.dtype)
        return jnp.concatenate([m[..., :3 * lora], pad, m[..., 3 * lora:]], axis=-1)

    row = lambda v: v.reshape(1, -1)
    hi_lo = lambda m: jnp.stack(_split_bf16(m, 2))
    wi = w_in[i]
    rwkv_cols = conv_w.shape[2]
    zeros = jnp.zeros((lora, w), F32)
    return {
        "w_a": wi[:, :FNET_WIDTH].astype(BF16),
        "w_main": wi[:, FNET_WIDTH:tail0].astype(BF16),
        "w_tail": reorder_tail(wi[:, tail0:FNET_WIDTH + rwkv_cols]).astype(BF16),
        "w_gates": wi[:, FNET_WIDTH + rwkv_cols:].astype(BF16),
        "conv_main": conv_w[i][:, :3 * w],
        "conv_tail": reorder_tail(conv_w[i][:, 3 * w:]),
        "decay_w0": decay_w0[i].reshape(1, 2 * w),
        "decay_w2": hi_lo(jnp.concatenate(
            [jnp.concatenate([decay_w2[i, 0], zeros], axis=1),
             jnp.concatenate([zeros, decay_w2[i, 1]], axis=1)], axis=0)),
        "iclr_a0": row(iclr_a0[i]),
        "iclr_a2": hi_lo(jnp.concatenate([iclr_a2[i], jnp.zeros((LANES - lora, w), F32)], axis=0)),
        "gate_g2": hi_lo(gate_g2[i]),
        "k_k": row(k_k[i]), "k_a": row(k_a[i]), "r_k": row(r_k[i]),
        "lnx_w": row(lnx_w[i]), "lnx_b": row(lnx_b[i]),
        "head_sum": _block_diag(jnp.ones((HEAD_DIM, HEAD_DIM), BF16), LANES // HEAD_DIM),
        "head_mean": _block_diag(jnp.full((HEAD_DIM, HEAD_DIM), 1.0 / HEAD_DIM, BF16),
                                 LANES // HEAD_DIM),
        "w_fnet_out": w_fnet_out[i].astype(BF16),
        "w_rwkv_out": w_rwkv_out[i].astype(BF16),
        "w_o": w_o[i].astype(BF16),
    }


def _channel_dft():
    k = jnp.arange(HEAD_DIM, dtype=jnp.int32)
    ang = ((k[:, None] * k[None, :]) % HEAD_DIM).astype(F32) * (2.0 * math.pi / HEAD_DIM)
    groups = FNET_WIDTH // HEAD_DIM
    return jnp.concatenate([_block_diag(jnp.cos(ang), groups), _block_diag(jnp.sin(ang), groups)],
                           axis=1).astype(BF16)


def _token_mixer(x, mod, ln, p, cs, ct, st):
    zc, zs, um, ut, gates = _inproj(x, mod, ln, p["w_a"], p["w_main"], p["w_tail"], p["w_gates"], cs)
    yf = _fnet(zc, zs, ct, st)
    r, k, v, a, b, e_f, e_b, g, bonus = _prep(um, ut, p)
    o_f = _wkv(r, k, v, a, b, e_f, False)
    o_b = _wkv(r, k, v, a, b, e_b, True)
    return _mixout(o_f, o_b, bonus, g, yf, gates, x, mod, p)


def kernel(x_prompt, x_sample, c_prompt, c_sample, w_ada, b_ada, ln_mix, w_in, conv_w, decay_w0, decay_w2, iclr_a0, iclr_a2, gate_g2, k_k, k_a, r_k, lnx_w, lnx_b, w_fnet_out, w_rwkv_out, w_o, ln_ffn, ff_w_gate, ff_w_up, ff_w_down, router_w, moe_w_gate, moe_w_up, moe_w_down, ln_final):
    depth, d = ln_mix.shape
    streams = [x_prompt, x_sample]
    nb = [x.shape[0] for x in streams]
    mod_all = _ada(jnp.concatenate([c_prompt, c_sample], axis=0), w_ada, b_ada)
    cs = _channel_dft()
    dfts = {}
    for x in streams:
        if x.shape[1] not in dfts:
            dfts[x.shape[1]] = _dft_matrices(x.shape[1])
    lnf = ln_final.reshape(1, d)
    for i in range(depth):
        p = _layer_params(i, w_in, conv_w, decay_w0, decay_w2, iclr_a0, iclr_a2, gate_g2, k_k, k_a,
                          r_k, lnx_w, lnx_b, w_fnet_out, w_rwkv_out, w_o)
        j = i // 2
        final = i == depth - 1
        if i % 2 == 0:
            ffw = (ff_w_gate[j].astype(BF16), ff_w_up[j].astype(BF16), ff_w_down[j].astype(BF16))
        else:
            ffw = (moe_w_gate[j].astype(BF16), moe_w_up[j].astype(BF16), moe_w_down[j].astype(BF16))
            router_t = router_w[j].T
        off = 0
        for s, x in enumerate(streams):
            mod = mod_all[i, off:off + nb[s]].reshape(nb[s], N_MOD, d)
            off += nb[s]
            x = _token_mixer(x, mod, ln_mix[i].reshape(1, d), p, cs, *dfts[x.shape[1]])
            ln2 = ln_ffn[i].reshape(1, d)
            if i % 2 == 0:
                x = _ffn_dense(x, mod, ln2, lnf, *ffw, final)
            else:
                x = _ffn_moe(x, mod, ln2, lnf, router_t, *ffw, final)
            streams[s] = x
    return tuple(streams)
```

```python
import functools
import math

import jax
import jax.numpy as jnp
from jax import lax
from jax.experimental import pallas as pl
from jax.experimental.pallas import tpu as pltpu

F32 = jnp.float32
BF16 = jnp.bfloat16
HIGHEST = lax.Precision.HIGHEST

HEAD_DIM = 64
LANES = 128
FNET_WIDTH = 256
RWKV_WIDTH = 768
LORA_TAIL = 384
N_MOD = 6
N_EXPERTS = 8
RMS_EPS = 1e-6
GN_EPS = 64e-5
WKV_CHUNK = 128
MOE_ROW_BLOCK = 512
VMEM_LIMIT = 56 * 1024 * 1024


def _params(sem, vmem=VMEM_LIMIT):
    return pltpu.CompilerParams(dimension_semantics=sem, vmem_limit_bytes=vmem)


def _const_spec(shape):
    nd = len(shape)
    return pl.BlockSpec(shape, lambda *_: (0,) * nd, pipeline_mode=pl.Buffered(1))


def _dot(a, b):
    return jnp.dot(a.astype(BF16), b.astype(BF16), preferred_element_type=F32)


def _dot_nt(a, b):
    return lax.dot_general(a.astype(BF16), b.astype(BF16), (((1,), (1,)), ((), ())),
                           preferred_element_type=F32)


def _dot_tn(a, b):
    return lax.dot_general(a.astype(BF16), b.astype(BF16), (((0,), (0,)), ((), ())),
                           preferred_element_type=F32)


def _split_bf16(x, parts):
    out, rem = [], x
    for _ in range(parts):
        p = rem.astype(BF16)
        out.append(p)
        rem = rem - p.astype(F32)
    return out


def _dot_exact_rhs(x, m, parts):
    acc = None
    for p in _split_bf16(x, parts):
        t = jnp.dot(p, m, preferred_element_type=F32)
        acc = t if acc is None else acc + t
    return acc


def _dot_exact_lhs(m, x, parts):
    acc = None
    for p in _split_bf16(x, parts):
        t = jnp.dot(m, p, preferred_element_type=F32)
        acc = t if acc is None else acc + t
    return acc


def _dot_f32(a, b):
    return jnp.dot(a, b, precision=HIGHEST, preferred_element_type=F32)


def _dot_3pass(a, w_ref):
    ah, al = _split_bf16(a, 2)
    d = lambda x, y: jnp.dot(x, y, preferred_element_type=F32)
    return d(ah, w_ref[0]) + (d(al, w_ref[0]) + d(ah, w_ref[1]))


def _norm_mod(x, gain, shift, scale):
    ms = jnp.mean(x * x, axis=-1, keepdims=True)
    return x * lax.rsqrt(ms + RMS_EPS) * gain * (1.0 + scale) + shift


def _silu(x):
    return x * jax.nn.sigmoid(x)


def _head_reduce(x, m):
    cols = [_dot_exact_rhs(x[:, j:j + LANES], m, 2) for j in range(0, x.shape[1], LANES)]
    return jnp.concatenate(cols, axis=1)


def _row_tile(t, want):
    tm = min(t, want)
    assert t % tm == 0
    return tm


def _ada_kernel(c_ref, w_ref, b_ref, o_ref):
    o_ref[0] = _dot_f32(_silu(c_ref[...]), w_ref[0]) + b_ref[0]


def _ada(c_all, w_ada, b_ada):
    depth, d, n = w_ada.shape
    bc = c_all.shape[0]
    tn = n // 4
    return pl.pallas_call(
        _ada_kernel,
        grid=(depth, n // tn),
        in_specs=[pl.BlockSpec((bc, d), lambda l, j: (0, 0)),
                  pl.BlockSpec((1, d, tn), lambda l, j: (l, 0, j)),
                  pl.BlockSpec((1, 1, tn), lambda l, j: (l, 0, j))],
        out_specs=pl.BlockSpec((1, bc, tn), lambda l, j: (l, 0, j)),
        out_shape=jax.ShapeDtypeStruct((depth, bc, n), F32),
        compiler_params=_params(("parallel", "parallel")),
        name="ada",
    )(c_all, w_ada, b_ada.reshape(depth, 1, n))


def _inproj_kernel(x_ref, mod_ref, ln_ref, wa_ref, wm_ref, wt_ref, wg_ref, cs_ref,
                   zc_ref, zs_ref, um_ref, ut_ref, g_ref):
    h = _norm_mod(x_ref[0], ln_ref[...], mod_ref[0, 0:1, :], mod_ref[0, 1:2, :]).astype(BF16)
    ua = jnp.dot(h, wa_ref[...], preferred_element_type=F32)
    zz = jnp.dot(ua.astype(BF16), cs_ref[...], preferred_element_type=F32)
    zc_ref[0] = zz[:, :FNET_WIDTH].astype(BF16)
    zs_ref[0] = zz[:, FNET_WIDTH:].astype(BF16)
    step = 384
    for c0 in range(0, wm_ref.shape[1], step):
        um_ref[0, :, c0:c0 + step] = jnp.dot(
            h, wm_ref[:, c0:c0 + step], preferred_element_type=F32).astype(BF16)
    ut_ref[0] = jnp.dot(h, wt_ref[...], preferred_element_type=F32)
    step = 512
    for c0 in range(0, wg_ref.shape[1], step):
        g_ref[0, :, c0:c0 + step] = jax.nn.sigmoid(
            jnp.dot(h, wg_ref[:, c0:c0 + step], preferred_element_type=F32)).astype(BF16)


def _inproj(x, mod, ln, wa, wm, wt, wg, cs):
    b, t, d = x.shape
    tm = _row_tile(t, 1024)
    row = lambda w: pl.BlockSpec((1, tm, w), lambda i, j: (i, j, 0))
    outs = [(FNET_WIDTH, BF16), (FNET_WIDTH, BF16), (wm.shape[1], BF16), (wt.shape[1], F32),
            (wg.shape[1], BF16)]
    return pl.pallas_call(
        _inproj_kernel,
        grid=(b, t // tm),
        in_specs=[row(d), pl.BlockSpec((1, N_MOD, d), lambda i, j: (i, 0, 0)),
                  _const_spec(ln.shape), _const_spec(wa.shape), _const_spec(wm.shape),
                  _const_spec(wt.shape), _const_spec(wg.shape), _const_spec(cs.shape)],
        out_specs=[row(w) for w, _ in outs],
        out_shape=[jax.ShapeDtypeStruct((b, t, w), dt) for w, dt in outs],
        compiler_params=_params(("parallel", "parallel")),
        name="inproj",
    )(x, mod, ln, wa, wm, wt, wg, cs)


def _fnet_kernel(ct_ref, st_ref, zc_ref, zs_ref, o_ref, *, scale):
    y = (jnp.dot(ct_ref[...], zc_ref[0], preferred_element_type=F32)
         - jnp.dot(st_ref[...], zs_ref[0], preferred_element_type=F32))
    o_ref[0] = (y * scale).astype(BF16)


def _dft_matrices(t):
    rr = 64
    k = lax.iota(jnp.int32, t)

    def table(rows):
        ang = ((rows[:, None] * k[None, :]) % t).astype(F32) * (2.0 * math.pi / t)
        return jnp.cos(ang), jnp.sin(ang)

    ca, sa = table(lax.iota(jnp.int32, t // rr) * rr)
    cb, sb = table(lax.iota(jnp.int32, rr))
    ca, sa, cb, sb = ca[:, None, :], sa[:, None, :], cb[None], sb[None]
    return ((ca * cb - sa * sb).reshape(t, t).astype(BF16),
            (sa * cb + ca * sb).reshape(t, t).astype(BF16))


def _fnet(zc, zs, ct, st):
    b, t, w = zc.shape
    tm = _row_tile(t, max(128, (2 * 1024 * 1024) // t))
    return pl.pallas_call(
        functools.partial(_fnet_kernel, scale=1.0 / math.sqrt(t * HEAD_DIM)),
        grid=(b, t // tm),
        in_specs=[pl.BlockSpec((tm, t), lambda i, j: (j, 0)),
                  pl.BlockSpec((tm, t), lambda i, j: (j, 0)),
                  pl.BlockSpec((1, t, w), lambda i, j: (i, 0, 0)),
                  pl.BlockSpec((1, t, w), lambda i, j: (i, 0, 0))],
        out_specs=pl.BlockSpec((1, tm, w), lambda i, j: (i, j, 0)),
        out_shape=jax.ShapeDtypeStruct((b, t, w), BF16),
        compiler_params=_params(("parallel", "arbitrary")),
        name="fnet",
    )(ct, st, zc, zs)


def _conv3(main, prev_row, next_row, w):
    tm = main.shape[0]
    ri = lax.broadcasted_iota(jnp.int32, main.shape, 0)
    up = jnp.where(ri == 0, prev_row, pltpu.roll(main, 1, 0))
    dn = jnp.where(ri == tm - 1, next_row, pltpu.roll(main, tm - 1, 0))
    return w[0:1] * up + w[1:2] * main + w[2:3] * dn


def _prep_kernel(um_ref, ump_ref, umn_ref, ut_ref, utp_ref, utn_ref, cwm_ref, cwt_ref,
                 w0_ref, w2_ref, a0_ref, a2_ref, g2_ref, kk_ref, ka_ref, rk_ref, hs_ref,
                 r_ref, k_ref, v_ref, a_ref, b_ref, ef_ref, eb_ref, g_ref, bonus_ref):
    j = pl.program_id(1)
    first = j == 0
    last = j == pl.num_programs(1) - 1
    hp = ump_ref.shape[1]

    def conv_main(c0, c1):
        prev = jnp.where(first, 0.0, ump_ref[0, hp - 1:hp, c0:c1].astype(F32))
        nxt = jnp.where(last, 0.0, umn_ref[0, 0:1, c0:c1].astype(F32))
        return _conv3(um_ref[0, :, c0:c1].astype(F32), prev, nxt, cwm_ref[:, c0:c1])

    w = RWKV_WIDTH
    r = conv_main(0, w)
    k = conv_main(w, 2 * w)
    v = conv_main(2 * w, 3 * w)
    tp = utp_ref.shape[1]
    tail = _conv3(ut_ref[0], jnp.where(first, 0.0, utp_ref[0, tp - 1:tp, :]),
                  jnp.where(last, 0.0, utn_ref[0, 0:1, :]), cwt_ref[...])
    xw, xa, xg = tail[:, 0:LANES], tail[:, LANES:2 * LANES], tail[:, 2 * LANES:3 * LANES]

    e = jax.nn.sigmoid(w0_ref[...] + _dot_3pass(jnp.tanh(xw), w2_ref)) * math.exp(-0.5)
    ef_ref[0] = e[:, :w]
    eb_ref[0] = e[:, w:]
    a = jax.nn.sigmoid(a0_ref[...] + _dot_3pass(xa, a2_ref))
    g_ref[0] = _dot_3pass(jax.nn.sigmoid(xg), g2_ref).astype(BF16)

    kk = k * kk_ref[...]
    kk = kk * lax.rsqrt(_head_reduce(kk * kk, hs_ref[...]) + 1e-12)
    k = k * (1.0 + (a - 1.0) * ka_ref[...])
    r_ref[0] = r.astype(BF16)
    k_ref[0] = k.astype(BF16)
    v_ref[0] = v.astype(BF16)
    a_ref[0] = (-kk).astype(BF16)
    b_ref[0] = (kk * a).astype(BF16)
    bonus_ref[0] = (_head_reduce(r * k * rk_ref[...], hs_ref[...]) * v).astype(BF16)


def _prep(um, ut, p):
    b, t, wm = um.shape
    wt = ut.shape[2]
    tm = _row_tile(t, 256)
    hm, ht = 16, 8
    nm, nt = tm // hm, tm // ht
    main = lambda w: pl.BlockSpec((1, tm, w), lambda i, j: (i, j, 0))
    prev = lambda h, n, w: pl.BlockSpec((1, h, w), lambda i, j: (i, jnp.maximum(j * n - 1, 0), 0))
    nxt = lambda h, n, w: pl.BlockSpec(
        (1, h, w), lambda i, j: (i, jnp.minimum((j + 1) * n, t // h - 1), 0))
    consts = [p["conv_main"], p["conv_tail"], p["decay_w0"], p["decay_w2"], p["iclr_a0"],
              p["iclr_a2"], p["gate_g2"], p["k_k"], p["k_a"], p["r_k"], p["head_sum"]]
    outs = [BF16] * 5 + [F32, F32, BF16, BF16]
    return pl.pallas_call(
        _prep_kernel,
        grid=(b, t // tm),
        in_specs=[main(wm), prev(hm, nm, wm), nxt(hm, nm, wm),
                  main(wt), prev(ht, nt, wt), nxt(ht, nt, wt)]
                 + [_const_spec(c.shape) for c in consts],
        out_specs=[main(RWKV_WIDTH) for _ in outs],
        out_shape=[jax.ShapeDtypeStruct((b, t, RWKV_WIDTH), dt) for dt in outs],
        compiler_params=_params(("parallel", "parallel")),
        name="rwkv_prep",
    )(um, um, um, ut, ut, ut, *consts)


def _wkv_kernel(r_ref, k_ref, v_ref, a_ref, b_ref, e_ref, o_ref, s_ref, *, reverse):
    c = r_ref.shape[1]
    pairs = range(s_ref.shape[0])
    heads = range(LANES // HEAD_DIM)
    chains = [(p, h) for p in pairs for h in heads]

    @pl.when(pl.program_id(1) == 0)
    def _():
        s_ref[...] = jnp.zeros_like(s_ref)

    row = lax.broadcasted_iota(jnp.int32, (c, c), 0)
    col = lax.broadcasted_iota(jnp.int32, (c, c), 1)
    if reverse:
        incl, strict = col >= row, col > row
    else:
        incl, strict = col <= row, col < row
    tri = jnp.where(incl, 1.0, 0.0).astype(BF16)
    end = 0 if reverse else c - 1
    lane = lax.broadcasted_iota(jnp.int32, (1, LANES), 1)
    in_head = [(lane >= h * HEAD_DIM) & (lane < (h + 1) * HEAD_DIM) for h in heads]
    same_head = (lax.broadcasted_iota(jnp.int32, (LANES, LANES), 0) // HEAD_DIM
                 == lax.broadcasted_iota(jnp.int32, (LANES, LANES), 1) // HEAD_DIM)

    per_row = r_ref.shape[2] // LANES

    def tile(ref, p):
        return ref[p // per_row, :, (p % per_row) * LANES:(p % per_row + 1) * LANES]

    cum = [_dot_exact_lhs(tri, tile(e_ref, p), 3) for p in pairs]
    total = [x[end:end + 1, :] for x in cum]
    ref = [x[c // 2:c // 2 + 1, :] for x in cum]
    dl = [cum[p] - ref[p] for p in pairs]
    g_inv = [jnp.exp(dl[p]) for p in pairs]
    at = [tile(a_ref, p).astype(F32) * jnp.exp(tile(e_ref, p) - dl[p]) for p in pairs]
    rt = [tile(r_ref, p).astype(F32) * jnp.exp(-dl[p]) for p in pairs]
    bt = [tile(b_ref, p).astype(F32) * g_inv[p] for p in pairs]
    kt = [tile(k_ref, p).astype(F32) * g_inv[p] for p in pairs]
    g_end = [jnp.exp(ref[p] - total[p]) for p in pairs]
    bh = [(bt[p] * g_end[p]).astype(BF16) for p in pairs]
    kh = [(kt[p] * g_end[p]).astype(BF16) for p in pairs]
    bk = [jnp.concatenate([bt[p], kt[p]], axis=0).astype(BF16) for p in pairs]
    ar = [jnp.concatenate([at[p], rt[p]], axis=0) for p in pairs]
    vb = [tile(v_ref, p) for p in pairs]

    prod = [_dot_nt(jnp.where(in_head[h], ar[p], 0.0), bk[p]) for p, h in chains]
    l_ab = [jnp.where(strict, x[:c, :c], 0.0) for x in prod]
    l_ak = [jnp.where(strict, x[:c, c:], 0.0) for x in prod]
    m_rb = [jnp.where(incl, x[c:, :c], 0.0).astype(BF16) for x in prod]
    m_rk = [jnp.where(incl, x[c:, c:], 0.0).astype(BF16) for x in prod]
    del prod
    eye = jnp.where(row == col, 1.0, 0.0)
    t = [eye + jnp.where((row >> 1) == (col >> 1), x, 0.0) for x in l_ab]
    for sh in range(1, (c - 1).bit_length()):
        joins = ((row >> (sh + 1)) == (col >> (sh + 1))) & ((row >> sh) != (col >> sh))
        et = [_dot(jnp.where(joins, l_ab[i], 0.0), t[i]) for i in range(len(chains))]
        t = [t[i] + _dot(t[i], et[i]) for i in range(len(chains))]
    lmv = [jnp.dot(jnp.concatenate([l_ak[i].astype(BF16), m_rk[i]], axis=0), vb[p],
                   preferred_element_type=F32) for i, (p, h) in enumerate(chains)]
    lakv = [v[:c] for v in lmv]
    mrkv = [v[c:] for v in lmv]
    x = [_dot(t[i], jnp.concatenate([at[p], lakv[i]], axis=1)) for i, (p, h) in enumerate(chains)]
    y = [jnp.dot(m_rb[i], x[i].astype(BF16), preferred_element_type=F32) for i in range(len(chains))]

    def pick(vals, p):
        out = vals[p * len(heads)]
        for h in heads[1:]:
            out = jnp.where(in_head[h], vals[p * len(heads) + h], out)
        return out

    a_solved = [pick([v[:, :LANES] for v in x], p) for p in pairs]
    u = [pick([v[:, LANES:] for v in x], p) for p in pairs]
    r_hat = [rt[p] + pick([v[:, :LANES] for v in y], p) for p in pairs]
    o_intra = [pick([y[i][:, LANES:] + mrkv[i] for i in range(len(chains))], p) for p in pairs]
    s = [s_ref[p] for p in pairs]
    s_hat = [(s[p] * jnp.exp(-ref[p])).astype(BF16) for p in pairs]
    out = [_dot_nt(r_hat[p], s_hat[p]) + o_intra[p] for p in pairs]
    g_t = [jnp.where(same_head, _dot_tn(a_solved[p], bh[p]), 0.0) for p in pairs]
    h_t = [jnp.where(same_head, _dot_tn(u[p], bh[p]) + _dot_tn(vb[p], kh[p]), 0.0) for p in pairs]
    s_new = [s[p] * jnp.exp(-total[p]) + _dot(s_hat[p], g_t[p]) + h_t[p] for p in pairs]
    for p in pairs:
        o_ref[p // per_row, :, (p % per_row) * LANES:(p % per_row + 1) * LANES] = out[p]
        s_ref[p] = s_new[p]


def _wkv(r, k, v, a, b, e, reverse):
    bsz, t, w = r.shape
    c = min(WKV_CHUNK, t)
    nc = t // c
    rows = 2 if bsz % 2 == 0 else 1
    if reverse:
        idx = lambda i, j: (i, nc - 1 - j, 0)
    else:
        idx = lambda i, j: (i, j, 0)
    spec = pl.BlockSpec((rows, c, w), idx)
    return pl.pallas_call(
        functools.partial(_wkv_kernel, reverse=reverse),
        grid=(bsz // rows, nc),
        in_specs=[spec] * 6,
        out_specs=spec,
        out_shape=jax.ShapeDtypeStruct((bsz, t, w), F32),
        scratch_shapes=[pltpu.VMEM((rows * (w // LANES), LANES, LANES), F32)],
        compiler_params=_params(("parallel", "arbitrary")),
        name="wkv_bwd" if reverse else "wkv_fwd",
    )(r, k, v, a, b, e)


def _mixout_kernel(of_ref, ob_ref, bonus_ref, g_ref, yf_ref, gates_ref, x_ref, mod_ref,
                   lnw_ref, lnb_ref, hm_ref, wfo_ref, wro_ref, wo_ref, out_ref):
    d = x_ref.shape[2]
    o = of_ref[0] + ob_ref[0]
    dev = o - _head_reduce(o, hm_ref[...])
    var = _head_reduce(dev * dev, hm_ref[...])
    o = dev * lax.rsqrt(var + GN_EPS) * lnw_ref[...] + lnb_ref[...]
    y = (o + bonus_ref[0].astype(F32)) * g_ref[0].astype(F32)
    y_b = _dot(y, wro_ref[...])
    y_a = jnp.dot(yf_ref[0], wfo_ref[...], preferred_element_type=F32)
    gates = gates_ref[0].astype(F32)
    merged = gates[:, :d] * y_a + gates[:, d:] * y_b
    out_ref[0] = x_ref[0] + mod_ref[0, 2:3, :] * _dot(merged, wo_ref[...])


def _mixout(o_f, o_b, bonus, g, yf, gates, x, mod, p):
    b, t, d = x.shape
    tm = _row_tile(t, 1024)
    row = lambda a: pl.BlockSpec((1, tm, a.shape[2]), lambda i, j: (i, j, 0))
    consts = [p["lnx_w"], p["lnx_b"], p["head_mean"], p["w_fnet_out"], p["w_rwkv_out"], p["w_o"]]
    acts = [o_f, o_b, bonus, g, yf, gates, x]
    return pl.pallas_call(
        _mixout_kernel,
        grid=(b, t // tm),
        in_specs=[row(a) for a in acts] + [pl.BlockSpec((1, N_MOD, d), lambda i, j: (i, 0, 0))]
                 + [_const_spec(c.shape) for c in consts],
        out_specs=pl.BlockSpec((1, tm, d), lambda i, j: (i, j, 0)),
        out_shape=jax.ShapeDtypeStruct((b, t, d), F32),
        compiler_params=_params(("parallel", "parallel")),
        name="mixout",
    )(*acts, mod, *consts)


def _ffn_kernel(x_ref, mod_ref, ln_ref, lnf_ref, wg_ref, wu_ref, wd_ref, out_ref, *, final_norm):
    x = x_ref[0]
    h = _norm_mod(x, ln_ref[...], mod_ref[0, 3:4, :], mod_ref[0, 4:5, :]).astype(BF16)
    f = wg_ref.shape[1]
    step = f // 4
    acc = None
    for c0 in range(0, f, step):
        gate = jnp.dot(h, wg_ref[:, c0:c0 + step], preferred_element_type=F32)
        up = jnp.dot(h, wu_ref[:, c0:c0 + step], preferred_element_type=F32)
        part = _dot(_silu(gate) * up, wd_ref[c0:c0 + step, :])
        acc = part if acc is None else acc + part
    out = x + mod_ref[0, 5:6, :] * acc
    if final_norm:
        out = _norm_mod(out, lnf_ref[...], 0.0, 0.0)
    out_ref[0] = out


def _ffn_dense(x, mod, ln, ln_final, wg, wu, wd, final_norm):
    b, t, d = x.shape
    tm = _row_tile(t, 1024)
    consts = [ln, ln_final, wg, wu, wd]
    return pl.pallas_call(
        functools.partial(_ffn_kernel, final_norm=final_norm),
        grid=(b, t // tm),
        in_specs=[pl.BlockSpec((1, tm, d), lambda i, j: (i, j, 0)),
                  pl.BlockSpec((1, N_MOD, d), lambda i, j: (i, 0, 0))]
                 + [_const_spec(c.shape) for c in consts],
        out_specs=pl.BlockSpec((1, tm, d), lambda i, j: (i, j, 0)),
        out_shape=jax.ShapeDtypeStruct((b, t, d), F32),
        compiler_params=_params(("parallel", "parallel")),
        name="ffn_dense",
    )(x, mod, *consts)


def _route_kernel(x_ref, mod_ref, ln_ref, rw_ref, h_ref, route_ref, cnt_ref):
    h = _norm_mod(x_ref[0], ln_ref[...], mod_ref[0, 3:4, :], mod_ref[0, 4:5, :])
    h_ref[0] = h
    tm = h.shape[0]
    logits = lax.dot_general(rw_ref[...], h, (((1,), (1,)), ((), ())), precision=HIGHEST,
                             preferred_element_type=F32)
    ne = logits.shape[0]
    ei = lax.broadcasted_iota(jnp.int32, logits.shape, 0)
    m1 = jnp.max(logits, axis=0, keepdims=True)
    i1 = jnp.min(jnp.where(logits == m1, ei, ne), axis=0, keepdims=True)
    rest = jnp.where(ei == i1, -jnp.inf, logits)
    m2 = jnp.max(rest, axis=0, keepdims=True)
    i2 = jnp.min(jnp.where(rest == m2, ei, ne), axis=0, keepdims=True)
    t2 = jnp.exp(m2 - m1)
    w1 = 1.0 / (1.0 + t2)
    w2 = t2 / (1.0 + t2)
    oh1 = jnp.where(ei == i1, 1.0, 0.0)
    oh2 = jnp.where(ei == i2, 1.0, 0.0)
    before = (lax.broadcasted_iota(jnp.int32, (tm, tm), 0)
              < lax.broadcasted_iota(jnp.int32, (tm, tm), 1))
    before = jnp.where(before, 1.0, 0.0).astype(BF16)
    cs1 = jnp.dot(oh1.astype(BF16), before, preferred_element_type=F32)
    cs2 = jnp.dot(oh2.astype(BF16), before, preferred_element_type=F32)
    n1 = jnp.sum(oh1, axis=1, keepdims=True)
    n2 = jnp.sum(oh2, axis=1, keepdims=True)
    rank1 = jnp.sum(oh1 * cs1, axis=0, keepdims=True)
    rank2 = jnp.sum(oh2 * (cs2 + n1), axis=0, keepdims=True)
    zero = jnp.zeros_like(w1)
    route_ref[0, 0] = jnp.concatenate(
        [i1.astype(F32), i2.astype(F32), w1, w2, rank1, rank2, zero, zero], axis=0)
    cnt_ref[0, 0] = jnp.broadcast_to(n1 + n2, (ne, LANES))


def _route(x, mod, ln, router_t):
    b, t, d = x.shape
    tm = _row_tile(t, 256)
    nt = t // tm
    ne = router_t.shape[0]
    return pl.pallas_call(
        _route_kernel,
        grid=(b, nt),
        in_specs=[pl.BlockSpec((1, tm, d), lambda i, j: (i, j, 0)),
                  pl.BlockSpec((1, N_MOD, d), lambda i, j: (i, 0, 0)),
                  _const_spec(ln.shape), _const_spec(router_t.shape)],
        out_specs=[pl.BlockSpec((1, tm, d), lambda i, j: (i, j, 0)),
                   pl.BlockSpec((1, 1, 8, tm), lambda i, j: (i, j, 0, 0)),
                   pl.BlockSpec((1, 1, ne, LANES), lambda i, j: (i, j, 0, 0))],
        out_shape=[jax.ShapeDtypeStruct((b, t, d), F32),
                   jax.ShapeDtypeStruct((b, nt, 8, tm), F32),
                   jax.ShapeDtypeStruct((b, nt, ne, LANES), F32)],
        compiler_params=_params(("parallel", "parallel")),
        name="moe_route",
    )(x, mod, ln, router_t)


def _row_copy(src, dst, sem):
    return pltpu.make_async_copy(src, dst, sem)


def _dispatch_kernel(dest_ref, h_ref, rows_in_ref, rows_ref, sem):
    del rows_in_ref
    tm = h_ref.shape[0]

    def start(i, _):
        for s in range(2):
            _row_copy(h_ref.at[pl.ds(i, 1)], rows_ref.at[pl.ds(dest_ref[0, 0, 2 * i + s], 1)],
                      sem.at[s]).start()
        return 0

    lax.fori_loop(0, tm, start, 0, unroll=8)
    for s in range(2):
        _row_copy(h_ref, rows_ref.at[pl.ds(0, tm)], sem.at[s]).wait()


def _dispatch(h, dest, n_rows):
    n, d = h.shape
    tm = dest.shape[2] // 2
    rows = jnp.zeros((n_rows, d), h.dtype)
    return pl.pallas_call(
        _dispatch_kernel,
        grid=(n // tm,),
        in_specs=[pl.BlockSpec((1, 1, 2 * tm), lambda i: (i, 0, 0), memory_space=pltpu.SMEM),
                  pl.BlockSpec((tm, d), lambda i: (i, 0)),
                  pl.BlockSpec(memory_space=pl.ANY)],
        out_specs=pl.BlockSpec(memory_space=pl.ANY),
        out_shape=jax.ShapeDtypeStruct((n_rows, d), h.dtype),
        scratch_shapes=[pltpu.SemaphoreType.DMA((2,))],
        input_output_aliases={2: 0},
        compiler_params=pltpu.CompilerParams(dimension_semantics=("arbitrary",),
                                             vmem_limit_bytes=VMEM_LIMIT, has_side_effects=True),
        name="moe_dispatch",
    )(dest, h, rows)


def _experts_kernel(be_ref, nb_ref, x_ref, wg_ref, wu_ref, wd_ref, y_ref):
    del be_ref

    @pl.when(pl.program_id(0) < nb_ref[0])
    def _():
        x = x_ref[...].astype(BF16)
        f = wg_ref.shape[2]
        step = f // 7
        acc = None
        for c0 in range(0, f, step):
            gate = jnp.dot(x, wg_ref[0, :, c0:c0 + step], preferred_element_type=F32)
            up = jnp.dot(x, wu_ref[0, :, c0:c0 + step], preferred_element_type=F32)
            part = _dot(_silu(gate) * up, wd_ref[0, c0:c0 + step, :])
            acc = part if acc is None else acc + part
        y_ref[...] = acc

    @pl.when(pl.program_id(0) >= nb_ref[0])
    def _():
        y_ref[...] = jnp.zeros_like(y_ref)


def _experts(rows, block_e, n_used, wg, wu, wd):
    n_rows, d = rows.shape
    rb = MOE_ROW_BLOCK
    f = wg.shape[2]
    wspec = lambda shape: pl.BlockSpec((1,) + shape, lambda i, be, nb: (be[i], 0, 0),
                                       pipeline_mode=pl.Buffered(1))
    return pl.pallas_call(
        _experts_kernel,
        grid_spec=pltpu.PrefetchScalarGridSpec(
            num_scalar_prefetch=2,
            grid=(n_rows // rb,),
            in_specs=[pl.BlockSpec((rb, d), lambda i, be, nb: (i, 0)),
                      wspec((d, f)), wspec((d, f)), wspec((f, d))],
            out_specs=pl.BlockSpec((rb, d), lambda i, be, nb: (i, 0))),
        out_shape=jax.ShapeDtypeStruct((n_rows, d), F32),
        compiler_params=_params(("arbitrary",)),
        name="moe_experts",
    )(block_e, n_used, rows, wg, wu, wd)


def _combine_kernel(dest_ref, y_hbm, x_ref, wts_ref, mod_ref, lnf_ref, out_ref, buf, sem,
                    *, final_norm):
    tm = x_ref.shape[1]

    def start(i, _):
        for s in range(2):
            _row_copy(y_hbm.at[pl.ds(dest_ref[0, 0, 2 * i + s], 1)], buf.at[s, pl.ds(i, 1)],
                      sem.at[s]).start()
        return 0

    lax.fori_loop(0, tm, start, 0, unroll=8)
    for s in range(2):
        _row_copy(y_hbm.at[pl.ds(0, tm)], buf.at[s], sem.at[s]).wait()
    f = wts_ref[0, :, 0:1] * buf[0] + wts_ref[0, :, 1:2] * buf[1]
    out = x_ref[0] + mod_ref[0, 5:6, :] * f
    if final_norm:
        out = _norm_mod(out, lnf_ref[...], 0.0, 0.0)
    out_ref[0] = out


def _combine(y_rows, dest, wts, x, mod, ln_final, final_norm):
    b, t, d = x.shape
    tm = dest.shape[2] // 2
    nt = t // tm
    return pl.pallas_call(
        functools.partial(_combine_kernel, final_norm=final_norm),
        grid=(b, nt),
        in_specs=[pl.BlockSpec((1, 1, 2 * tm), lambda i, j: (i * nt + j, 0, 0),
                               memory_space=pltpu.SMEM),
                  pl.BlockSpec(memory_space=pl.ANY),
                  pl.BlockSpec((1, tm, d), lambda i, j: (i, j, 0)),
                  pl.BlockSpec((1, tm, 2), lambda i, j: (i, j, 0)),
                  pl.BlockSpec((1, N_MOD, d), lambda i, j: (i, 0, 0)),
                  _const_spec(ln_final.shape)],
        out_specs=pl.BlockSpec((1, tm, d), lambda i, j: (i, j, 0)),
        out_shape=jax.ShapeDtypeStruct((b, t, d), F32),
        scratch_shapes=[pltpu.VMEM((2, tm, d), F32), pltpu.SemaphoreType.DMA((2,))],
        compiler_params=_params(("arbitrary", "arbitrary")),
        name="moe_combine",
    )(dest, y_rows, x, wts, mod, ln_final)


def _ffn_moe(x, mod, ln, ln_final, router_t, wg, wu, wd, final_norm):
    b, t, d = x.shape
    n = b * t
    ne = router_t.shape[0]
    rb = MOE_ROW_BLOCK
    h, route, cnt = _route(x, mod, ln, router_t)
    nt, tm = route.shape[1], route.shape[3]
    tile_cnt = cnt[..., 0].reshape(b * nt, ne).astype(jnp.int32)
    totals = jnp.sum(tile_cnt, axis=0)
    padded = ((totals + rb - 1) // rb) * rb
    padded_end = jnp.cumsum(padded)
    base = (padded_end - padded)[None, :] + jnp.cumsum(tile_cnt, axis=0) - tile_cnt
    route = route.reshape(b * nt, 8, tm)
    experts = route[:, 0:2, :].astype(jnp.int32)
    rank = route[:, 4:6, :].astype(jnp.int32)
    onehot = experts[..., None] == jnp.arange(ne, dtype=jnp.int32)
    dest = jnp.sum(jnp.where(onehot, base[:, None, None, :], 0), axis=-1) + rank
    dest = jnp.swapaxes(dest, 1, 2).reshape(b * nt, 1, 2 * tm)
    wts = jnp.swapaxes(route[:, 2:4, :], 1, 2).reshape(b, t, 2)
    n_blocks = -(-(n * 2) // rb) + ne
    block_e = jnp.minimum(
        jnp.searchsorted(padded_end, jnp.arange(n_blocks, dtype=jnp.int32) * rb, side="right"),
        ne - 1).astype(jnp.int32)
    n_used = (padded_end[-1:] // rb).astype(jnp.int32)
    rows = _dispatch(h.reshape(n, d), dest, n_blocks * rb)
    y_rows = _experts(rows, block_e, n_used, wg, wu, wd)
    return _combine(y_rows, dest, wts, x, mod, ln_final, final_norm)


def _block_diag(m, n):
    return jnp.kron(jnp.eye(n, dtype=m.dtype), m)


def _layer_params(i, w_in, conv_w, decay_w0, decay_w2, iclr_a0, iclr_a2, gate_g2, k_k, k_a, r_k,
                  lnx_w, lnx_b, w_fnet_out, w_rwkv_out, w_o):
    w = RWKV_WIDTH
    lora = decay_w2.shape[2]
    tail0 = FNET_WIDTH + 3 * w

    def reorder_tail(m):
        pad = jnp.zeros(m.shape[:-1] + (LANES - lora,), m.dtype)
        return jnp.concatenate([m[..., :3 * lora], pad, m[..., 3 * lora:]], axis=-1)

    row = lambda v: v.reshape(1, -1)
    hi_lo = lambda m: jnp.stack(_split_bf16(m, 2))
    wi = w_in[i]
    rwkv_cols = conv_w.shape[2]
    zeros = jnp.zeros((lora, w), F32)
    return {
        "w_a": wi[:, :FNET_WIDTH].astype(BF16),
        "w_main": wi[:, FNET_WIDTH:tail0].astype(BF16),
        "w_tail": reorder_tail(wi[:, tail0:FNET_WIDTH + rwkv_cols]).astype(BF16),
        "w_gates": wi[:, FNET_WIDTH + rwkv_cols:].astype(BF16),
        "conv_main": conv_w[i][:, :3 * w],
        "conv_tail": reorder_tail(conv_w[i][:, 3 * w:]),
        "decay_w0": decay_w0[i].reshape(1, 2 * w),
        "decay_w2": hi_lo(jnp.concatenate(
            [jnp.concatenate([decay_w2[i, 0], zeros], axis=1),
             jnp.concatenate([zeros, decay_w2[i, 1]], axis=1)], axis=0)),
        "iclr_a0": row(iclr_a0[i]),
        "iclr_a2": hi_lo(jnp.concatenate([iclr_a2[i], jnp.zeros((LANES - lora, w), F32)], axis=0)),
        "gate_g2": hi_lo(gate_g2[i]),
        "k_k": row(k_k[i]), "k_a": row(k_a[i]), "r_k": row(r_k[i]),
        "lnx_w": row(lnx_w[i]), "lnx_b": row(lnx_b[i]),
        "head_sum": _block_diag(jnp.ones((HEAD_DIM, HEAD_DIM), BF16), LANES // HEAD_DIM),
        "head_mean": _block_diag(jnp.full((HEAD_DIM, HEAD_DIM), 1.0 / HEAD_DIM, BF16),
                                 LANES // HEAD_DIM),
        "w_fnet_out": w_fnet_out[i].astype(BF16),
        "w_rwkv_out": w_rwkv_out[i].astype(BF16),
        "w_o": w_o[i].astype(BF16),
    }


def _channel_dft():
    k = jnp.arange(HEAD_DIM, dtype=jnp.int32)
    ang = ((k[:, None] * k[None, :]) % HEAD_DIM).astype(F32) * (2.0 * math.pi / HEAD_DIM)
    groups = FNET_WIDTH // HEAD_DIM
    return jnp.concatenate([_block_diag(jnp.cos(ang), groups), _block_diag(jnp.sin(ang), groups)],
                           axis=1).astype(BF16)


def _token_mixer(x, mod, ln, p, cs, ct, st):
    zc, zs, um, ut, gates = _inproj(x, mod, ln, p["w_a"], p["w_main"], p["w_tail"], p["w_gates"], cs)
    yf = _fnet(zc, zs, ct, st)
    r, k, v, a, b, e_f, e_b, g, bonus = _prep(um, ut, p)
    o_f = _wkv(r, k, v, a, b, e_f, False)
    o_b = _wkv(r, k, v, a, b, e_b, True)
    return _mixout(o_f, o_b, bonus, g, yf, gates, x, mod, p)


def kernel(x_prompt, x_sample, c_prompt, c_sample, w_ada, b_ada, ln_mix, w_in, conv_w, decay_w0, decay_w2, iclr_a0, iclr_a2, gate_g2, k_k, k_a, r_k, lnx_w, lnx_b, w_fnet_out, w_rwkv_out, w_o, ln_ffn, ff_w_gate, ff_w_up, ff_w_down, router_w, moe_w_gate, moe_w_up, moe_w_down, ln_final):
    depth, d = ln_mix.shape
    streams = [x_prompt, x_sample]
    nb = [x.shape[0] for x in streams]
    mod_all = _ada(jnp.concatenate([c_prompt, c_sample], axis=0), w_ada, b_ada)
    cs = _channel_dft()
    dfts = {}
    for x in streams:
        if x.shape[1] not in dfts:
            dfts[x.shape[1]] = _dft_matrices(x.shape[1])
    lnf = ln_final.reshape(1, d)
    for i in range(depth):
        p = _layer_params(i, w_in, conv_w, decay_w0, decay_w2, iclr_a0, iclr_a2, gate_g2, k_k, k_a,
                          r_k, lnx_w, lnx_b, w_fnet_out, w_rwkv_out, w_o)
        j = i // 2
        final = i == depth - 1
        if i % 2 == 0:
            ffw = (ff_w_gate[j].astype(BF16), ff_w_up[j].astype(BF16), ff_w_down[j].astype(BF16))
        else:
            ffw = (moe_w_gate[j].astype(BF16), moe_w_up[j].astype(BF16), moe_w_down[j].astype(BF16))
            router_t = router_w[j].T
        off = 0
        for s, x in enumerate(streams):
            mod = mod_all[i, off:off + nb[s]].reshape(nb[s], N_MOD, d)
            off += nb[s]
            x = _token_mixer(x, mod, ln_mix[i].reshape(1, d), p, cs, *dfts[x.shape[1]])
            ln2 = ln_ffn[i].reshape(1, d)
            if i % 2 == 0:
                x = _ffn_dense(x, mod, ln2, lnf, *ffw, final)
            else:
                x = _ffn_moe(x, mod, ln2, lnf, router_t, *ffw, final)
            streams[s] = x
    return tuple(streams)
```

```python
import functools
import math

import jax
import jax.numpy as jnp
from jax import lax
from jax.experimental import pallas as pl
from jax.experimental.pallas import tpu as pltpu

F32 = jnp.float32
BF16 = jnp.bfloat16
HIGHEST = lax.Precision.HIGHEST

HEAD_DIM = 64
LANES = 128
FNET_WIDTH = 256
RWKV_WIDTH = 768
LORA_TAIL = 384
N_MOD = 6
N_EXPERTS = 8
RMS_EPS = 1e-6
GN_EPS = 64e-5
WKV_CHUNK = 128
MOE_ROW_BLOCK = 512
VMEM_LIMIT = 56 * 1024 * 1024


def _params(sem, vmem=VMEM_LIMIT):
    return pltpu.CompilerParams(dimension_semantics=sem, vmem_limit_bytes=vmem)


def _const_spec(shape):
    nd = len(shape)
    return pl.BlockSpec(shape, lambda *_: (0,) * nd, pipeline_mode=pl.Buffered(1))


def _dot(a, b):
    return jnp.dot(a.astype(BF16), b.astype(BF16), preferred_element_type=F32)


def _dot_nt(a, b):
    return lax.dot_general(a.astype(BF16), b.astype(BF16), (((1,), (1,)), ((), ())),
                           preferred_element_type=F32)


def _dot_tn(a, b):
    return lax.dot_general(a.astype(BF16), b.astype(BF16), (((0,), (0,)), ((), ())),
                           preferred_element_type=F32)


def _split_bf16(x, parts):
    out, rem = [], x
    for _ in range(parts):
        p = rem.astype(BF16)
        out.append(p)
        rem = rem - p.astype(F32)
    return out


def _dot_exact_rhs(x, m, parts):
    acc = None
    for p in _split_bf16(x, parts):
        t = jnp.dot(p, m, preferred_element_type=F32)
        acc = t if acc is None else acc + t
    return acc


def _dot_f32(a, b):
    return jnp.dot(a, b, precision=HIGHEST, preferred_element_type=F32)


def _dot_3pass(a, w_ref):
    ah, al = _split_bf16(a, 2)
    d = lambda x, y: jnp.dot(x, y, preferred_element_type=F32)
    return d(ah, w_ref[0]) + (d(al, w_ref[0]) + d(ah, w_ref[1]))


def _norm_mod(x, gain, shift, scale):
    ms = jnp.mean(x * x, axis=-1, keepdims=True)
    return x * lax.rsqrt(ms + RMS_EPS) * gain * (1.0 + scale) + shift


def _silu(x):
    return x * jax.nn.sigmoid(x)


def _head_reduce(x, m):
    cols = [_dot_exact_rhs(x[:, j:j + LANES], m, 2) for j in range(0, x.shape[1], LANES)]
    return jnp.concatenate(cols, axis=1)


def _row_tile(t, want):
    tm = min(t, want)
    assert t % tm == 0
    return tm


def _sub_blocks(rows, size=512):
    size = min(size, rows)
    assert rows % size == 0
    return [slice(r, r + size) for r in range(0, rows, size)]


def _col_chunks(width, size):
    return [slice(c, min(c + size, width)) for c in range(0, width, size)]


def _ada_kernel(c_ref, w_ref, b_ref, o_ref):
    o_ref[0] = _dot_f32(_silu(c_ref[...]), w_ref[0]) + b_ref[0]


def _ada(c_all, w_ada, b_ada):
    depth, d, n = w_ada.shape
    bc = c_all.shape[0]
    tn = n // 4
    return pl.pallas_call(
        _ada_kernel,
        grid=(depth, n // tn),
        in_specs=[pl.BlockSpec((bc, d), lambda l, j: (0, 0)),
                  pl.BlockSpec((1, d, tn), lambda l, j: (l, 0, j)),
                  pl.BlockSpec((1, 1, tn), lambda l, j: (l, 0, j))],
        out_specs=pl.BlockSpec((1, bc, tn), lambda l, j: (l, 0, j)),
        out_shape=jax.ShapeDtypeStruct((depth, bc, n), F32),
        compiler_params=_params(("parallel", "parallel")),
        name="ada",
    )(c_all, w_ada, b_ada.reshape(depth, 1, n))


def _inproj_kernel(x_ref, mod_ref, ln_ref, wa_ref, wm_ref, wt_ref, wg_ref, cs_ref,
                   zc_ref, zs_ref, um_ref, ut_ref, g_ref):
    for rows in _sub_blocks(x_ref.shape[1]):
        h = _norm_mod(x_ref[0, rows, :], ln_ref[...], mod_ref[0, 0:1, :],
                      mod_ref[0, 1:2, :]).astype(BF16)
        ua = jnp.dot(h, wa_ref[...], preferred_element_type=F32)
        zz = jnp.dot(ua.astype(BF16), cs_ref[...], preferred_element_type=F32)
        zc_ref[0, rows, :] = zz[:, :FNET_WIDTH].astype(BF16)
        zs_ref[0, rows, :] = zz[:, FNET_WIDTH:].astype(BF16)
        for cols in _col_chunks(wm_ref.shape[1], 768):
            um_ref[0, rows, cols] = jnp.dot(
                h, wm_ref[:, cols], preferred_element_type=F32).astype(BF16)
        ut_ref[0, rows, :] = jnp.dot(h, wt_ref[...], preferred_element_type=F32)
        for cols in _col_chunks(wg_ref.shape[1], 512):
            g_ref[0, rows, cols] = jax.nn.sigmoid(
                jnp.dot(h, wg_ref[:, cols], preferred_element_type=F32)).astype(BF16)


def _inproj(x, mod, ln, wa, wm, wt, wg, cs):
    b, t, d = x.shape
    tm = _row_tile(t, 1024)
    row = lambda w: pl.BlockSpec((1, tm, w), lambda i, j: (i, j, 0))
    outs = [(FNET_WIDTH, BF16), (FNET_WIDTH, BF16), (wm.shape[1], BF16), (wt.shape[1], F32),
            (wg.shape[1], BF16)]
    return pl.pallas_call(
        _inproj_kernel,
        grid=(b, t // tm),
        in_specs=[row(d), pl.BlockSpec((1, N_MOD, d), lambda i, j: (i, 0, 0)),
                  _const_spec(ln.shape), _const_spec(wa.shape), _const_spec(wm.shape),
                  _const_spec(wt.shape), _const_spec(wg.shape), _const_spec(cs.shape)],
        out_specs=[row(w) for w, _ in outs],
        out_shape=[jax.ShapeDtypeStruct((b, t, w), dt) for w, dt in outs],
        compiler_params=_params(("parallel", "parallel")),
        name="inproj",
    )(x, mod, ln, wa, wm, wt, wg, cs)


def _fnet_kernel(ct_ref, st_ref, zc_ref, zs_ref, o_ref, *, scale):
    y = (jnp.dot(ct_ref[...], zc_ref[0], preferred_element_type=F32)
         - jnp.dot(st_ref[...], zs_ref[0], preferred_element_type=F32))
    o_ref[0] = (y * scale).astype(BF16)


def _dft_matrices(t):
    rr = 64
    k = lax.iota(jnp.int32, t)

    def table(rows):
        ang = ((rows[:, None] * k[None, :]) % t).astype(F32) * (2.0 * math.pi / t)
        return jnp.cos(ang), jnp.sin(ang)

    ca, sa = table(lax.iota(jnp.int32, t // rr) * rr)
    cb, sb = table(lax.iota(jnp.int32, rr))
    ca, sa, cb, sb = ca[:, None, :], sa[:, None, :], cb[None], sb[None]
    return ((ca * cb - sa * sb).reshape(t, t).astype(BF16),
            (sa * cb + ca * sb).reshape(t, t).astype(BF16))


def _fnet(zc, zs, ct, st):
    b, t, w = zc.shape
    tm = _row_tile(t, max(128, (2 * 1024 * 1024) // t))
    return pl.pallas_call(
        functools.partial(_fnet_kernel, scale=1.0 / math.sqrt(t * HEAD_DIM)),
        grid=(b, t // tm),
        in_specs=[pl.BlockSpec((tm, t), lambda i, j: (j, 0)),
                  pl.BlockSpec((tm, t), lambda i, j: (j, 0)),
                  pl.BlockSpec((1, t, w), lambda i, j: (i, 0, 0)),
                  pl.BlockSpec((1, t, w), lambda i, j: (i, 0, 0))],
        out_specs=pl.BlockSpec((1, tm, w), lambda i, j: (i, j, 0)),
        out_shape=jax.ShapeDtypeStruct((b, t, w), BF16),
        compiler_params=_params(("parallel", "arbitrary")),
        name="fnet",
    )(ct, st, zc, zs)


def _conv3(main, prev_row, next_row, w):
    tm = main.shape[0]
    ri = lax.broadcasted_iota(jnp.int32, main.shape, 0)
    up = jnp.where(ri == 0, prev_row, pltpu.roll(main, 1, 0))
    dn = jnp.where(ri == tm - 1, next_row, pltpu.roll(main, tm - 1, 0))
    return w[0:1] * up + w[1:2] * main + w[2:3] * dn


def _prep_kernel(um_ref, ump_ref, umn_ref, ut_ref, utp_ref, utn_ref, cwm_ref, cwt_ref,
                 w0_ref, w2_ref, a0_ref, a2_ref, g2_ref, kk_ref, ka_ref, rk_ref, hs_ref,
                 r_ref, k_ref, v_ref, a_ref, b_ref, ef_ref, eb_ref, g_ref, bonus_ref):
    j = pl.program_id(1)
    first = j == 0
    last = j == pl.num_programs(1) - 1
    hp = ump_ref.shape[1]

    def conv_main(c0, c1):
        prev = jnp.where(first, 0.0, ump_ref[0, hp - 1:hp, c0:c1].astype(F32))
        nxt = jnp.where(last, 0.0, umn_ref[0, 0:1, c0:c1].astype(F32))
        return _conv3(um_ref[0, :, c0:c1].astype(F32), prev, nxt, cwm_ref[:, c0:c1])

    w = RWKV_WIDTH
    r = conv_main(0, w)
    k = conv_main(w, 2 * w)
    v = conv_main(2 * w, 3 * w)
    tp = utp_ref.shape[1]
    tail = _conv3(ut_ref[0], jnp.where(first, 0.0, utp_ref[0, tp - 1:tp, :]),
                  jnp.where(last, 0.0, utn_ref[0, 0:1, :]), cwt_ref[...])
    xw, xa, xg = tail[:, 0:LANES], tail[:, LANES:2 * LANES], tail[:, 2 * LANES:3 * LANES]

    e = jax.nn.sigmoid(w0_ref[...] + _dot_3pass(jnp.tanh(xw), w2_ref)) * math.exp(-0.5)
    ef_ref[0] = e[:, :w]
    eb_ref[0] = e[:, w:]
    a = jax.nn.sigmoid(a0_ref[...] + _dot_3pass(xa, a2_ref))
    g_ref[0] = _dot_3pass(jax.nn.sigmoid(xg), g2_ref).astype(BF16)

    kk = k * kk_ref[...]
    kk = kk * lax.rsqrt(_head_reduce(kk * kk, hs_ref[...]) + 1e-12)
    k = k * (1.0 + (a - 1.0) * ka_ref[...])
    r_ref[0] = r.astype(BF16)
    k_ref[0] = k.astype(BF16)
    v_ref[0] = v.astype(BF16)
    a_ref[0] = (-kk).astype(BF16)
    b_ref[0] = (kk * a).astype(BF16)
    bonus_ref[0] = (_head_reduce(r * k * rk_ref[...], hs_ref[...]) * v).astype(BF16)


def _prep(um, ut, p):
    b, t, wm = um.shape
    wt = ut.shape[2]
    tm = _row_tile(t, 256)
    hm, ht = 16, 8
    nm, nt = tm // hm, tm // ht
    main = lambda w: pl.BlockSpec((1, tm, w), lambda i, j: (i, j, 0))
    prev = lambda h, n, w: pl.BlockSpec((1, h, w), lambda i, j: (i, jnp.maximum(j * n - 1, 0), 0))
    nxt = lambda h, n, w: pl.BlockSpec(
        (1, h, w), lambda i, j: (i, jnp.minimum((j + 1) * n, t // h - 1), 0))
    consts = [p["conv_main"], p["conv_tail"], p["decay_w0"], p["decay_w2"], p["iclr_a0"],
              p["iclr_a2"], p["gate_g2"], p["k_k"], p["k_a"], p["r_k"], p["head_sum"]]
    outs = [BF16] * 5 + [F32, F32, BF16, BF16]
    return pl.pallas_call(
        _prep_kernel,
        grid=(b, t // tm),
        in_specs=[main(wm), prev(hm, nm, wm), nxt(hm, nm, wm),
                  main(wt), prev(ht, nt, wt), nxt(ht, nt, wt)]
                 + [_const_spec(c.shape) for c in consts],
        out_specs=[main(RWKV_WIDTH) for _ in outs],
        out_shape=[jax.ShapeDtypeStruct((b, t, RWKV_WIDTH), dt) for dt in outs],
        compiler_params=_params(("parallel", "parallel")),
        name="rwkv_prep",
    )(um, um, um, ut, ut, ut, *consts)


def _wkv_kernel(r_ref, k_ref, v_ref, a_ref, b_ref, e_ref, o_ref, s_ref, *, reverse):
    c = r_ref.shape[1]
    pairs = range(s_ref.shape[0])
    heads = range(LANES // HEAD_DIM)
    chains = [(p, h) for p in pairs for h in heads]

    @pl.when(pl.program_id(1) == 0)
    def _():
        s_ref[...] = jnp.zeros_like(s_ref)

    row = lax.broadcasted_iota(jnp.int32, (c, c), 0)
    col = lax.broadcasted_iota(jnp.int32, (c, c), 1)
    if reverse:
        incl, strict = col >= row, col > row
    else:
        incl, strict = col <= row, col < row
    tri = jnp.where(incl, 1.0, 0.0).astype(BF16)
    end = 0 if reverse else c - 1
    lane = lax.broadcasted_iota(jnp.int32, (1, LANES), 1)
    in_head = [(lane >= h * HEAD_DIM) & (lane < (h + 1) * HEAD_DIM) for h in heads]
    same_head = (lax.broadcasted_iota(jnp.int32, (LANES, LANES), 0) // HEAD_DIM
                 == lax.broadcasted_iota(jnp.int32, (LANES, LANES), 1) // HEAD_DIM)

    per_row = r_ref.shape[2] // LANES

    def tile(ref, p):
        return ref[p // per_row, :, (p % per_row) * LANES:(p % per_row + 1) * LANES]

    tri2 = jnp.concatenate([tri, tri], axis=1)

    def running_sum(e):
        return jnp.dot(tri2, jnp.concatenate(_split_bf16(e, 2), axis=0), preferred_element_type=F32)

    cum = [running_sum(tile(e_ref, p)) for p in pairs]
    total = [x[end:end + 1, :] for x in cum]
    ref = [x[c // 2:c // 2 + 1, :] for x in cum]
    dl = [cum[p] - ref[p] for p in pairs]
    g_inv = [jnp.exp(dl[p]) for p in pairs]
    at = [tile(a_ref, p).astype(F32) * jnp.exp(tile(e_ref, p) - dl[p]) for p in pairs]
    rt = [tile(r_ref, p).astype(F32) * jnp.exp(-dl[p]) for p in pairs]
    bt = [tile(b_ref, p).astype(F32) * g_inv[p] for p in pairs]
    kt = [tile(k_ref, p).astype(F32) * g_inv[p] for p in pairs]
    g_end = [jnp.exp(ref[p] - total[p]) for p in pairs]
    bh = [(bt[p] * g_end[p]).astype(BF16) for p in pairs]
    kh = [(kt[p] * g_end[p]).astype(BF16) for p in pairs]
    bk = [jnp.concatenate([bt[p], kt[p]], axis=0).astype(BF16) for p in pairs]
    ar = [jnp.concatenate([at[p], rt[p]], axis=0) for p in pairs]
    vb = [tile(v_ref, p) for p in pairs]

    prod = [_dot_nt(jnp.where(in_head[h], ar[p], 0.0), bk[p]) for p, h in chains]
    l_ab = [jnp.where(strict, x[:c, :c], 0.0) for x in prod]
    l_ak = [jnp.where(strict, x[:c, c:], 0.0) for x in prod]
    m_rb = [jnp.where(incl, x[c:, :c], 0.0).astype(BF16) for x in prod]
    m_rk = [jnp.where(incl, x[c:, c:], 0.0).astype(BF16) for x in prod]
    del prod
    eye = jnp.where(row == col, 1.0, 0.0)
    t = [eye + jnp.where((row >> 1) == (col >> 1), x, 0.0) for x in l_ab]
    nch = range(len(chains))
    for sh in range(1, (c - 1).bit_length()):
        m = 1 << sh
        joins = ((row >> (sh + 1)) == (col >> (sh + 1))) & ((row >> sh) != (col >> sh))
        e_k = [jnp.where(joins, l_ab[i], 0.0) for i in nch]
        if m < 8:
            et = [_dot(e_k[i], t[i]) for i in nch]
            t = [t[i] + _dot(t[i], et[i]) for i in nch]
            continue
        live = [slice(b0 + (0 if reverse else m), b0 + (m if reverse else 2 * m))
                for b0 in range(0, c, 2 * m)]

        def take(v):
            return jnp.concatenate([v[rs] for rs in live], axis=0)

        def spread(v):
            zero = jnp.zeros((m, v.shape[1]), v.dtype)
            parts = []
            for j in range(len(live)):
                blk = v[j * m:(j + 1) * m]
                parts += [blk, zero] if reverse else [zero, blk]
            return jnp.concatenate(parts, axis=0)

        et = [_dot(take(e_k[i]), t[i]) for i in nch]
        upd = [_dot(take(t[i]), spread(et[i])) for i in nch]
        t = [t[i] + spread(upd[i]) for i in nch]
    lakv = [jnp.dot(l_ak[i].astype(BF16), vb[p], preferred_element_type=F32)
            for i, (p, h) in enumerate(chains)]
    x = [_dot(t[i], jnp.concatenate([at[p], lakv[i]], axis=1)) for i, (p, h) in enumerate(chains)]
    x_a = [v[:, :LANES] for v in x]
    x_u = [v[:, LANES:] for v in x]
    y_a = [jnp.dot(m_rb[i], x_a[i].astype(BF16), preferred_element_type=F32) for i in nch]
    y_o = [jnp.dot(jnp.concatenate([m_rb[i], m_rk[i]], axis=1),
                   jnp.concatenate([x_u[i].astype(BF16), vb[p]], axis=0),
                   preferred_element_type=F32) for i, (p, h) in enumerate(chains)]

    def pick(vals, p):
        out = vals[p * len(heads)]
        for h in heads[1:]:
            out = jnp.where(in_head[h], vals[p * len(heads) + h], out)
        return out

    a_solved = [pick(x_a, p) for p in pairs]
    u = [pick(x_u, p) for p in pairs]
    r_hat = [rt[p] + pick(y_a, p) for p in pairs]
    o_intra = [pick(y_o, p) for p in pairs]
    s = [s_ref[p] for p in pairs]
    s_hat = [(s[p] * jnp.exp(-ref[p])).astype(BF16) for p in pairs]
    out = [_dot_nt(r_hat[p], s_hat[p]) + o_intra[p] for p in pairs]
    g_t = [jnp.where(same_head, _dot_tn(a_solved[p], bh[p]), 0.0) for p in pairs]
    h_t = [jnp.where(same_head,
                     _dot_tn(jnp.concatenate([u[p].astype(BF16), vb[p]], axis=0),
                             jnp.concatenate([bh[p], kh[p]], axis=0)), 0.0) for p in pairs]
    s_new = [s[p] * jnp.exp(-total[p]) + _dot(s_hat[p], g_t[p]) + h_t[p] for p in pairs]
    for p in pairs:
        o_ref[p // per_row, :, (p % per_row) * LANES:(p % per_row + 1) * LANES] = out[p]
        s_ref[p] = s_new[p]


def _wkv(r, k, v, a, b, e, reverse):
    bsz, t, w = r.shape
    c = min(WKV_CHUNK, t)
    nc = t // c
    rows = math.gcd(bsz, 2)
    if reverse:
        idx = lambda i, j: (i, nc - 1 - j, 0)
    else:
        idx = lambda i, j: (i, j, 0)
    spec = pl.BlockSpec((rows, c, w), idx)
    return pl.pallas_call(
        functools.partial(_wkv_kernel, reverse=reverse),
        grid=(bsz // rows, nc),
        in_specs=[spec] * 6,
        out_specs=spec,
        out_shape=jax.ShapeDtypeStruct((bsz, t, w), F32),
        scratch_shapes=[pltpu.VMEM((rows * (w // LANES), LANES, LANES), F32)],
        compiler_params=_params(("parallel", "arbitrary")),
        name="wkv_bwd" if reverse else "wkv_fwd",
    )(r, k, v, a, b, e)


def _mixout_kernel(of_ref, ob_ref, bonus_ref, g_ref, yf_ref, gates_ref, x_ref, mod_ref,
                   lnw_ref, lnb_ref, hm_ref, wfo_ref, wro_ref, wo_ref, out_ref):
    d = x_ref.shape[2]
    for rows in _sub_blocks(x_ref.shape[1]):
        o = of_ref[0, rows, :] + ob_ref[0, rows, :]
        dev = o - _head_reduce(o, hm_ref[...])
        var = _head_reduce(dev * dev, hm_ref[...])
        o = dev * lax.rsqrt(var + GN_EPS) * lnw_ref[...] + lnb_ref[...]
        y = (o + bonus_ref[0, rows, :].astype(F32)) * g_ref[0, rows, :].astype(F32)
        y_b = _dot(y, wro_ref[...])
        y_a = jnp.dot(yf_ref[0, rows, :], wfo_ref[...], preferred_element_type=F32)
        merged = (gates_ref[0, rows, :d].astype(F32) * y_a
                  + gates_ref[0, rows, d:].astype(F32) * y_b)
        out_ref[0, rows, :] = x_ref[0, rows, :] + mod_ref[0, 2:3, :] * _dot(merged, wo_ref[...])


def _mixout(o_f, o_b, bonus, g, yf, gates, x, mod, p):
    b, t, d = x.shape
    tm = _row_tile(t, 1024)
    row = lambda a: pl.BlockSpec((1, tm, a.shape[2]), lambda i, j: (i, j, 0))
    consts = [p["lnx_w"], p["lnx_b"], p["head_mean"], p["w_fnet_out"], p["w_rwkv_out"], p["w_o"]]
    acts = [o_f, o_b, bonus, g, yf, gates, x]
    return pl.pallas_call(
        _mixout_kernel,
        grid=(b, t // tm),
        in_specs=[row(a) for a in acts] + [pl.BlockSpec((1, N_MOD, d), lambda i, j: (i, 0, 0))]
                 + [_const_spec(c.shape) for c in consts],
        out_specs=pl.BlockSpec((1, tm, d), lambda i, j: (i, j, 0)),
        out_shape=jax.ShapeDtypeStruct((b, t, d), F32),
        compiler_params=_params(("parallel", "parallel")),
        name="mixout",
    )(*acts, mod, *consts)


def _ffn_kernel(x_ref, mod_ref, ln_ref, lnf_ref, wg_ref, wu_ref, wd_ref, out_ref, *, final_norm):
    for rows in _sub_blocks(x_ref.shape[1]):
        x = x_ref[0, rows, :]
        h = _norm_mod(x, ln_ref[...], mod_ref[0, 3:4, :], mod_ref[0, 4:5, :]).astype(BF16)
        acc = None
        for cols in _col_chunks(wg_ref.shape[1], 768):
            gate = jnp.dot(h, wg_ref[:, cols], preferred_element_type=F32)
            up = jnp.dot(h, wu_ref[:, cols], preferred_element_type=F32)
            part = _dot(_silu(gate) * up, wd_ref[cols, :])
            acc = part if acc is None else acc + part
        out = x + mod_ref[0, 5:6, :] * acc
        if final_norm:
            out = _norm_mod(out, lnf_ref[...], 0.0, 0.0)
        out_ref[0, rows, :] = out


def _ffn_dense(x, mod, ln, ln_final, wg, wu, wd, final_norm):
    b, t, d = x.shape
    tm = _row_tile(t, 1024)
    consts = [ln, ln_final, wg, wu, wd]
    return pl.pallas_call(
        functools.partial(_ffn_kernel, final_norm=final_norm),
        grid=(b, t // tm),
        in_specs=[pl.BlockSpec((1, tm, d), lambda i, j: (i, j, 0)),
                  pl.BlockSpec((1, N_MOD, d), lambda i, j: (i, 0, 0))]
                 + [_const_spec(c.shape) for c in consts],
        out_specs=pl.BlockSpec((1, tm, d), lambda i, j: (i, j, 0)),
        out_shape=jax.ShapeDtypeStruct((b, t, d), F32),
        compiler_params=_params(("parallel", "parallel")),
        name="ffn_dense",
    )(x, mod, *consts)


def _route_kernel(x_ref, mod_ref, ln_ref, rw_ref, h_ref, route_ref, cnt_ref):
    h = _norm_mod(x_ref[0], ln_ref[...], mod_ref[0, 3:4, :], mod_ref[0, 4:5, :])
    h_ref[0] = h
    tm = h.shape[0]
    logits = lax.dot_general(rw_ref[...], h, (((1,), (1,)), ((), ())), precision=HIGHEST,
                             preferred_element_type=F32)
    ne = logits.shape[0]
    ei = lax.broadcasted_iota(jnp.int32, logits.shape, 0)
    m1 = jnp.max(logits, axis=0, keepdims=True)
    i1 = jnp.min(jnp.where(logits == m1, ei, ne), axis=0, keepdims=True)
    rest = jnp.where(ei == i1, -jnp.inf, logits)
    m2 = jnp.max(rest, axis=0, keepdims=True)
    i2 = jnp.min(jnp.where(rest == m2, ei, ne), axis=0, keepdims=True)
    t2 = jnp.exp(m2 - m1)
    w1 = 1.0 / (1.0 + t2)
    w2 = t2 / (1.0 + t2)
    oh1 = jnp.where(ei == i1, 1.0, 0.0)
    oh2 = jnp.where(ei == i2, 1.0, 0.0)
    before = (lax.broadcasted_iota(jnp.int32, (tm, tm), 0)
              < lax.broadcasted_iota(jnp.int32, (tm, tm), 1))
    before = jnp.where(before, 1.0, 0.0).astype(BF16)
    cs1 = jnp.dot(oh1.astype(BF16), before, preferred_element_type=F32)
    cs2 = jnp.dot(oh2.astype(BF16), before, preferred_element_type=F32)
    n1 = jnp.sum(oh1, axis=1, keepdims=True)
    n2 = jnp.sum(oh2, axis=1, keepdims=True)
    rank1 = jnp.sum(oh1 * cs1, axis=0, keepdims=True)
    rank2 = jnp.sum(oh2 * (cs2 + n1), axis=0, keepdims=True)
    zero = jnp.zeros_like(w1)
    route_ref[0, 0] = jnp.concatenate(
        [i1.astype(F32), i2.astype(F32), w1, w2, rank1, rank2, zero, zero], axis=0)
    cnt_ref[0, 0] = jnp.broadcast_to(n1 + n2, (ne, LANES))


def _route(x, mod, ln, router_t):
    b, t, d = x.shape
    tm = _row_tile(t, 256)
    nt = t // tm
    ne = router_t.shape[0]
    return pl.pallas_call(
        _route_kernel,
        grid=(b, nt),
        in_specs=[pl.BlockSpec((1, tm, d), lambda i, j: (i, j, 0)),
                  pl.BlockSpec((1, N_MOD, d), lambda i, j: (i, 0, 0)),
                  _const_spec(ln.shape), _const_spec(router_t.shape)],
        out_specs=[pl.BlockSpec((1, tm, d), lambda i, j: (i, j, 0)),
                   pl.BlockSpec((1, 1, 8, tm), lambda i, j: (i, j, 0, 0)),
                   pl.BlockSpec((1, 1, ne, LANES), lambda i, j: (i, j, 0, 0))],
        out_shape=[jax.ShapeDtypeStruct((b, t, d), F32),
                   jax.ShapeDtypeStruct((b, nt, 8, tm), F32),
                   jax.ShapeDtypeStruct((b, nt, ne, LANES), F32)],
        compiler_params=_params(("parallel", "parallel")),
        name="moe_route",
    )(x, mod, ln, router_t)


def _row_copy(src, dst, sem):
    return pltpu.make_async_copy(src, dst, sem)


def _dispatch_kernel(dest_ref, h_ref, rows_in_ref, rows_ref, sem):
    del rows_in_ref
    tm = h_ref.shape[0]

    def start(i, _):
        for s in range(2):
            _row_copy(h_ref.at[pl.ds(i, 1)], rows_ref.at[pl.ds(dest_ref[0, 0, 2 * i + s], 1)],
                      sem.at[s]).start()
        return 0

    lax.fori_loop(0, tm, start, 0, unroll=8)
    for s in range(2):
        _row_copy(h_ref, rows_ref.at[pl.ds(0, tm)], sem.at[s]).wait()


def _dispatch(h, dest, n_rows):
    n, d = h.shape
    tm = dest.shape[2] // 2
    rows = jnp.zeros((n_rows, d), h.dtype)
    return pl.pallas_call(
        _dispatch_kernel,
        grid=(n // tm,),
        in_specs=[pl.BlockSpec((1, 1, 2 * tm), lambda i: (i, 0, 0), memory_space=pltpu.SMEM),
                  pl.BlockSpec((tm, d), lambda i: (i, 0)),
                  pl.BlockSpec(memory_space=pl.ANY)],
        out_specs=pl.BlockSpec(memory_space=pl.ANY),
        out_shape=jax.ShapeDtypeStruct((n_rows, d), h.dtype),
        scratch_shapes=[pltpu.SemaphoreType.DMA((2,))],
        input_output_aliases={2: 0},
        compiler_params=pltpu.CompilerParams(dimension_semantics=("arbitrary",),
                                             vmem_limit_bytes=VMEM_LIMIT, has_side_effects=True),
        name="moe_dispatch",
    )(dest, h, rows)


def _experts_kernel(be_ref, nb_ref, x_ref, wg_ref, wu_ref, wd_ref, y_ref):
    del be_ref

    @pl.when(pl.program_id(0) < nb_ref[0])
    def _():
        x = x_ref[...].astype(BF16)
        f = wg_ref.shape[2]
        step = f // 7
        acc = None
        for c0 in range(0, f, step):
            gate = jnp.dot(x, wg_ref[0, :, c0:c0 + step], preferred_element_type=F32)
            up = jnp.dot(x, wu_ref[0, :, c0:c0 + step], preferred_element_type=F32)
            part = _dot(_silu(gate) * up, wd_ref[0, c0:c0 + step, :])
            acc = part if acc is None else acc + part
        y_ref[...] = acc

    @pl.when(pl.program_id(0) >= nb_ref[0])
    def _():
        y_ref[...] = jnp.zeros_like(y_ref)


def _experts(rows, block_e, n_used, wg, wu, wd):
    n_rows, d = rows.shape
    rb = MOE_ROW_BLOCK
    f = wg.shape[2]
    wspec = lambda shape: pl.BlockSpec((1,) + shape, lambda i, be, nb: (be[i], 0, 0),
                                       pipeline_mode=pl.Buffered(1))
    return pl.pallas_call(
        _experts_kernel,
        grid_spec=pltpu.PrefetchScalarGridSpec(
            num_scalar_prefetch=2,
            grid=(n_rows // rb,),
            in_specs=[pl.BlockSpec((rb, d), lambda i, be, nb: (i, 0)),
                      wspec((d, f)), wspec((d, f)), wspec((f, d))],
            out_specs=pl.BlockSpec((rb, d), lambda i, be, nb: (i, 0))),
        out_shape=jax.ShapeDtypeStruct((n_rows, d), F32),
        compiler_params=_params(("arbitrary",)),
        name="moe_experts",
    )(block_e, n_used, rows, wg, wu, wd)


def _combine_kernel(dest_ref, y_hbm, x_ref, wts_ref, mod_ref, lnf_ref, out_ref, buf, sem,
                    *, final_norm):
    tm = x_ref.shape[1]

    def start(i, _):
        for s in range(2):
            _row_copy(y_hbm.at[pl.ds(dest_ref[0, 0, 2 * i + s], 1)], buf.at[s, pl.ds(i, 1)],
                      sem.at[s]).start()
        return 0

    lax.fori_loop(0, tm, start, 0, unroll=8)
    for s in range(2):
        _row_copy(y_hbm.at[pl.ds(0, tm)], buf.at[s], sem.at[s]).wait()
    f = wts_ref[0, :, 0:1] * buf[0] + wts_ref[0, :, 1:2] * buf[1]
    out = x_ref[0] + mod_ref[0, 5:6, :] * f
    if final_norm:
        out = _norm_mod(out, lnf_ref[...], 0.0, 0.0)
    out_ref[0] = out


def _combine(y_rows, dest, wts, x, mod, ln_final, final_norm):
    b, t, d = x.shape
    tm = dest.shape[2] // 2
    nt = t // tm
    return pl.pallas_call(
        functools.partial(_combine_kernel, final_norm=final_norm),
        grid=(b, nt),
        in_specs=[pl.BlockSpec((1, 1, 2 * tm), lambda i, j: (i * nt + j, 0, 0),
                               memory_space=pltpu.SMEM),
                  pl.BlockSpec(memory_space=pl.ANY),
                  pl.BlockSpec((1, tm, d), lambda i, j: (i, j, 0)),
                  pl.BlockSpec((1, tm, 2), lambda i, j: (i, j, 0)),
                  pl.BlockSpec((1, N_MOD, d), lambda i, j: (i, 0, 0)),
                  _const_spec(ln_final.shape)],
        out_specs=pl.BlockSpec((1, tm, d), lambda i, j: (i, j, 0)),
        out_shape=jax.ShapeDtypeStruct((b, t, d), F32),
        scratch_shapes=[pltpu.VMEM((2, tm, d), F32), pltpu.SemaphoreType.DMA((2,))],
        compiler_params=_params(("arbitrary", "arbitrary")),
        name="moe_combine",
    )(dest, y_rows, x, wts, mod, ln_final)


def _ffn_moe(x, mod, ln, ln_final, router_t, wg, wu, wd, final_norm):
    b, t, d = x.shape
    n = b * t
    ne = router_t.shape[0]
    rb = MOE_ROW_BLOCK
    h, route, cnt = _route(x, mod, ln, router_t)
    nt, tm = route.shape[1], route.shape[3]
    tile_cnt = cnt[..., 0].reshape(b * nt, ne).astype(jnp.int32)
    totals = jnp.sum(tile_cnt, axis=0)
    padded = ((totals + rb - 1) // rb) * rb
    padded_end = jnp.cumsum(padded)
    base = (padded_end - padded)[None, :] + jnp.cumsum(tile_cnt, axis=0) - tile_cnt
    route = route.reshape(b * nt, 8, tm)
    experts = route[:, 0:2, :].astype(jnp.int32)
    rank = route[:, 4:6, :].astype(jnp.int32)
    onehot = experts[..., None] == jnp.arange(ne, dtype=jnp.int32)
    dest = jnp.sum(jnp.where(onehot, base[:, None, None, :], 0), axis=-1) + rank
    dest = jnp.swapaxes(dest, 1, 2).reshape(b * nt, 1, 2 * tm)
    wts = jnp.swapaxes(route[:, 2:4, :], 1, 2).reshape(b, t, 2)
    n_blocks = -(-(n * 2) // rb) + ne
    block_e = jnp.minimum(
        jnp.searchsorted(padded_end, jnp.arange(n_blocks, dtype=jnp.int32) * rb, side="right"),
        ne - 1).astype(jnp.int32)
    n_used = (padded_end[-1:] // rb).astype(jnp.int32)
    rows = _dispatch(h.reshape(n, d), dest, n_blocks * rb)
    y_rows = _experts(rows, block_e, n_used, wg, wu, wd)
    return _combine(y_rows, dest, wts, x, mod, ln_final, final_norm)


def _block_diag(m, n):
    return jnp.kron(jnp.eye(n, dtype=m.dtype), m)


def _layer_params(i, w_in, conv_w, decay_w0, decay_w2, iclr_a0, iclr_a2, gate_g2, k_k, k_a, r_k,
                  lnx_w, lnx_b, w_fnet_out, w_rwkv_out, w_o):
    w = RWKV_WIDTH
    lora = decay_w2.shape[2]
    tail0 = FNET_WIDTH + 3 * w

    def reorder_tail(m):
        pad = jnp.zeros(m.shape[:-1] + (LANES - lora,), m.dtype)
        return jnp.concatenate([m[..., :3 * lora], pad, m[..., 3 * lora:]], axis=-1)

    row = lambda v: v.reshape(1, -1)
    hi_lo = lambda m: jnp.stack(_split_bf16(m, 2))
    wi = w_in[i]
    rwkv_cols = conv_w.shape[2]
    zeros = jnp.zeros((lora, w), F32)
    return {
        "w_a": wi[:, :FNET_WIDTH].astype(BF16),
        "w_main": wi[:, FNET_WIDTH:tail0].astype(BF16),
        "w_tail": reorder_tail(wi[:, tail0:FNET_WIDTH + rwkv_cols]).astype(BF16),
        "w_gates": wi[:, FNET_WIDTH + rwkv_cols:].astype(BF16),
        "conv_main": conv_w[i][:, :3 * w],
        "conv_tail": reorder_tail(conv_w[i][:, 3 * w:]),
        "decay_w0": decay_w0[i].reshape(1, 2 * w),
        "decay_w2": hi_lo(jnp.concatenate(
            [jnp.concatenate([decay_w2[i, 0], zeros], axis=1),
             jnp.concatenate([zeros, decay_w2[i, 1]], axis=1)], axis=0)),
        "iclr_a0": row(iclr_a0[i]),
        "iclr_a2": hi_lo(jnp.concatenate([iclr_a2[i], jnp.zeros((LANES - lora, w), F32)], axis=0)),
        "gate_g2": hi_lo(gate_g2[i]),
        "k_k": row(k_k[i]), "k_a": row(k_a[i]), "r_k": row(r_k[i]),
        "lnx_w": row(lnx_w[i]), "lnx_b": row(lnx_b[i]),
        "head_sum": _block_diag(jnp.ones((HEAD_DIM, HEAD_DIM), BF16), LANES // HEAD_DIM),
        "head_mean": _block_diag(jnp.full((HEAD_DIM, HEAD_DIM), 1.0 / HEAD_DIM, BF16),
                                 LANES // HEAD_DIM),
        "w_fnet_out": w_fnet_out[i].astype(BF16),
        "w_rwkv_out": w_rwkv_out[i].astype(BF16),
        "w_o": w_o[i].astype(BF16),
    }


def _channel_dft():
    k = jnp.arange(HEAD_DIM, dtype=jnp.int32)
    ang = ((k[:, None] * k[None, :]) % HEAD_DIM).astype(F32) * (2.0 * math.pi / HEAD_DIM)
    groups = FNET_WIDTH // HEAD_DIM
    return jnp.concatenate([_block_diag(jnp.cos(ang), groups), _block_diag(jnp.sin(ang), groups)],
                           axis=1).astype(BF16)


def _token_mixer(x, mod, ln, p, cs, ct, st):
    zc, zs, um, ut, gates = _inproj(x, mod, ln, p["w_a"], p["w_main"], p["w_tail"], p["w_gates"], cs)
    yf = _fnet(zc, zs, ct, st)
    r, k, v, a, b, e_f, e_b, g, bonus = _prep(um, ut, p)
    o_f = _wkv(r, k, v, a, b, e_f, False)
    o_b = _wkv(r, k, v, a, b, e_b, True)
    return _mixout(o_f, o_b, bonus, g, yf, gates, x, mod, p)


def kernel(x_prompt, x_sample, c_prompt, c_sample, w_ada, b_ada, ln_mix, w_in, conv_w, decay_w0, decay_w2, iclr_a0, iclr_a2, gate_g2, k_k, k_a, r_k, lnx_w, lnx_b, w_fnet_out, w_rwkv_out, w_o, ln_ffn, ff_w_gate, ff_w_up, ff_w_down, router_w, moe_w_gate, moe_w_up, moe_w_down, ln_final):
    depth, d = ln_mix.shape
    streams = [x_prompt, x_sample]
    nb = [x.shape[0] for x in streams]
    mod_all = _ada(jnp.concatenate([c_prompt, c_sample], axis=0), w_ada, b_ada)
    cs = _channel_dft()
    dfts = {}
    for x in streams:
        if x.shape[1] not in dfts:
            dfts[x.shape[1]] = _dft_matrices(x.shape[1])
    lnf = ln_final.reshape(1, d)
    for i in range(depth):
        p = _layer_params(i, w_in, conv_w, decay_w0, decay_w2, iclr_a0, iclr_a2, gate_g2, k_k, k_a,
                          r_k, lnx_w, lnx_b, w_fnet_out, w_rwkv_out, w_o)
        j = i // 2
        final = i == depth - 1
        if i % 2 == 0:
            ffw = (ff_w_gate[j].astype(BF16), ff_w_up[j].astype(BF16), ff_w_down[j].astype(BF16))
        else:
            ffw = (moe_w_gate[j].astype(BF16), moe_w_up[j].astype(BF16), moe_w_down[j].astype(BF16))
            router_t = router_w[j].T
        off = 0
        for s, x in enumerate(streams):
            mod = mod_all[i, off:off + nb[s]].reshape(nb[s], N_MOD, d)
            off += nb[s]
            x = _token_mixer(x, mod, ln_mix[i].reshape(1, d), p, cs, *dfts[x.shape[1]])
            ln2 = ln_ffn[i].reshape(1, d)
            if i % 2 == 0:
                x = _ffn_dense(x, mod, ln2, lnf, *ffw, final)
            else:
                x = _ffn_moe(x, mod, ln2, lnf, router_t, *ffw, final)
            streams[s] = x
    return tuple(streams)
```

```python
import functools
import math

import jax
import jax.numpy as jnp
from jax import lax
from jax.experimental import pallas as pl
from jax.experimental.pallas import tpu as pltpu

F32 = jnp.float32
BF16 = jnp.bfloat16
HIGHEST = lax.Precision.HIGHEST

HEAD_DIM = 64
LANES = 128
FNET_WIDTH = 256
RWKV_WIDTH = 768
LORA_TAIL = 384
N_MOD = 6
N_EXPERTS = 8
RMS_EPS = 1e-6
GN_EPS = 64e-5
WKV_CHUNK = 128
MOE_ROW_BLOCK = 512
VMEM_LIMIT = 56 * 1024 * 1024


def _params(sem, vmem=VMEM_LIMIT):
    return pltpu.CompilerParams(dimension_semantics=sem, vmem_limit_bytes=vmem)


def _const_spec(shape):
    nd = len(shape)
    return pl.BlockSpec(shape, lambda *_: (0,) * nd, pipeline_mode=pl.Buffered(1))


def _dot(a, b):
    return jnp.dot(a.astype(BF16), b.astype(BF16), preferred_element_type=F32)


def _dot_nt(a, b):
    return lax.dot_general(a.astype(BF16), b.astype(BF16), (((1,), (1,)), ((), ())),
                           preferred_element_type=F32)


def _dot_tn(a, b):
    return lax.dot_general(a.astype(BF16), b.astype(BF16), (((0,), (0,)), ((), ())),
                           preferred_element_type=F32)


def _split_bf16(x, parts):
    out, rem = [], x
    for _ in range(parts):
        p = rem.astype(BF16)
        out.append(p)
        rem = rem - p.astype(F32)
    return out


def _dot_exact_rhs(x, m, parts):
    acc = None
    for p in _split_bf16(x, parts):
        t = jnp.dot(p, m, preferred_element_type=F32)
        acc = t if acc is None else acc + t
    return acc


def _dot_f32(a, b):
    return jnp.dot(a, b, precision=HIGHEST, preferred_element_type=F32)


def _dot_3pass(a, w_ref):
    ah, al = _split_bf16(a, 2)
    d = lambda x, y: jnp.dot(x, y, preferred_element_type=F32)
    return d(ah, w_ref[0]) + (d(al, w_ref[0]) + d(ah, w_ref[1]))


def _norm_mod(x, gain, shift, scale):
    ms = jnp.mean(x * x, axis=-1, keepdims=True)
    return x * lax.rsqrt(ms + RMS_EPS) * gain * (1.0 + scale) + shift


def _silu(x):
    return x * jax.nn.sigmoid(x)


def _head_reduce(x, m):
    cols = [_dot_exact_rhs(x[:, j:j + LANES], m, 2) for j in range(0, x.shape[1], LANES)]
    return jnp.concatenate(cols, axis=1)


def _row_tile(t, want):
    tm = min(t, want)
    assert t % tm == 0
    return tm


def _sub_blocks(rows, size=512):
    size = min(size, rows)
    assert rows % size == 0
    return [slice(r, r + size) for r in range(0, rows, size)]


def _col_chunks(width, size):
    return [slice(c, min(c + size, width)) for c in range(0, width, size)]


def _ada_kernel(c_ref, w_ref, b_ref, o_ref):
    o_ref[0] = _dot_f32(_silu(c_ref[...]), w_ref[0]) + b_ref[0]


def _ada(c_all, w_ada, b_ada):
    depth, d, n = w_ada.shape
    bc = c_all.shape[0]
    tn = n // 4
    return pl.pallas_call(
        _ada_kernel,
        grid=(depth, n // tn),
        in_specs=[pl.BlockSpec((bc, d), lambda l, j: (0, 0)),
                  pl.BlockSpec((1, d, tn), lambda l, j: (l, 0, j)),
                  pl.BlockSpec((1, 1, tn), lambda l, j: (l, 0, j))],
        out_specs=pl.BlockSpec((1, bc, tn), lambda l, j: (l, 0, j)),
        out_shape=jax.ShapeDtypeStruct((depth, bc, n), F32),
        compiler_params=_params(("parallel", "parallel")),
        name="ada",
    )(c_all, w_ada, b_ada.reshape(depth, 1, n))


def _inproj_kernel(x_ref, mod_ref, ln_ref, wa_ref, wm_ref, wt_ref, wg_ref, cs_ref,
                   zc_ref, zs_ref, um_ref, ut_ref, g_ref):
    for rows in _sub_blocks(x_ref.shape[1]):
        h = _norm_mod(x_ref[0, rows, :], ln_ref[...], mod_ref[0, 0:1, :],
                      mod_ref[0, 1:2, :]).astype(BF16)
        ua = jnp.dot(h, wa_ref[...], preferred_element_type=F32)
        zz = jnp.dot(ua.astype(BF16), cs_ref[...], preferred_element_type=F32)
        zc_ref[0, rows, :] = zz[:, :FNET_WIDTH].astype(BF16)
        zs_ref[0, rows, :] = zz[:, FNET_WIDTH:].astype(BF16)
        for cols in _col_chunks(wm_ref.shape[1], 768):
            um_ref[0, rows, cols] = jnp.dot(
                h, wm_ref[:, cols], preferred_element_type=F32).astype(BF16)
        ut_ref[0, rows, :] = jnp.dot(h, wt_ref[...], preferred_element_type=F32)
        for cols in _col_chunks(wg_ref.shape[1], 512):
            g_ref[0, rows, cols] = jax.nn.sigmoid(
                jnp.dot(h, wg_ref[:, cols], preferred_element_type=F32)).astype(BF16)


def _inproj(x, mod, ln, wa, wm, wt, wg, cs):
    b, t, d = x.shape
    tm = _row_tile(t, 1024)
    row = lambda w: pl.BlockSpec((1, tm, w), lambda i, j: (i, j, 0))
    outs = [(FNET_WIDTH, BF16), (FNET_WIDTH, BF16), (wm.shape[1], BF16), (wt.shape[1], F32),
            (wg.shape[1], BF16)]
    return pl.pallas_call(
        _inproj_kernel,
        grid=(b, t // tm),
        in_specs=[row(d), pl.BlockSpec((1, N_MOD, d), lambda i, j: (i, 0, 0)),
                  _const_spec(ln.shape), _const_spec(wa.shape), _const_spec(wm.shape),
                  _const_spec(wt.shape), _const_spec(wg.shape), _const_spec(cs.shape)],
        out_specs=[row(w) for w, _ in outs],
        out_shape=[jax.ShapeDtypeStruct((b, t, w), dt) for w, dt in outs],
        compiler_params=_params(("parallel", "parallel")),
        name="inproj",
    )(x, mod, ln, wa, wm, wt, wg, cs)


FNET_T2 = 64
FNET_COLS = 2048


def _fnet_stage1_kernel(d_ref, zc_ref, zs_ref, twc_ref, tws_ref, gr_ref, gi_ref):
    t2 = zc_ref.shape[1]
    w = gr_ref.shape[3]
    g = jnp.dot(d_ref[...], jnp.concatenate([zc_ref[0], zs_ref[0]], axis=0),
                preferred_element_type=F32)
    for j in range(gr_ref.shape[1]):
        g_r = g[:t2, j * w:(j + 1) * w]
        g_i = g[t2:, j * w:(j + 1) * w]
        c = jnp.concatenate([twc_ref[j]] * (w // LANES), axis=1)
        s = jnp.concatenate([tws_ref[j]] * (w // LANES), axis=1)
        gr_ref[0, j] = (g_r * c + g_i * s).astype(BF16)
        gi_ref[0, j] = (g_i * c - g_r * s).astype(BF16)


def _fnet_stage2_kernel(d_ref, gr_ref, gi_ref, o_ref, *, scale):
    y = jnp.dot(d_ref[...], jnp.concatenate([gr_ref[0], gi_ref[0]], axis=0),
                preferred_element_type=F32)
    o_ref[0] = (y * scale).astype(BF16)


def _dft_tables(t):
    t2 = FNET_T2
    t1 = t // t2
    assert t1 * t2 == t

    def cos_sin(rows, cols, n):
        k = (lax.iota(jnp.int32, rows)[:, None] * lax.iota(jnp.int32, cols)[None, :]) % n
        ang = k.astype(F32) * (2.0 * math.pi / n)
        return jnp.cos(ang), jnp.sin(ang)

    c2, s2 = cos_sin(t2, t2, t2)
    c1, s1 = cos_sin(t1, t1, t1)
    twc, tws = cos_sin(t1, t2, t)
    lanes = lambda m: jnp.broadcast_to(m[:, :, None], (t1, t2, LANES))
    return {
        "stage1": jnp.block([[c2, -s2], [-s2, -c2]]).astype(BF16),
        "stage2": jnp.concatenate([c1, s1], axis=1).astype(BF16),
        "twc": lanes(twc), "tws": lanes(tws),
    }


def _fnet(zc, zs, tab):
    b, t, w = zc.shape
    t2 = FNET_T2
    t1 = t // t2
    nb = min(FNET_COLS // w, t1)
    in_view = lambda z: z.reshape(b, t2, t1 * w)
    g_shape = jax.ShapeDtypeStruct((b, t1, t2, w), BF16)
    g_r, g_i = pl.pallas_call(
        _fnet_stage1_kernel,
        grid=(b, t1 // nb),
        in_specs=[_const_spec(tab["stage1"].shape),
                  pl.BlockSpec((1, t2, nb * w), lambda i, j: (i, 0, j)),
                  pl.BlockSpec((1, t2, nb * w), lambda i, j: (i, 0, j)),
                  pl.BlockSpec((nb, t2, LANES), lambda i, j: (j, 0, 0)),
                  pl.BlockSpec((nb, t2, LANES), lambda i, j: (j, 0, 0))],
        out_specs=[pl.BlockSpec((1, nb, t2, w), lambda i, j: (i, j, 0, 0))] * 2,
        out_shape=[g_shape, g_shape],
        compiler_params=_params(("parallel", "parallel")),
        name="fnet_stage1",
    )(tab["stage1"], in_view(zc), in_view(zs), tab["twc"], tab["tws"])
    cols = min(FNET_COLS, t2 * w)
    spec = pl.BlockSpec((1, t1, cols), lambda i, j: (i, 0, j))
    y = pl.pallas_call(
        functools.partial(_fnet_stage2_kernel, scale=1.0 / math.sqrt(t * HEAD_DIM)),
        grid=(b, (t2 * w) // cols),
        in_specs=[_const_spec(tab["stage2"].shape), spec, spec],
        out_specs=spec,
        out_shape=jax.ShapeDtypeStruct((b, t1, t2 * w), BF16),
        compiler_params=_params(("parallel", "parallel")),
        name="fnet_stage2",
    )(tab["stage2"], g_r.reshape(b, t1, t2 * w), g_i.reshape(b, t1, t2 * w))
    return y.reshape(b, t, w)


def _conv3(main, prev_row, next_row, w):
    tm = main.shape[0]
    ri = lax.broadcasted_iota(jnp.int32, main.shape, 0)
    up = jnp.where(ri == 0, prev_row, pltpu.roll(main, 1, 0))
    dn = jnp.where(ri == tm - 1, next_row, pltpu.roll(main, tm - 1, 0))
    return w[0:1] * up + w[1:2] * main + w[2:3] * dn


def _prep_kernel(um_ref, ump_ref, umn_ref, ut_ref, utp_ref, utn_ref, cwm_ref, cwt_ref,
                 w0_ref, w2_ref, a0_ref, a2_ref, g2_ref, kk_ref, ka_ref, rk_ref, hs_ref, shift_ref,
                 r_ref, k_ref, v_ref, a_ref, b_ref, ef_ref, eb_ref, g_ref, bonus_ref):
    j = pl.program_id(1)
    first = j == 0
    last = j == pl.num_programs(1) - 1
    hp = ump_ref.shape[1]

    def conv_main(c0, c1):
        main = um_ref[0, :, c0:c1]
        tm = main.shape[0]
        shifted = jnp.dot(shift_ref[...], main, preferred_element_type=F32)
        ri = lax.broadcasted_iota(jnp.int32, main.shape, 0)
        prev = jnp.where(first, 0.0, ump_ref[0, hp - 1:hp, c0:c1].astype(F32))
        nxt = jnp.where(last, 0.0, umn_ref[0, 0:1, c0:c1].astype(F32))
        up = jnp.where(ri == 0, prev, shifted[:tm])
        dn = jnp.where(ri == tm - 1, nxt, shifted[tm:])
        w = cwm_ref[:, c0:c1]
        return w[0:1] * up + w[1:2] * main.astype(F32) + w[2:3] * dn

    w = RWKV_WIDTH
    r = conv_main(0, w)
    k = conv_main(w, 2 * w)
    v = conv_main(2 * w, 3 * w)
    tp = utp_ref.shape[1]
    tail = _conv3(ut_ref[0], jnp.where(first, 0.0, utp_ref[0, tp - 1:tp, :]),
                  jnp.where(last, 0.0, utn_ref[0, 0:1, :]), cwt_ref[...])
    xw, xa, xg = tail[:, 0:LANES], tail[:, LANES:2 * LANES], tail[:, 2 * LANES:3 * LANES]

    e = jax.nn.sigmoid(w0_ref[...] + _dot_3pass(jnp.tanh(xw), w2_ref)) * math.exp(-0.5)
    ef_ref[0] = e[:, :w]
    eb_ref[0] = e[:, w:]
    a = jax.nn.sigmoid(a0_ref[...] + _dot_3pass(xa, a2_ref))
    g_ref[0] = _dot_3pass(jax.nn.sigmoid(xg), g2_ref).astype(BF16)

    kk = k * kk_ref[...]
    kk = kk * lax.rsqrt(_head_reduce(kk * kk, hs_ref[...]) + 1e-12)
    k = k * (1.0 + (a - 1.0) * ka_ref[...])
    r_ref[0] = r.astype(BF16)
    k_ref[0] = k.astype(BF16)
    v_ref[0] = v.astype(BF16)
    a_ref[0] = (-kk).astype(BF16)
    b_ref[0] = (kk * a).astype(BF16)
    bonus_ref[0] = (_head_reduce(r * k * rk_ref[...], hs_ref[...]) * v).astype(BF16)


def _prep(um, ut, p):
    b, t, wm = um.shape
    wt = ut.shape[2]
    tm = _row_tile(t, 256)
    hm, ht = 16, 8
    nm, nt = tm // hm, tm // ht
    main = lambda w: pl.BlockSpec((1, tm, w), lambda i, j: (i, j, 0))
    prev = lambda h, n, w: pl.BlockSpec((1, h, w), lambda i, j: (i, jnp.maximum(j * n - 1, 0), 0))
    nxt = lambda h, n, w: pl.BlockSpec(
        (1, h, w), lambda i, j: (i, jnp.minimum((j + 1) * n, t // h - 1), 0))
    consts = [p["conv_main"], p["conv_tail"], p["decay_w0"], p["decay_w2"], p["iclr_a0"],
              p["iclr_a2"], p["gate_g2"], p["k_k"], p["k_a"], p["r_k"], p["head_sum"],
              jnp.concatenate([jnp.eye(tm, k=-1, dtype=BF16), jnp.eye(tm, k=1, dtype=BF16)], axis=0)]
    outs = [BF16] * 5 + [F32, F32, BF16, BF16]
    return pl.pallas_call(
        _prep_kernel,
        grid=(b, t // tm),
        in_specs=[main(wm), prev(hm, nm, wm), nxt(hm, nm, wm),
                  main(wt), prev(ht, nt, wt), nxt(ht, nt, wt)]
                 + [_const_spec(c.shape) for c in consts],
        out_specs=[main(RWKV_WIDTH) for _ in outs],
        out_shape=[jax.ShapeDtypeStruct((b, t, RWKV_WIDTH), dt) for dt in outs],
        compiler_params=_params(("parallel", "parallel")),
        name="rwkv_prep",
    )(um, um, um, ut, ut, ut, *consts)


def _wkv_kernel(r_ref, k_ref, v_ref, a_ref, b_ref, e_ref, o_ref, s_ref, *, reverse):
    c = r_ref.shape[1]
    pairs = range(s_ref.shape[0])
    heads = range(LANES // HEAD_DIM)
    chains = [(p, h) for p in pairs for h in heads]

    @pl.when(pl.program_id(1) == 0)
    def _():
        s_ref[...] = jnp.zeros_like(s_ref)

    row = lax.broadcasted_iota(jnp.int32, (c, c), 0)
    col = lax.broadcasted_iota(jnp.int32, (c, c), 1)
    if reverse:
        incl, strict = col >= row, col > row
    else:
        incl, strict = col <= row, col < row
    tri = jnp.where(incl, 1.0, 0.0).astype(BF16)
    end = 0 if reverse else c - 1
    lane = lax.broadcasted_iota(jnp.int32, (1, LANES), 1)
    in_head = [(lane >= h * HEAD_DIM) & (lane < (h + 1) * HEAD_DIM) for h in heads]
    same_head = (lax.broadcasted_iota(jnp.int32, (LANES, LANES), 0) // HEAD_DIM
                 == lax.broadcasted_iota(jnp.int32, (LANES, LANES), 1) // HEAD_DIM)

    per_row = r_ref.shape[2] // LANES

    def tile(ref, p):
        return ref[p // per_row, :, (p % per_row) * LANES:(p % per_row + 1) * LANES]

    tri2 = jnp.concatenate([tri, tri], axis=1)

    def running_sum(e):
        return jnp.dot(tri2, jnp.concatenate(_split_bf16(e, 2), axis=0), preferred_element_type=F32)

    cum = [running_sum(tile(e_ref, p)) for p in pairs]
    total = [x[end:end + 1, :] for x in cum]
    ref = [x[c // 2:c // 2 + 1, :] for x in cum]
    dl = [cum[p] - ref[p] for p in pairs]
    g_inv = [jnp.exp(dl[p]) for p in pairs]
    at = [tile(a_ref, p).astype(F32) * jnp.exp(tile(e_ref, p) - dl[p]) for p in pairs]
    rt = [tile(r_ref, p).astype(F32) * jnp.exp(-dl[p]) for p in pairs]
    bt = [tile(b_ref, p).astype(F32) * g_inv[p] for p in pairs]
    kt = [tile(k_ref, p).astype(F32) * g_inv[p] for p in pairs]
    g_end = [jnp.exp(ref[p] - total[p]) for p in pairs]
    bh = [(bt[p] * g_end[p]).astype(BF16) for p in pairs]
    kh = [(kt[p] * g_end[p]).astype(BF16) for p in pairs]
    bk = [jnp.concatenate([bt[p], kt[p]], axis=0).astype(BF16) for p in pairs]
    ar = [jnp.concatenate([at[p], rt[p]], axis=0) for p in pairs]
    vb = [tile(v_ref, p) for p in pairs]

    prod = [_dot_nt(jnp.where(in_head[h], ar[p], 0.0), bk[p]) for p, h in chains]
    l_ab = [jnp.where(strict, x[:c, :c], 0.0) for x in prod]
    l_ak = [jnp.where(strict, x[:c, c:], 0.0) for x in prod]
    m_rb = [jnp.where(incl, x[c:, :c], 0.0).astype(BF16) for x in prod]
    m_rk = [jnp.where(incl, x[c:, c:], 0.0).astype(BF16) for x in prod]
    del prod
    eye = jnp.where(row == col, 1.0, 0.0)
    t = [eye + jnp.where((row >> 1) == (col >> 1), x, 0.0) for x in l_ab]
    nch = range(len(chains))
    for sh in range(1, (c - 1).bit_length()):
        m = 1 << sh
        joins = ((row >> (sh + 1)) == (col >> (sh + 1))) & ((row >> sh) != (col >> sh))
        e_k = [jnp.where(joins, l_ab[i], 0.0) for i in nch]
        if m < 8:
            et = [_dot(e_k[i], t[i]) for i in nch]
            t = [t[i] + _dot(t[i], et[i]) for i in nch]
            continue
        live = [slice(b0 + (0 if reverse else m), b0 + (m if reverse else 2 * m))
                for b0 in range(0, c, 2 * m)]

        def take(v):
            return jnp.concatenate([v[rs] for rs in live], axis=0)

        def spread(v):
            zero = jnp.zeros((m, v.shape[1]), v.dtype)
            parts = []
            for j in range(len(live)):
                blk = v[j * m:(j + 1) * m]
                parts += [blk, zero] if reverse else [zero, blk]
            return jnp.concatenate(parts, axis=0)

        et = [_dot(take(e_k[i]), t[i]) for i in nch]
        upd = [_dot(take(t[i]), spread(et[i])) for i in nch]
        t = [t[i] + spread(upd[i]) for i in nch]
    lakv = [jnp.dot(l_ak[i].astype(BF16), vb[p], preferred_element_type=F32)
            for i, (p, h) in enumerate(chains)]
    x = [_dot(t[i], jnp.concatenate([at[p], lakv[i]], axis=1)) for i, (p, h) in enumerate(chains)]
    x_a = [v[:, :LANES] for v in x]
    x_u = [v[:, LANES:] for v in x]
    y_a = [jnp.dot(m_rb[i], x_a[i].astype(BF16), preferred_element_type=F32) for i in nch]
    y_o = [jnp.dot(jnp.concatenate([m_rb[i], m_rk[i]], axis=1),
                   jnp.concatenate([x_u[i].astype(BF16), vb[p]], axis=0),
                   preferred_element_type=F32) for i, (p, h) in enumerate(chains)]

    def pick(vals, p):
        out = vals[p * len(heads)]
        for h in heads[1:]:
            out = jnp.where(in_head[h], vals[p * len(heads) + h], out)
        return out

    a_solved = [pick(x_a, p) for p in pairs]
    u = [pick(x_u, p) for p in pairs]
    r_hat = [rt[p] + pick(y_a, p) for p in pairs]
    o_intra = [pick(y_o, p) for p in pairs]
    s = [s_ref[p] for p in pairs]
    s_hat = [(s[p] * jnp.exp(-ref[p])).astype(BF16) for p in pairs]
    out = [_dot_nt(r_hat[p], s_hat[p]) + o_intra[p] for p in pairs]
    g_t = [jnp.where(same_head, _dot_tn(a_solved[p], bh[p]), 0.0) for p in pairs]
    h_t = [jnp.where(same_head,
                     _dot_tn(jnp.concatenate([u[p].astype(BF16), vb[p]], axis=0),
                             jnp.concatenate([bh[p], kh[p]], axis=0)), 0.0) for p in pairs]
    s_new = [s[p] * jnp.exp(-total[p]) + _dot(s_hat[p], g_t[p]) + h_t[p] for p in pairs]
    for p in pairs:
        o_ref[p // per_row, :, (p % per_row) * LANES:(p % per_row + 1) * LANES] = out[p]
        s_ref[p] = s_new[p]


def _wkv(r, k, v, a, b, e, reverse):
    bsz, t, w = r.shape
    c = min(WKV_CHUNK, t)
    nc = t // c
    rows = math.gcd(bsz, 2)
    if reverse:
        idx = lambda i, j: (i, nc - 1 - j, 0)
    else:
        idx = lambda i, j: (i, j, 0)
    spec = pl.BlockSpec((rows, c, w), idx)
    return pl.pallas_call(
        functools.partial(_wkv_kernel, reverse=reverse),
        grid=(bsz // rows, nc),
        in_specs=[spec] * 6,
        out_specs=spec,
        out_shape=jax.ShapeDtypeStruct((bsz, t, w), F32),
        scratch_shapes=[pltpu.VMEM((rows * (w // LANES), LANES, LANES), F32)],
        compiler_params=_params(("parallel", "arbitrary")),
        name="wkv_bwd" if reverse else "wkv_fwd",
    )(r, k, v, a, b, e)


def _mixout_kernel(of_ref, ob_ref, bonus_ref, g_ref, yf_ref, gates_ref, x_ref, mod_ref,
                   lnw_ref, lnb_ref, hm_ref, wfo_ref, wro_ref, wo_ref, out_ref):
    d = x_ref.shape[2]
    for rows in _sub_blocks(x_ref.shape[1]):
        o = of_ref[0, rows, :] + ob_ref[0, rows, :]
        dev = o - _head_reduce(o, hm_ref[...])
        var = _head_reduce(dev * dev, hm_ref[...])
        o = dev * lax.rsqrt(var + GN_EPS) * lnw_ref[...] + lnb_ref[...]
        y = (o + bonus_ref[0, rows, :].astype(F32)) * g_ref[0, rows, :].astype(F32)
        y_b = _dot(y, wro_ref[...])
        y_a = jnp.dot(yf_ref[0, rows, :], wfo_ref[...], preferred_element_type=F32)
        merged = (gates_ref[0, rows, :d].astype(F32) * y_a
                  + gates_ref[0, rows, d:].astype(F32) * y_b)
        out_ref[0, rows, :] = x_ref[0, rows, :] + mod_ref[0, 2:3, :] * _dot(merged, wo_ref[...])


def _mixout(o_f, o_b, bonus, g, yf, gates, x, mod, p):
    b, t, d = x.shape
    tm = _row_tile(t, 1024)
    row = lambda a: pl.BlockSpec((1, tm, a.shape[2]), lambda i, j: (i, j, 0))
    consts = [p["lnx_w"], p["lnx_b"], p["head_mean"], p["w_fnet_out"], p["w_rwkv_out"], p["w_o"]]
    acts = [o_f, o_b, bonus, g, yf, gates, x]
    return pl.pallas_call(
        _mixout_kernel,
        grid=(b, t // tm),
        in_specs=[row(a) for a in acts] + [pl.BlockSpec((1, N_MOD, d), lambda i, j: (i, 0, 0))]
                 + [_const_spec(c.shape) for c in consts],
        out_specs=pl.BlockSpec((1, tm, d), lambda i, j: (i, j, 0)),
        out_shape=jax.ShapeDtypeStruct((b, t, d), F32),
        compiler_params=_params(("parallel", "parallel")),
        name="mixout",
    )(*acts, mod, *consts)


def _ffn_kernel(x_ref, mod_ref, ln_ref, lnf_ref, wg_ref, wu_ref, wd_ref, out_ref, *, final_norm):
    for rows in _sub_blocks(x_ref.shape[1]):
        x = x_ref[0, rows, :]
        h = _norm_mod(x, ln_ref[...], mod_ref[0, 3:4, :], mod_ref[0, 4:5, :]).astype(BF16)
        acc = None
        for cols in _col_chunks(wg_ref.shape[1], 768):
            gate = jnp.dot(h, wg_ref[:, cols], preferred_element_type=F32)
            up = jnp.dot(h, wu_ref[:, cols], preferred_element_type=F32)
            part = _dot(_silu(gate) * up, wd_ref[cols, :])
            acc = part if acc is None else acc + part
        out = x + mod_ref[0, 5:6, :] * acc
        if final_norm:
            out = _norm_mod(out, lnf_ref[...], 0.0, 0.0)
        out_ref[0, rows, :] = out


def _ffn_dense(x, mod, ln, ln_final, wg, wu, wd, final_norm):
    b, t, d = x.shape
    tm = _row_tile(t, 1024)
    consts = [ln, ln_final, wg, wu, wd]
    return pl.pallas_call(
        functools.partial(_ffn_kernel, final_norm=final_norm),
        grid=(b, t // tm),
        in_specs=[pl.BlockSpec((1, tm, d), lambda i, j: (i, j, 0)),
                  pl.BlockSpec((1, N_MOD, d), lambda i, j: (i, 0, 0))]
                 + [_const_spec(c.shape) for c in consts],
        out_specs=pl.BlockSpec((1, tm, d), lambda i, j: (i, j, 0)),
        out_shape=jax.ShapeDtypeStruct((b, t, d), F32),
        compiler_params=_params(("parallel", "parallel")),
        name="ffn_dense",
    )(x, mod, *consts)


def _route_kernel(x_ref, mod_ref, ln_ref, rw_ref, h_ref, route_ref, cnt_ref):
    h = _norm_mod(x_ref[0], ln_ref[...], mod_ref[0, 3:4, :], mod_ref[0, 4:5, :])
    h_ref[0] = h
    tm = h.shape[0]
    logits = lax.dot_general(rw_ref[...], h, (((1,), (1,)), ((), ())), precision=HIGHEST,
                             preferred_element_type=F32)
    ne = logits.shape[0]
    ei = lax.broadcasted_iota(jnp.int32, logits.shape, 0)
    m1 = jnp.max(logits, axis=0, keepdims=True)
    i1 = jnp.min(jnp.where(logits == m1, ei, ne), axis=0, keepdims=True)
    rest = jnp.where(ei == i1, -jnp.inf, logits)
    m2 = jnp.max(rest, axis=0, keepdims=True)
    i2 = jnp.min(jnp.where(rest == m2, ei, ne), axis=0, keepdims=True)
    t2 = jnp.exp(m2 - m1)
    w1 = 1.0 / (1.0 + t2)
    w2 = t2 / (1.0 + t2)
    oh1 = jnp.where(ei == i1, 1.0, 0.0)
    oh2 = jnp.where(ei == i2, 1.0, 0.0)
    before = (lax.broadcasted_iota(jnp.int32, (tm, tm), 0)
              < lax.broadcasted_iota(jnp.int32, (tm, tm), 1))
    before = jnp.where(before, 1.0, 0.0).astype(BF16)
    cs1 = jnp.dot(oh1.astype(BF16), before, preferred_element_type=F32)
    cs2 = jnp.dot(oh2.astype(BF16), before, preferred_element_type=F32)
    n1 = jnp.sum(oh1, axis=1, keepdims=True)
    n2 = jnp.sum(oh2, axis=1, keepdims=True)
    rank1 = jnp.sum(oh1 * cs1, axis=0, keepdims=True)
    rank2 = jnp.sum(oh2 * (cs2 + n1), axis=0, keepdims=True)
    zero = jnp.zeros_like(w1)
    route_ref[0, 0] = jnp.concatenate(
        [i1.astype(F32), i2.astype(F32), w1, w2, rank1, rank2, zero, zero], axis=0)
    cnt_ref[0, 0] = jnp.broadcast_to(n1 + n2, (ne, LANES))


def _route(x, mod, ln, router_t):
    b, t, d = x.shape
    tm = _row_tile(t, 512)
    nt = t // tm
    ne = router_t.shape[0]
    return pl.pallas_call(
        _route_kernel,
        grid=(b, nt),
        in_specs=[pl.BlockSpec((1, tm, d), lambda i, j: (i, j, 0)),
                  pl.BlockSpec((1, N_MOD, d), lambda i, j: (i, 0, 0)),
                  _const_spec(ln.shape), _const_spec(router_t.shape)],
        out_specs=[pl.BlockSpec((1, tm, d), lambda i, j: (i, j, 0)),
                   pl.BlockSpec((1, 1, 8, tm), lambda i, j: (i, j, 0, 0)),
                   pl.BlockSpec((1, 1, ne, LANES), lambda i, j: (i, j, 0, 0))],
        out_shape=[jax.ShapeDtypeStruct((b, t, d), F32),
                   jax.ShapeDtypeStruct((b, nt, 8, tm), F32),
                   jax.ShapeDtypeStruct((b, nt, ne, LANES), F32)],
        compiler_params=_params(("parallel", "parallel")),
        name="moe_route",
    )(x, mod, ln, router_t)


def _row_copy(src, dst, sem):
    return pltpu.make_async_copy(src, dst, sem)


def _dispatch_kernel(dest_ref, h_ref, rows_in_ref, rows_ref, sem):
    del rows_in_ref
    tm = h_ref.shape[0]

    def start(i, _):
        for s in range(2):
            _row_copy(h_ref.at[pl.ds(i, 1)], rows_ref.at[pl.ds(dest_ref[0, 0, 2 * i + s], 1)],
                      sem.at[s]).start()
        return 0

    lax.fori_loop(0, tm, start, 0, unroll=8)
    for s in range(2):
        _row_copy(h_ref, rows_ref.at[pl.ds(0, tm)], sem.at[s]).wait()


def _dispatch(h, dest, n_rows):
    n, d = h.shape
    tm = dest.shape[2] // 2
    rows = jnp.zeros((n_rows, d), h.dtype)
    return pl.pallas_call(
        _dispatch_kernel,
        grid=(n // tm,),
        in_specs=[pl.BlockSpec((1, 1, 2 * tm), lambda i: (i, 0, 0), memory_space=pltpu.SMEM),
                  pl.BlockSpec((tm, d), lambda i: (i, 0)),
                  pl.BlockSpec(memory_space=pl.ANY)],
        out_specs=pl.BlockSpec(memory_space=pl.ANY),
        out_shape=jax.ShapeDtypeStruct((n_rows, d), h.dtype),
        scratch_shapes=[pltpu.SemaphoreType.DMA((2,))],
        input_output_aliases={2: 0},
        compiler_params=pltpu.CompilerParams(dimension_semantics=("arbitrary",),
                                             vmem_limit_bytes=VMEM_LIMIT, has_side_effects=True),
        name="moe_dispatch",
    )(dest, h, rows)


def _experts_kernel(be_ref, nb_ref, x_ref, wg_ref, wu_ref, wd_ref, y_ref):
    del be_ref

    @pl.when(pl.program_id(0) < nb_ref[0])
    def _():
        x = x_ref[...].astype(BF16)
        f = wg_ref.shape[2]
        step = f // 7
        acc = None
        for c0 in range(0, f, step):
            gate = jnp.dot(x, wg_ref[0, :, c0:c0 + step], preferred_element_type=F32)
            up = jnp.dot(x, wu_ref[0, :, c0:c0 + step], preferred_element_type=F32)
            part = _dot(_silu(gate) * up, wd_ref[0, c0:c0 + step, :])
            acc = part if acc is None else acc + part
        y_ref[...] = acc

    @pl.when(pl.program_id(0) >= nb_ref[0])
    def _():
        y_ref[...] = jnp.zeros_like(y_ref)


def _experts(rows, block_e, n_used, wg, wu, wd):
    n_rows, d = rows.shape
    rb = MOE_ROW_BLOCK
    f = wg.shape[2]
    wspec = lambda shape: pl.BlockSpec((1,) + shape, lambda i, be, nb: (be[i], 0, 0),
                                       pipeline_mode=pl.Buffered(1))
    return pl.pallas_call(
        _experts_kernel,
        grid_spec=pltpu.PrefetchScalarGridSpec(
            num_scalar_prefetch=2,
            grid=(n_rows // rb,),
            in_specs=[pl.BlockSpec((rb, d), lambda i, be, nb: (i, 0)),
                      wspec((d, f)), wspec((d, f)), wspec((f, d))],
            out_specs=pl.BlockSpec((rb, d), lambda i, be, nb: (i, 0))),
        out_shape=jax.ShapeDtypeStruct((n_rows, d), F32),
        compiler_params=_params(("arbitrary",)),
        name="moe_experts",
    )(block_e, n_used, rows, wg, wu, wd)


def _combine_kernel(dest_ref, y_hbm, x_ref, wts_ref, mod_ref, lnf_ref, out_ref, buf, sem,
                    *, final_norm):
    tm = x_ref.shape[1]

    def start(i, _):
        for s in range(2):
            _row_copy(y_hbm.at[pl.ds(dest_ref[0, 0, 2 * i + s], 1)], buf.at[s, pl.ds(i, 1)],
                      sem.at[s]).start()
        return 0

    lax.fori_loop(0, tm, start, 0, unroll=8)
    for s in range(2):
        _row_copy(y_hbm.at[pl.ds(0, tm)], buf.at[s], sem.at[s]).wait()
    f = wts_ref[0, :, 0:1] * buf[0] + wts_ref[0, :, 1:2] * buf[1]
    out = x_ref[0] + mod_ref[0, 5:6, :] * f
    if final_norm:
        out = _norm_mod(out, lnf_ref[...], 0.0, 0.0)
    out_ref[0] = out


def _combine(y_rows, dest, wts, x, mod, ln_final, final_norm):
    b, t, d = x.shape
    tm = dest.shape[2] // 2
    nt = t // tm
    return pl.pallas_call(
        functools.partial(_combine_kernel, final_norm=final_norm),
        grid=(b, nt),
        in_specs=[pl.BlockSpec((1, 1, 2 * tm), lambda i, j: (i * nt + j, 0, 0),
                               memory_space=pltpu.SMEM),
                  pl.BlockSpec(memory_space=pl.ANY),
                  pl.BlockSpec((1, tm, d), lambda i, j: (i, j, 0)),
                  pl.BlockSpec((1, tm, 2), lambda i, j: (i, j, 0)),
                  pl.BlockSpec((1, N_MOD, d), lambda i, j: (i, 0, 0)),
                  _const_spec(ln_final.shape)],
        out_specs=pl.BlockSpec((1, tm, d), lambda i, j: (i, j, 0)),
        out_shape=jax.ShapeDtypeStruct((b, t, d), F32),
        scratch_shapes=[pltpu.VMEM((2, tm, d), F32), pltpu.SemaphoreType.DMA((2,))],
        compiler_params=_params(("arbitrary", "arbitrary")),
        name="moe_combine",
    )(dest, y_rows, x, wts, mod, ln_final)


def _ffn_moe(x, mod, ln, ln_final, router_t, wg, wu, wd, final_norm):
    b, t, d = x.shape
    n = b * t
    ne = router_t.shape[0]
    rb = MOE_ROW_BLOCK
    h, route, cnt = _route(x, mod, ln, router_t)
    nt, tm = route.shape[1], route.shape[3]
    tile_cnt = cnt[..., 0].reshape(b * nt, ne).astype(jnp.int32)
    totals = jnp.sum(tile_cnt, axis=0)
    padded = ((totals + rb - 1) // rb) * rb
    padded_end = jnp.cumsum(padded)
    base = (padded_end - padded)[None, :] + jnp.cumsum(tile_cnt, axis=0) - tile_cnt
    route = route.reshape(b * nt, 8, tm)
    experts = route[:, 0:2, :].astype(jnp.int32)
    rank = route[:, 4:6, :].astype(jnp.int32)
    onehot = experts[..., None] == jnp.arange(ne, dtype=jnp.int32)
    dest = jnp.sum(jnp.where(onehot, base[:, None, None, :], 0), axis=-1) + rank
    dest = jnp.swapaxes(dest, 1, 2).reshape(b * nt, 1, 2 * tm)
    wts = jnp.swapaxes(route[:, 2:4, :], 1, 2).reshape(b, t, 2)
    n_blocks = -(-(n * 2) // rb) + ne
    block_e = jnp.minimum(
        jnp.searchsorted(padded_end, jnp.arange(n_blocks, dtype=jnp.int32) * rb, side="right"),
        ne - 1).astype(jnp.int32)
    n_used = (padded_end[-1:] // rb).astype(jnp.int32)
    rows = _dispatch(h.reshape(n, d), dest, n_blocks * rb)
    y_rows = _experts(rows, block_e, n_used, wg, wu, wd)
    return _combine(y_rows, dest, wts, x, mod, ln_final, final_norm)


def _block_diag(m, n):
    return jnp.kron(jnp.eye(n, dtype=m.dtype), m)


def _layer_params(i, w_in, conv_w, decay_w0, decay_w2, iclr_a0, iclr_a2, gate_g2, k_k, k_a, r_k,
                  lnx_w, lnx_b, w_fnet_out, w_rwkv_out, w_o):
    w = RWKV_WIDTH
    lora = decay_w2.shape[2]
    tail0 = FNET_WIDTH + 3 * w

    def reorder_tail(m):
        pad = jnp.zeros(m.shape[:-1] + (LANES - lora,), m.dtype)
        return jnp.concatenate([m[..., :3 * lora], pad, m[..., 3 * lora:]], axis=-1)

    row = lambda v: v.reshape(1, -1)
    hi_lo = lambda m: jnp.stack(_split_bf16(m, 2))
    wi = w_in[i]
    rwkv_cols = conv_w.shape[2]
    zeros = jnp.zeros((lora, w), F32)
    return {
        "w_a": wi[:, :FNET_WIDTH].astype(BF16),
        "w_main": wi[:, FNET_WIDTH:tail0].astype(BF16),
        "w_tail": reorder_tail(wi[:, tail0:FNET_WIDTH + rwkv_cols]).astype(BF16),
        "w_gates": wi[:, FNET_WIDTH + rwkv_cols:].astype(BF16),
        "conv_main": conv_w[i][:, :3 * w],
        "conv_tail": reorder_tail(conv_w[i][:, 3 * w:]),
        "decay_w0": decay_w0[i].reshape(1, 2 * w),
        "decay_w2": hi_lo(jnp.concatenate(
            [jnp.concatenate([decay_w2[i, 0], zeros], axis=1),
             jnp.concatenate([zeros, decay_w2[i, 1]], axis=1)], axis=0)),
        "iclr_a0": row(iclr_a0[i]),
        "iclr_a2": hi_lo(jnp.concatenate([iclr_a2[i], jnp.zeros((LANES - lora, w), F32)], axis=0)),
        "gate_g2": hi_lo(gate_g2[i]),
        "k_k": row(k_k[i]), "k_a": row(k_a[i]), "r_k": row(r_k[i]),
        "lnx_w": row(lnx_w[i]), "lnx_b": row(lnx_b[i]),
        "head_sum": _block_diag(jnp.ones((HEAD_DIM, HEAD_DIM), BF16), LANES // HEAD_DIM),
        "head_mean": _block_diag(jnp.full((HEAD_DIM, HEAD_DIM), 1.0 / HEAD_DIM, BF16),
                                 LANES // HEAD_DIM),
        "w_fnet_out": w_fnet_out[i].astype(BF16),
        "w_rwkv_out": w_rwkv_out[i].astype(BF16),
        "w_o": w_o[i].astype(BF16),
    }


def _channel_dft():
    k = jnp.arange(HEAD_DIM, dtype=jnp.int32)
    ang = ((k[:, None] * k[None, :]) % HEAD_DIM).astype(F32) * (2.0 * math.pi / HEAD_DIM)
    groups = FNET_WIDTH // HEAD_DIM
    return jnp.concatenate([_block_diag(jnp.cos(ang), groups), _block_diag(jnp.sin(ang), groups)],
                           axis=1).astype(BF16)


def _token_mixer(x, mod, ln, p, cs, dft_tab):
    zc, zs, um, ut, gates = _inproj(x, mod, ln, p["w_a"], p["w_main"], p["w_tail"], p["w_gates"], cs)
    yf = _fnet(zc, zs, dft_tab)
    r, k, v, a, b, e_f, e_b, g, bonus = _prep(um, ut, p)
    o_f = _wkv(r, k, v, a, b, e_f, False)
    o_b = _wkv(r, k, v, a, b, e_b, True)
    return _mixout(o_f, o_b, bonus, g, yf, gates, x, mod, p)


def kernel(x_prompt, x_sample, c_prompt, c_sample, w_ada, b_ada, ln_mix, w_in, conv_w, decay_w0, decay_w2, iclr_a0, iclr_a2, gate_g2, k_k, k_a, r_k, lnx_w, lnx_b, w_fnet_out, w_rwkv_out, w_o, ln_ffn, ff_w_gate, ff_w_up, ff_w_down, router_w, moe_w_gate, moe_w_up, moe_w_down, ln_final):
    depth, d = ln_mix.shape
    streams = [x_prompt, x_sample]
    nb = [x.shape[0] for x in streams]
    mod_all = _ada(jnp.concatenate([c_prompt, c_sample], axis=0), w_ada, b_ada)
    cs = _channel_dft()
    dfts = {}
    for x in streams:
        if x.shape[1] not in dfts:
            dfts[x.shape[1]] = _dft_tables(x.shape[1])
    lnf = ln_final.reshape(1, d)
    for i in range(depth):
        p = _layer_params(i, w_in, conv_w, decay_w0, decay_w2, iclr_a0, iclr_a2, gate_g2, k_k, k_a,
                          r_k, lnx_w, lnx_b, w_fnet_out, w_rwkv_out, w_o)
        j = i // 2
        final = i == depth - 1
        if i % 2 == 0:
            ffw = (ff_w_gate[j].astype(BF16), ff_w_up[j].astype(BF16), ff_w_down[j].astype(BF16))
        else:
            ffw = (moe_w_gate[j].astype(BF16), moe_w_up[j].astype(BF16), moe_w_down[j].astype(BF16))
            router_t = router_w[j].T
        off = 0
        for s, x in enumerate(streams):
            mod = mod_all[i, off:off + nb[s]].reshape(nb[s], N_MOD, d)
            off += nb[s]
            x = _token_mixer(x, mod, ln_mix[i].reshape(1, d), p, cs, dfts[x.shape[1]])
            ln2 = ln_ffn[i].reshape(1, d)
            if i % 2 == 0:
                x = _ffn_dense(x, mod, ln2, lnf, *ffw, final)
            else:
                x = _ffn_moe(x, mod, ln2, lnf, router_t, *ffw, final)
            streams[s] = x
    return tuple(streams)
```

```python
import functools
import math

import jax
import jax.numpy as jnp
from jax import lax
from jax.experimental import pallas as pl
from jax.experimental.pallas import tpu as pltpu

F32 = jnp.float32
BF16 = jnp.bfloat16
HIGHEST = lax.Precision.HIGHEST

HEAD_DIM = 64
LANES = 128
FNET_WIDTH = 256
RWKV_WIDTH = 768
LORA_TAIL = 384
N_MOD = 6
N_EXPERTS = 8
RMS_EPS = 1e-6
GN_EPS = 64e-5
WKV_CHUNK = 128
MOE_ROW_BLOCK = 512
VMEM_LIMIT = 56 * 1024 * 1024


def _params(sem, vmem=VMEM_LIMIT):
    return pltpu.CompilerParams(dimension_semantics=sem, vmem_limit_bytes=vmem)


def _const_spec(shape):
    nd = len(shape)
    return pl.BlockSpec(shape, lambda *_: (0,) * nd, pipeline_mode=pl.Buffered(1))


def _dot(a, b):
    return jnp.dot(a.astype(BF16), b.astype(BF16), preferred_element_type=F32)


def _dot_nt(a, b):
    return lax.dot_general(a.astype(BF16), b.astype(BF16), (((1,), (1,)), ((), ())),
                           preferred_element_type=F32)


def _dot_tn(a, b):
    return lax.dot_general(a.astype(BF16), b.astype(BF16), (((0,), (0,)), ((), ())),
                           preferred_element_type=F32)


def _split_bf16(x, parts):
    out, rem = [], x
    for _ in range(parts):
        p = rem.astype(BF16)
        out.append(p)
        rem = rem - p.astype(F32)
    return out


def _dot_exact_rhs(x, m, parts):
    acc = None
    for p in _split_bf16(x, parts):
        t = jnp.dot(p, m, preferred_element_type=F32)
        acc = t if acc is None else acc + t
    return acc


def _dot_f32(a, b):
    return jnp.dot(a, b, precision=HIGHEST, preferred_element_type=F32)


def _dot_3pass(a, w_ref):
    ah, al = _split_bf16(a, 2)
    d = lambda x, y: jnp.dot(x, y, preferred_element_type=F32)
    return d(ah, w_ref[0]) + (d(al, w_ref[0]) + d(ah, w_ref[1]))


def _norm_mod(x, gain, shift, scale):
    ms = jnp.mean(x * x, axis=-1, keepdims=True)
    return x * lax.rsqrt(ms + RMS_EPS) * gain * (1.0 + scale) + shift


def _silu(x):
    return x * jax.nn.sigmoid(x)


def _head_reduce(x, m):
    cols = [_dot_exact_rhs(x[:, j:j + LANES], m, 2) for j in range(0, x.shape[1], LANES)]
    return jnp.concatenate(cols, axis=1)


def _row_tile(t, want):
    tm = min(t, want)
    assert t % tm == 0
    return tm


def _sub_blocks(rows, size=512):
    size = min(size, rows)
    assert rows % size == 0
    return [slice(r, r + size) for r in range(0, rows, size)]


def _col_chunks(width, size):
    return [slice(c, min(c + size, width)) for c in range(0, width, size)]


def _ada_kernel(c_ref, w_ref, b_ref, o_ref):
    o_ref[0] = _dot_f32(_silu(c_ref[...]), w_ref[0]) + b_ref[0]


def _ada(c_all, w_ada, b_ada):
    depth, d, n = w_ada.shape
    bc = c_all.shape[0]
    tn = n // 4
    return pl.pallas_call(
        _ada_kernel,
        grid=(depth, n // tn),
        in_specs=[pl.BlockSpec((bc, d), lambda l, j: (0, 0)),
                  pl.BlockSpec((1, d, tn), lambda l, j: (l, 0, j)),
                  pl.BlockSpec((1, 1, tn), lambda l, j: (l, 0, j))],
        out_specs=pl.BlockSpec((1, bc, tn), lambda l, j: (l, 0, j)),
        out_shape=jax.ShapeDtypeStruct((depth, bc, n), F32),
        compiler_params=_params(("parallel", "parallel")),
        name="ada",
    )(c_all, w_ada, b_ada.reshape(depth, 1, n))


def _inproj_kernel(x_ref, mod_ref, ln_ref, wa_ref, wm_ref, wt_ref, wg_ref, cs_ref,
                   zc_ref, zs_ref, um_ref, ut_ref, g_ref):
    for rows in _sub_blocks(x_ref.shape[1]):
        h = _norm_mod(x_ref[0, rows, :], ln_ref[...], mod_ref[0, 0:1, :],
                      mod_ref[0, 1:2, :]).astype(BF16)
        ua = jnp.dot(h, wa_ref[...], preferred_element_type=F32)
        zz = jnp.dot(ua.astype(BF16), cs_ref[...], preferred_element_type=F32)
        zc_ref[0, rows, :] = zz[:, :FNET_WIDTH]
        zs_ref[0, rows, :] = zz[:, FNET_WIDTH:]
        for cols in _col_chunks(wm_ref.shape[1], 768):
            um_ref[0, rows, cols] = jnp.dot(
                h, wm_ref[:, cols], preferred_element_type=F32).astype(BF16)
        ut_ref[0, rows, :] = jnp.dot(h, wt_ref[...], preferred_element_type=F32)
        for cols in _col_chunks(wg_ref.shape[1], 512):
            g_ref[0, rows, cols] = jax.nn.sigmoid(
                jnp.dot(h, wg_ref[:, cols], preferred_element_type=F32)).astype(BF16)


def _inproj(x, mod, ln, wa, wm, wt, wg, cs):
    b, t, d = x.shape
    tm = _row_tile(t, 1024)
    row = lambda w: pl.BlockSpec((1, tm, w), lambda i, j: (i, j, 0))
    outs = [(FNET_WIDTH, F32), (FNET_WIDTH, F32), (wm.shape[1], BF16), (wt.shape[1], F32),
            (wg.shape[1], BF16)]
    return pl.pallas_call(
        _inproj_kernel,
        grid=(b, t // tm),
        in_specs=[row(d), pl.BlockSpec((1, N_MOD, d), lambda i, j: (i, 0, 0)),
                  _const_spec(ln.shape), _const_spec(wa.shape), _const_spec(wm.shape),
                  _const_spec(wt.shape), _const_spec(wg.shape), _const_spec(cs.shape)],
        out_specs=[row(w) for w, _ in outs],
        out_shape=[jax.ShapeDtypeStruct((b, t, w), dt) for w, dt in outs],
        compiler_params=_params(("parallel", "parallel")),
        name="inproj",
    )(x, mod, ln, wa, wm, wt, wg, cs)


FNET_T2 = 64
FNET_ROWS = 8


def _rows_at(ref, j):
    return ref[0, :, j:j + 1, :].reshape(ref.shape[1], ref.shape[3])


def _fnet_stage1_kernel(d_ref, zc_ref, zs_ref, twc_ref, tws_ref, gr_ref, gi_ref):
    t2, w = zc_ref.shape[1], zc_ref.shape[3]
    for j in range(zc_ref.shape[2]):
        data = jnp.concatenate([_rows_at(zc_ref, j), _rows_at(zs_ref, j)], axis=0)
        g = _dot(d_ref[...], data)
        g_r, g_i = g[:t2], g[t2:]
        c = jnp.concatenate([twc_ref[j]] * (w // LANES), axis=1)
        s = jnp.concatenate([tws_ref[j]] * (w // LANES), axis=1)
        gr_ref[0, j] = g_r * c + g_i * s
        gi_ref[0, j] = g_i * c - g_r * s


def _fnet_stage2_kernel(d_ref, gr_ref, gi_ref, o_ref, *, scale):
    t1, w = gr_ref.shape[1], gr_ref.shape[3]
    for j in range(gr_ref.shape[2]):
        data = jnp.concatenate([_rows_at(gr_ref, j), _rows_at(gi_ref, j)], axis=0)
        o_ref[0, :, j:j + 1, :] = (_dot(d_ref[...], data) * scale).reshape(t1, 1, w)


def _dft_tables(t):
    t2 = FNET_T2
    t1 = t // t2
    assert t1 * t2 == t

    def cos_sin(rows, cols, n):
        k = (lax.iota(jnp.int32, rows)[:, None] * lax.iota(jnp.int32, cols)[None, :]) % n
        ang = k.astype(F32) * (2.0 * math.pi / n)
        return jnp.cos(ang), jnp.sin(ang)

    c2, s2 = cos_sin(t2, t2, t2)
    c1, s1 = cos_sin(t1, t1, t1)
    twc, tws = cos_sin(t1, t2, t)
    lanes = lambda m: jnp.broadcast_to(m[:, :, None], (t1, t2, LANES))
    return {
        "stage1": jnp.block([[c2, -s2], [-s2, -c2]]).astype(BF16),
        "stage2": jnp.concatenate([c1, s1], axis=1).astype(BF16),
        "twc": lanes(twc), "tws": lanes(tws),
    }


def _fnet(zc, zs, tab):
    b, t, w = zc.shape
    t2 = FNET_T2
    t1 = t // t2
    n1 = min(FNET_ROWS, t1)
    in_spec = pl.BlockSpec((1, t2, n1, w), lambda i, j: (i, 0, j, 0))
    tw_spec = pl.BlockSpec((n1, t2, LANES), lambda i, j: (j, 0, 0))
    g_shape = jax.ShapeDtypeStruct((b, t1, t2, w), F32)
    g_r, g_i = pl.pallas_call(
        _fnet_stage1_kernel,
        grid=(b, t1 // n1),
        in_specs=[_const_spec(tab["stage1"].shape), in_spec, in_spec, tw_spec, tw_spec],
        out_specs=[pl.BlockSpec((1, n1, t2, w), lambda i, j: (i, j, 0, 0))] * 2,
        out_shape=[g_shape, g_shape],
        compiler_params=_params(("parallel", "parallel")),
        name="fnet_stage1",
    )(tab["stage1"], zc.reshape(b, t2, t1, w), zs.reshape(b, t2, t1, w), tab["twc"], tab["tws"])
    n2 = min(FNET_ROWS, t2)
    spec = pl.BlockSpec((1, t1, n2, w), lambda i, j: (i, 0, j, 0))
    y = pl.pallas_call(
        functools.partial(_fnet_stage2_kernel, scale=1.0 / math.sqrt(t * HEAD_DIM)),
        grid=(b, t2 // n2),
        in_specs=[_const_spec(tab["stage2"].shape), spec, spec],
        out_specs=spec,
        out_shape=jax.ShapeDtypeStruct((b, t1, t2, w), F32),
        compiler_params=_params(("parallel", "parallel")),
        name="fnet_stage2",
    )(tab["stage2"], g_r, g_i)
    return y.reshape(b, t, w)


def _conv3(main, prev_row, next_row, w):
    tm = main.shape[0]
    ri = lax.broadcasted_iota(jnp.int32, main.shape, 0)
    up = jnp.where(ri == 0, prev_row, pltpu.roll(main, 1, 0))
    dn = jnp.where(ri == tm - 1, next_row, pltpu.roll(main, tm - 1, 0))
    return w[0:1] * up + w[1:2] * main + w[2:3] * dn


def _prep_kernel(um_ref, ump_ref, umn_ref, ut_ref, utp_ref, utn_ref, cwm_ref, cwt_ref,
                 w0_ref, w2_ref, a0_ref, a2_ref, g2_ref, kk_ref, ka_ref, rk_ref, hs_ref, shift_ref,
                 r_ref, k_ref, v_ref, a_ref, b_ref, ef_ref, eb_ref, g_ref, bonus_ref):
    j = pl.program_id(1)
    first = j == 0
    last = j == pl.num_programs(1) - 1
    hp = ump_ref.shape[1]

    def conv_main(c0, c1):
        main = um_ref[0, :, c0:c1]
        tm = main.shape[0]
        shifted = jnp.dot(shift_ref[...], main, preferred_element_type=F32)
        ri = lax.broadcasted_iota(jnp.int32, main.shape, 0)
        prev = jnp.where(first, 0.0, ump_ref[0, hp - 1:hp, c0:c1].astype(F32))
        nxt = jnp.where(last, 0.0, umn_ref[0, 0:1, c0:c1].astype(F32))
        up = jnp.where(ri == 0, prev, shifted[:tm])
        dn = jnp.where(ri == tm - 1, nxt, shifted[tm:])
        w = cwm_ref[:, c0:c1]
        return w[0:1] * up + w[1:2] * main.astype(F32) + w[2:3] * dn

    w = RWKV_WIDTH
    r = conv_main(0, w)
    k = conv_main(w, 2 * w)
    v = conv_main(2 * w, 3 * w)
    tp = utp_ref.shape[1]
    tail = _conv3(ut_ref[0], jnp.where(first, 0.0, utp_ref[0, tp - 1:tp, :]),
                  jnp.where(last, 0.0, utn_ref[0, 0:1, :]), cwt_ref[...])
    xw, xa, xg = tail[:, 0:LANES], tail[:, LANES:2 * LANES], tail[:, 2 * LANES:3 * LANES]

    e = jax.nn.sigmoid(w0_ref[...] + _dot_3pass(jnp.tanh(xw), w2_ref)) * math.exp(-0.5)
    ef_ref[0] = e[:, :w]
    eb_ref[0] = e[:, w:]
    a = jax.nn.sigmoid(a0_ref[...] + _dot_3pass(xa, a2_ref))
    g_ref[0] = _dot_3pass(jax.nn.sigmoid(xg), g2_ref).astype(BF16)

    kk = k * kk_ref[...]
    kk = kk * lax.rsqrt(_head_reduce(kk * kk, hs_ref[...]) + 1e-12)
    k = k * (1.0 + (a - 1.0) * ka_ref[...])
    r_ref[0] = r.astype(BF16)
    k_ref[0] = k.astype(BF16)
    v_ref[0] = v.astype(BF16)
    a_ref[0] = (-kk).astype(BF16)
    b_ref[0] = (kk * a).astype(BF16)
    bonus_ref[0] = (_head_reduce(r * k * rk_ref[...], hs_ref[...]) * v).astype(BF16)


def _prep(um, ut, p):
    b, t, wm = um.shape
    wt = ut.shape[2]
    tm = _row_tile(t, 256)
    hm, ht = 16, 8
    nm, nt = tm // hm, tm // ht
    main = lambda w: pl.BlockSpec((1, tm, w), lambda i, j: (i, j, 0))
    prev = lambda h, n, w: pl.BlockSpec((1, h, w), lambda i, j: (i, jnp.maximum(j * n - 1, 0), 0))
    nxt = lambda h, n, w: pl.BlockSpec(
        (1, h, w), lambda i, j: (i, jnp.minimum((j + 1) * n, t // h - 1), 0))
    consts = [p["conv_main"], p["conv_tail"], p["decay_w0"], p["decay_w2"], p["iclr_a0"],
              p["iclr_a2"], p["gate_g2"], p["k_k"], p["k_a"], p["r_k"], p["head_sum"],
              jnp.concatenate([jnp.eye(tm, k=-1, dtype=BF16), jnp.eye(tm, k=1, dtype=BF16)], axis=0)]
    outs = [BF16] * 5 + [F32, F32, BF16, BF16]
    return pl.pallas_call(
        _prep_kernel,
        grid=(b, t // tm),
        in_specs=[main(wm), prev(hm, nm, wm), nxt(hm, nm, wm),
                  main(wt), prev(ht, nt, wt), nxt(ht, nt, wt)]
                 + [_const_spec(c.shape) for c in consts],
        out_specs=[main(RWKV_WIDTH) for _ in outs],
        out_shape=[jax.ShapeDtypeStruct((b, t, RWKV_WIDTH), dt) for dt in outs],
        compiler_params=_params(("parallel", "parallel")),
        name="rwkv_prep",
    )(um, um, um, ut, ut, ut, *consts)


def _wkv_kernel(r_ref, k_ref, v_ref, a_ref, b_ref, e_ref, *rest, reverse):
    *add_refs, o_ref, s_ref = rest
    add_ref = add_refs[0] if add_refs else None
    c = r_ref.shape[1]
    pairs = range(s_ref.shape[0])
    heads = range(LANES // HEAD_DIM)
    chains = [(p, h) for p in pairs for h in heads]

    @pl.when(pl.program_id(1) == 0)
    def _():
        s_ref[...] = jnp.zeros_like(s_ref)

    row = lax.broadcasted_iota(jnp.int32, (c, c), 0)
    col = lax.broadcasted_iota(jnp.int32, (c, c), 1)
    if reverse:
        incl, strict = col >= row, col > row
    else:
        incl, strict = col <= row, col < row
    tri = jnp.where(incl, 1.0, 0.0).astype(BF16)
    end = 0 if reverse else c - 1
    lane = lax.broadcasted_iota(jnp.int32, (1, LANES), 1)
    in_head = [(lane >= h * HEAD_DIM) & (lane < (h + 1) * HEAD_DIM) for h in heads]
    same_head = (lax.broadcasted_iota(jnp.int32, (LANES, LANES), 0) // HEAD_DIM
                 == lax.broadcasted_iota(jnp.int32, (LANES, LANES), 1) // HEAD_DIM)

    per_row = r_ref.shape[2] // LANES

    def tile(ref, p):
        return ref[p // per_row, :, (p % per_row) * LANES:(p % per_row + 1) * LANES]

    tri2 = jnp.concatenate([tri, tri], axis=1)

    def running_sum(e):
        return jnp.dot(tri2, jnp.concatenate(_split_bf16(e, 2), axis=0), preferred_element_type=F32)

    cum = [running_sum(tile(e_ref, p)) for p in pairs]
    total = [x[end:end + 1, :] for x in cum]
    ref = [x[c // 2:c // 2 + 1, :] for x in cum]
    dl = [cum[p] - ref[p] for p in pairs]
    g_inv = [jnp.exp(dl[p]) for p in pairs]
    at = [tile(a_ref, p).astype(F32) * jnp.exp(tile(e_ref, p) - dl[p]) for p in pairs]
    rt = [tile(r_ref, p).astype(F32) * jnp.exp(-dl[p]) for p in pairs]
    bt = [tile(b_ref, p).astype(F32) * g_inv[p] for p in pairs]
    kt = [tile(k_ref, p).astype(F32) * g_inv[p] for p in pairs]
    g_end = [jnp.exp(ref[p] - total[p]) for p in pairs]
    bh = [(bt[p] * g_end[p]).astype(BF16) for p in pairs]
    kh = [(kt[p] * g_end[p]).astype(BF16) for p in pairs]
    bk = [jnp.concatenate([bt[p], kt[p]], axis=0).astype(BF16) for p in pairs]
    ar = [jnp.concatenate([at[p], rt[p]], axis=0) for p in pairs]
    vb = [tile(v_ref, p) for p in pairs]

    prod = [_dot_nt(jnp.where(in_head[h], ar[p], 0.0), bk[p]) for p, h in chains]
    l_ab = [jnp.where(strict, x[:c, :c], 0.0) for x in prod]
    l_ak = [jnp.where(strict, x[:c, c:], 0.0) for x in prod]
    m_rb = [jnp.where(incl, x[c:, :c], 0.0).astype(BF16) for x in prod]
    m_rk = [jnp.where(incl, x[c:, c:], 0.0).astype(BF16) for x in prod]
    del prod
    eye = jnp.where(row == col, 1.0, 0.0)
    t = [eye + jnp.where((row >> 1) == (col >> 1), x, 0.0) for x in l_ab]
    nch = range(len(chains))
    for sh in range(1, (c - 1).bit_length()):
        m = 1 << sh
        joins = ((row >> (sh + 1)) == (col >> (sh + 1))) & ((row >> sh) != (col >> sh))
        e_k = [jnp.where(joins, l_ab[i], 0.0) for i in nch]
        if m < 8:
            et = [_dot(e_k[i], t[i]) for i in nch]
            t = [t[i] + _dot(t[i], et[i]) for i in nch]
            continue
        live = [slice(b0 + (0 if reverse else m), b0 + (m if reverse else 2 * m))
                for b0 in range(0, c, 2 * m)]

        def take(v):
            return jnp.concatenate([v[rs] for rs in live], axis=0)

        def spread(v):
            zero = jnp.zeros((m, v.shape[1]), v.dtype)
            parts = []
            for j in range(len(live)):
                blk = v[j * m:(j + 1) * m]
                parts += [blk, zero] if reverse else [zero, blk]
            return jnp.concatenate(parts, axis=0)

        et = [_dot(take(e_k[i]), t[i]) for i in nch]
        upd = [_dot(take(t[i]), spread(et[i])) for i in nch]
        t = [t[i] + spread(upd[i]) for i in nch]
    lakv = [jnp.dot(l_ak[i].astype(BF16), vb[p], preferred_element_type=F32)
            for i, (p, h) in enumerate(chains)]
    x = [_dot(t[i], jnp.concatenate([at[p], lakv[i]], axis=1)) for i, (p, h) in enumerate(chains)]
    x_a = [v[:, :LANES] for v in x]
    x_u = [v[:, LANES:] for v in x]
    y_a = [jnp.dot(m_rb[i], x_a[i].astype(BF16), preferred_element_type=F32) for i in nch]
    y_o = [jnp.dot(jnp.concatenate([m_rb[i], m_rk[i]], axis=1),
                   jnp.concatenate([x_u[i].astype(BF16), vb[p]], axis=0),
                   preferred_element_type=F32) for i, (p, h) in enumerate(chains)]

    def pick(vals, p):
        out = vals[p * len(heads)]
        for h in heads[1:]:
            out = jnp.where(in_head[h], vals[p * len(heads) + h], out)
        return out

    a_solved = [pick(x_a, p) for p in pairs]
    u = [pick(x_u, p) for p in pairs]
    r_hat = [rt[p] + pick(y_a, p) for p in pairs]
    o_intra = [pick(y_o, p) for p in pairs]
    s = [s_ref[p] for p in pairs]
    s_hat = [(s[p] * jnp.exp(-ref[p])).astype(BF16) for p in pairs]
    out = [_dot_nt(r_hat[p], s_hat[p]) + o_intra[p] for p in pairs]
    g_t = [jnp.where(same_head, _dot_tn(a_solved[p], bh[p]), 0.0) for p in pairs]
    h_t = [jnp.where(same_head,
                     _dot_tn(jnp.concatenate([u[p].astype(BF16), vb[p]], axis=0),
                             jnp.concatenate([bh[p], kh[p]], axis=0)), 0.0) for p in pairs]
    s_new = [s[p] * jnp.exp(-total[p]) + _dot(s_hat[p], g_t[p]) + h_t[p] for p in pairs]
    for p in pairs:
        if add_ref is not None:
            out[p] = out[p] + tile(add_ref, p).astype(F32)
        o_ref[p // per_row, :, (p % per_row) * LANES:(p % per_row + 1) * LANES] = (
            out[p].astype(o_ref.dtype))
        s_ref[p] = s_new[p]


def _wkv(r, k, v, a, b, e, reverse, add=None):
    bsz, t, w = r.shape
    c = min(WKV_CHUNK, t)
    nc = t // c
    rows = math.gcd(bsz, 2)
    if reverse:
        idx = lambda i, j: (i, nc - 1 - j, 0)
    else:
        idx = lambda i, j: (i, j, 0)
    spec = pl.BlockSpec((rows, c, w), idx)
    return pl.pallas_call(
        functools.partial(_wkv_kernel, reverse=reverse),
        grid=(bsz // rows, nc),
        in_specs=[spec] * (6 if add is None else 7),
        out_specs=spec,
        out_shape=jax.ShapeDtypeStruct((bsz, t, w), BF16),
        scratch_shapes=[pltpu.VMEM((rows * (w // LANES), LANES, LANES), F32)],
        compiler_params=_params(("parallel", "arbitrary")),
        name="wkv_bwd" if reverse else "wkv_fwd",
    )(r, k, v, a, b, e, *(() if add is None else (add,)))


def _mixout_kernel(o_ref, bonus_ref, g_ref, yf_ref, gates_ref, x_ref, mod_ref,
                   lnw_ref, lnb_ref, hm_ref, wfo_ref, wro_ref, wo_ref, out_ref):
    d = x_ref.shape[2]
    for rows in _sub_blocks(x_ref.shape[1]):
        o = o_ref[0, rows, :].astype(F32)
        dev = o - _head_reduce(o, hm_ref[...])
        var = _head_reduce(dev * dev, hm_ref[...])
        o = dev * lax.rsqrt(var + GN_EPS) * lnw_ref[...] + lnb_ref[...]
        y = (o + bonus_ref[0, rows, :].astype(F32)) * g_ref[0, rows, :].astype(F32)
        y_b = _dot(y, wro_ref[...])
        y_a = _dot(yf_ref[0, rows, :], wfo_ref[...])
        merged = (gates_ref[0, rows, :d].astype(F32) * y_a
                  + gates_ref[0, rows, d:].astype(F32) * y_b)
        out_ref[0, rows, :] = x_ref[0, rows, :] + mod_ref[0, 2:3, :] * _dot(merged, wo_ref[...])


def _mixout(o, bonus, g, yf, gates, x, mod, p):
    b, t, d = x.shape
    tm = _row_tile(t, 1024)
    row = lambda a: pl.BlockSpec((1, tm, a.shape[2]), lambda i, j: (i, j, 0))
    consts = [p["lnx_w"], p["lnx_b"], p["head_mean"], p["w_fnet_out"], p["w_rwkv_out"], p["w_o"]]
    acts = [o, bonus, g, yf, gates, x]
    return pl.pallas_call(
        _mixout_kernel,
        grid=(b, t // tm),
        in_specs=[row(a) for a in acts] + [pl.BlockSpec((1, N_MOD, d), lambda i, j: (i, 0, 0))]
                 + [_const_spec(c.shape) for c in consts],
        out_specs=pl.BlockSpec((1, tm, d), lambda i, j: (i, j, 0)),
        out_shape=jax.ShapeDtypeStruct((b, t, d), F32),
        compiler_params=_params(("parallel", "parallel")),
        name="mixout",
    )(*acts, mod, *consts)


def _ffn_kernel(x_ref, mod_ref, ln_ref, lnf_ref, wg_ref, wu_ref, wd_ref, out_ref, *, final_norm):
    for rows in _sub_blocks(x_ref.shape[1]):
        x = x_ref[0, rows, :]
        h = _norm_mod(x, ln_ref[...], mod_ref[0, 3:4, :], mod_ref[0, 4:5, :]).astype(BF16)
        acc = None
        for cols in _col_chunks(wg_ref.shape[1], 768):
            gate = jnp.dot(h, wg_ref[:, cols], preferred_element_type=F32)
            up = jnp.dot(h, wu_ref[:, cols], preferred_element_type=F32)
            part = _dot(_silu(gate) * up, wd_ref[cols, :])
            acc = part if acc is None else acc + part
        out = x + mod_ref[0, 5:6, :] * acc
        if final_norm:
            out = _norm_mod(out, lnf_ref[...], 0.0, 0.0)
        out_ref[0, rows, :] = out


def _ffn_dense(x, mod, ln, ln_final, wg, wu, wd, final_norm):
    b, t, d = x.shape
    tm = _row_tile(t, 1024)
    consts = [ln, ln_final, wg, wu, wd]
    return pl.pallas_call(
        functools.partial(_ffn_kernel, final_norm=final_norm),
        grid=(b, t // tm),
        in_specs=[pl.BlockSpec((1, tm, d), lambda i, j: (i, j, 0)),
                  pl.BlockSpec((1, N_MOD, d), lambda i, j: (i, 0, 0))]
                 + [_const_spec(c.shape) for c in consts],
        out_specs=pl.BlockSpec((1, tm, d), lambda i, j: (i, j, 0)),
        out_shape=jax.ShapeDtypeStruct((b, t, d), F32),
        compiler_params=_params(("parallel", "parallel")),
        name="ffn_dense",
    )(x, mod, *consts)


def _route_kernel(x_ref, mod_ref, ln_ref, rw_ref, h_ref, route_ref, cnt_ref):
    h = _norm_mod(x_ref[0], ln_ref[...], mod_ref[0, 3:4, :], mod_ref[0, 4:5, :])
    h_ref[0] = h
    tm = h.shape[0]
    logits = lax.dot_general(rw_ref[...], h, (((1,), (1,)), ((), ())), precision=HIGHEST,
                             preferred_element_type=F32)
    ne = logits.shape[0]
    ei = lax.broadcasted_iota(jnp.int32, logits.shape, 0)
    m1 = jnp.max(logits, axis=0, keepdims=True)
    i1 = jnp.min(jnp.where(logits == m1, ei, ne), axis=0, keepdims=True)
    rest = jnp.where(ei == i1, -jnp.inf, logits)
    m2 = jnp.max(rest, axis=0, keepdims=True)
    i2 = jnp.min(jnp.where(rest == m2, ei, ne), axis=0, keepdims=True)
    t2 = jnp.exp(m2 - m1)
    w1 = 1.0 / (1.0 + t2)
    w2 = t2 / (1.0 + t2)
    oh1 = jnp.where(ei == i1, 1.0, 0.0)
    oh2 = jnp.where(ei == i2, 1.0, 0.0)
    before = (lax.broadcasted_iota(jnp.int32, (tm, tm), 0)
              < lax.broadcasted_iota(jnp.int32, (tm, tm), 1))
    before = jnp.where(before, 1.0, 0.0).astype(BF16)
    cs1 = jnp.dot(oh1.astype(BF16), before, preferred_element_type=F32)
    cs2 = jnp.dot(oh2.astype(BF16), before, preferred_element_type=F32)
    n1 = jnp.sum(oh1, axis=1, keepdims=True)
    n2 = jnp.sum(oh2, axis=1, keepdims=True)
    rank1 = jnp.sum(oh1 * cs1, axis=0, keepdims=True)
    rank2 = jnp.sum(oh2 * (cs2 + n1), axis=0, keepdims=True)
    zero = jnp.zeros_like(w1)
    route_ref[0, 0] = jnp.concatenate(
        [i1.astype(F32), i2.astype(F32), w1, w2, rank1, rank2, zero, zero], axis=0)
    cnt_ref[0, 0] = jnp.broadcast_to(n1 + n2, (ne, LANES))


def _route(x, mod, ln, router_t):
    b, t, d = x.shape
    tm = _row_tile(t, 512)
    nt = t // tm
    ne = router_t.shape[0]
    return pl.pallas_call(
        _route_kernel,
        grid=(b, nt),
        in_specs=[pl.BlockSpec((1, tm, d), lambda i, j: (i, j, 0)),
                  pl.BlockSpec((1, N_MOD, d), lambda i, j: (i, 0, 0)),
                  _const_spec(ln.shape), _const_spec(router_t.shape)],
        out_specs=[pl.BlockSpec((1, tm, d), lambda i, j: (i, j, 0)),
                   pl.BlockSpec((1, 1, 8, tm), lambda i, j: (i, j, 0, 0)),
                   pl.BlockSpec((1, 1, ne, LANES), lambda i, j: (i, j, 0, 0))],
        out_shape=[jax.ShapeDtypeStruct((b, t, d), F32),
                   jax.ShapeDtypeStruct((b, nt, 8, tm), F32),
                   jax.ShapeDtypeStruct((b, nt, ne, LANES), F32)],
        compiler_params=_params(("parallel", "parallel")),
        name="moe_route",
    )(x, mod, ln, router_t)


def _row_copy(src, dst, sem):
    return pltpu.make_async_copy(src, dst, sem)


def _dispatch_kernel(dest_ref, h_ref, rows_in_ref, rows_ref, sem):
    del rows_in_ref
    tm = h_ref.shape[0]

    def start(i, _):
        for s in range(2):
            _row_copy(h_ref.at[pl.ds(i, 1)], rows_ref.at[pl.ds(dest_ref[0, 0, 2 * i + s], 1)],
                      sem.at[s]).start()
        return 0

    lax.fori_loop(0, tm, start, 0, unroll=8)
    for s in range(2):
        _row_copy(h_ref, rows_ref.at[pl.ds(0, tm)], sem.at[s]).wait()


def _dispatch(h, dest, n_rows):
    n, d = h.shape
    tm = dest.shape[2] // 2
    rows = jnp.zeros((n_rows, d), h.dtype)
    return pl.pallas_call(
        _dispatch_kernel,
        grid=(n // tm,),
        in_specs=[pl.BlockSpec((1, 1, 2 * tm), lambda i: (i, 0, 0), memory_space=pltpu.SMEM),
                  pl.BlockSpec((tm, d), lambda i: (i, 0)),
                  pl.BlockSpec(memory_space=pl.ANY)],
        out_specs=pl.BlockSpec(memory_space=pl.ANY),
        out_shape=jax.ShapeDtypeStruct((n_rows, d), h.dtype),
        scratch_shapes=[pltpu.SemaphoreType.DMA((2,))],
        input_output_aliases={2: 0},
        compiler_params=pltpu.CompilerParams(dimension_semantics=("arbitrary",),
                                             vmem_limit_bytes=VMEM_LIMIT, has_side_effects=True),
        name="moe_dispatch",
    )(dest, h, rows)


def _experts_kernel(be_ref, nb_ref, x_ref, wg_ref, wu_ref, wd_ref, y_ref):
    del be_ref

    @pl.when(pl.program_id(0) < nb_ref[0])
    def _():
        x = x_ref[...].astype(BF16)
        f = wg_ref.shape[2]
        step = f // 7
        acc = None
        for c0 in range(0, f, step):
            gate = jnp.dot(x, wg_ref[0, :, c0:c0 + step], preferred_element_type=F32)
            up = jnp.dot(x, wu_ref[0, :, c0:c0 + step], preferred_element_type=F32)
            part = _dot(_silu(gate) * up, wd_ref[0, c0:c0 + step, :])
            acc = part if acc is None else acc + part
        y_ref[...] = acc

    @pl.when(pl.program_id(0) >= nb_ref[0])
    def _():
        y_ref[...] = jnp.zeros_like(y_ref)


def _experts(rows, block_e, n_used, wg, wu, wd):
    n_rows, d = rows.shape
    rb = MOE_ROW_BLOCK
    f = wg.shape[2]
    wspec = lambda shape: pl.BlockSpec((1,) + shape, lambda i, be, nb: (be[i], 0, 0),
                                       pipeline_mode=pl.Buffered(1))
    return pl.pallas_call(
        _experts_kernel,
        grid_spec=pltpu.PrefetchScalarGridSpec(
            num_scalar_prefetch=2,
            grid=(n_rows // rb,),
            in_specs=[pl.BlockSpec((rb, d), lambda i, be, nb: (i, 0)),
                      wspec((d, f)), wspec((d, f)), wspec((f, d))],
            out_specs=pl.BlockSpec((rb, d), lambda i, be, nb: (i, 0))),
        out_shape=jax.ShapeDtypeStruct((n_rows, d), F32),
        compiler_params=_params(("arbitrary",)),
        name="moe_experts",
    )(block_e, n_used, rows, wg, wu, wd)


def _combine_kernel(dest_ref, y_hbm, x_ref, wts_ref, mod_ref, lnf_ref, out_ref, buf, sem,
                    *, final_norm):
    tm = x_ref.shape[1]

    def start(i, _):
        for s in range(2):
            _row_copy(y_hbm.at[pl.ds(dest_ref[0, 0, 2 * i + s], 1)], buf.at[s, pl.ds(i, 1)],
                      sem.at[s]).start()
        return 0

    lax.fori_loop(0, tm, start, 0, unroll=8)
    for s in range(2):
        _row_copy(y_hbm.at[pl.ds(0, tm)], buf.at[s], sem.at[s]).wait()
    f = wts_ref[0, :, 0:1] * buf[0] + wts_ref[0, :, 1:2] * buf[1]
    out = x_ref[0] + mod_ref[0, 5:6, :] * f
    if final_norm:
        out = _norm_mod(out, lnf_ref[...], 0.0, 0.0)
    out_ref[0] = out


def _combine(y_rows, dest, wts, x, mod, ln_final, final_norm):
    b, t, d = x.shape
    tm = dest.shape[2] // 2
    nt = t // tm
    return pl.pallas_call(
        functools.partial(_combine_kernel, final_norm=final_norm),
        grid=(b, nt),
        in_specs=[pl.BlockSpec((1, 1, 2 * tm), lambda i, j: (i * nt + j, 0, 0),
                               memory_space=pltpu.SMEM),
                  pl.BlockSpec(memory_space=pl.ANY),
                  pl.BlockSpec((1, tm, d), lambda i, j: (i, j, 0)),
                  pl.BlockSpec((1, tm, 2), lambda i, j: (i, j, 0)),
                  pl.BlockSpec((1, N_MOD, d), lambda i, j: (i, 0, 0)),
                  _const_spec(ln_final.shape)],
        out_specs=pl.BlockSpec((1, tm, d), lambda i, j: (i, j, 0)),
        out_shape=jax.ShapeDtypeStruct((b, t, d), F32),
        scratch_shapes=[pltpu.VMEM((2, tm, d), F32), pltpu.SemaphoreType.DMA((2,))],
        compiler_params=_params(("arbitrary", "arbitrary")),
        name="moe_combine",
    )(dest, y_rows, x, wts, mod, ln_final)


def _ffn_moe(x, mod, ln, ln_final, router_t, wg, wu, wd, final_norm):
    b, t, d = x.shape
    n = b * t
    ne = router_t.shape[0]
    rb = MOE_ROW_BLOCK
    h, route, cnt = _route(x, mod, ln, router_t)
    nt, tm = route.shape[1], route.shape[3]
    tile_cnt = cnt[..., 0].reshape(b * nt, ne).astype(jnp.int32)
    totals = jnp.sum(tile_cnt, axis=0)
    padded = ((totals + rb - 1) // rb) * rb
    padded_end = jnp.cumsum(padded)
    base = (padded_end - padded)[None, :] + jnp.cumsum(tile_cnt, axis=0) - tile_cnt
    route = route.reshape(b * nt, 8, tm)
    experts = route[:, 0:2, :].astype(jnp.int32)
    rank = route[:, 4:6, :].astype(jnp.int32)
    onehot = experts[..., None] == jnp.arange(ne, dtype=jnp.int32)
    dest = jnp.sum(jnp.where(onehot, base[:, None, None, :], 0), axis=-1) + rank
    dest = jnp.swapaxes(dest, 1, 2).reshape(b * nt, 1, 2 * tm)
    wts = jnp.swapaxes(route[:, 2:4, :], 1, 2).reshape(b, t, 2)
    n_blocks = -(-(n * 2) // rb) + ne
    block_e = jnp.minimum(
        jnp.searchsorted(padded_end, jnp.arange(n_blocks, dtype=jnp.int32) * rb, side="right"),
        ne - 1).astype(jnp.int32)
    n_used = (padded_end[-1:] // rb).astype(jnp.int32)
    rows = _dispatch(h.reshape(n, d), dest, n_blocks * rb)
    y_rows = _experts(rows, block_e, n_used, wg, wu, wd)
    return _combine(y_rows, dest, wts, x, mod, ln_final, final_norm)


def _block_diag(m, n):
    return jnp.kron(jnp.eye(n, dtype=m.dtype), m)


def _layer_params(i, w_in, conv_w, decay_w0, decay_w2, iclr_a0, iclr_a2, gate_g2, k_k, k_a, r_k,
                  lnx_w, lnx_b, w_fnet_out, w_rwkv_out, w_o):
    w = RWKV_WIDTH
    lora = decay_w2.shape[2]
    tail0 = FNET_WIDTH + 3 * w

    def reorder_tail(m):
        pad = jnp.zeros(m.shape[:-1] + (LANES - lora,), m.dtype)
        return jnp.concatenate([m[..., :3 * lora], pad, m[..., 3 * lora:]], axis=-1)

    row = lambda v: v.reshape(1, -1)
    hi_lo = lambda m: jnp.stack(_split_bf16(m, 2))
    wi = w_in[i]
    rwkv_cols = conv_w.shape[2]
    zeros = jnp.zeros((lora, w), F32)
    return {
        "w_a": wi[:, :FNET_WIDTH].astype(BF16),
        "w_main": wi[:, FNET_WIDTH:tail0].astype(BF16),
        "w_tail": reorder_tail(wi[:, tail0:FNET_WIDTH + rwkv_cols]).astype(BF16),
        "w_gates": wi[:, FNET_WIDTH + rwkv_cols:].astype(BF16),
        "conv_main": conv_w[i][:, :3 * w],
        "conv_tail": reorder_tail(conv_w[i][:, 3 * w:]),
        "decay_w0": decay_w0[i].reshape(1, 2 * w),
        "decay_w2": hi_lo(jnp.concatenate(
            [jnp.concatenate([decay_w2[i, 0], zeros], axis=1),
             jnp.concatenate([zeros, decay_w2[i, 1]], axis=1)], axis=0)),
        "iclr_a0": row(iclr_a0[i]),
        "iclr_a2": hi_lo(jnp.concatenate([iclr_a2[i], jnp.zeros((LANES - lora, w), F32)], axis=0)),
        "gate_g2": hi_lo(gate_g2[i]),
        "k_k": row(k_k[i]), "k_a": row(k_a[i]), "r_k": row(r_k[i]),
        "lnx_w": row(lnx_w[i]), "lnx_b": row(lnx_b[i]),
        "head_sum": _block_diag(jnp.ones((HEAD_DIM, HEAD_DIM), BF16), LANES // HEAD_DIM),
        "head_mean": _block_diag(jnp.full((HEAD_DIM, HEAD_DIM), 1.0 / HEAD_DIM, BF16),
                                 LANES // HEAD_DIM),
        "w_fnet_out": w_fnet_out[i].astype(BF16),
        "w_rwkv_out": w_rwkv_out[i].astype(BF16),
        "w_o": w_o[i].astype(BF16),
    }


def _channel_dft():
    k = jnp.arange(HEAD_DIM, dtype=jnp.int32)
    ang = ((k[:, None] * k[None, :]) % HEAD_DIM).astype(F32) * (2.0 * math.pi / HEAD_DIM)
    groups = FNET_WIDTH // HEAD_DIM
    return jnp.concatenate([_block_diag(jnp.cos(ang), groups), _block_diag(jnp.sin(ang), groups)],
                           axis=1).astype(BF16)


def _token_mixer(x, mod, ln, p, cs, dft_tab):
    zc, zs, um, ut, gates = _inproj(x, mod, ln, p["w_a"], p["w_main"], p["w_tail"], p["w_gates"], cs)
    yf = _fnet(zc, zs, dft_tab)
    r, k, v, a, b, e_f, e_b, g, bonus = _prep(um, ut, p)
    o = _wkv(r, k, v, a, b, e_b, True, add=_wkv(r, k, v, a, b, e_f, False))
    return _mixout(o, bonus, g, yf, gates, x, mod, p)


def kernel(x_prompt, x_sample, c_prompt, c_sample, w_ada, b_ada, ln_mix, w_in, conv_w, decay_w0, decay_w2, iclr_a0, iclr_a2, gate_g2, k_k, k_a, r_k, lnx_w, lnx_b, w_fnet_out, w_rwkv_out, w_o, ln_ffn, ff_w_gate, ff_w_up, ff_w_down, router_w, moe_w_gate, moe_w_up, moe_w_down, ln_final):
    depth, d = ln_mix.shape
    streams = [x_prompt, x_sample]
    nb = [x.shape[0] for x in streams]
    mod_all = _ada(jnp.concatenate([c_prompt, c_sample], axis=0), w_ada, b_ada)
    cs = _channel_dft()
    dfts = {}
    for x in streams:
        if x.shape[1] not in dfts:
            dfts[x.shape[1]] = _dft_tables(x.shape[1])
    lnf = ln_final.reshape(1, d)
    for i in range(depth):
        p = _layer_params(i, w_in, conv_w, decay_w0, decay_w2, iclr_a0, iclr_a2, gate_g2, k_k, k_a,
                          r_k, lnx_w, lnx_b, w_fnet_out, w_rwkv_out, w_o)
        j = i // 2
        final = i == depth - 1
        if i % 2 == 0:
            ffw = (ff_w_gate[j].astype(BF16), ff_w_up[j].astype(BF16), ff_w_down[j].astype(BF16))
        else:
            ffw = (moe_w_gate[j].astype(BF16), moe_w_up[j].astype(BF16), moe_w_down[j].astype(BF16))
            router_t = router_w[j].T
        off = 0
        for s, x in enumerate(streams):
            mod = mod_all[i, off:off + nb[s]].reshape(nb[s], N_MOD, d)
            off += nb[s]
            x = _token_mixer(x, mod, ln_mix[i].reshape(1, d), p, cs, dfts[x.shape[1]])
            ln2 = ln_ffn[i].reshape(1, d)
            if i % 2 == 0:
                x = _ffn_dense(x, mod, ln2, lnf, *ffw, final)
            else:
                x = _ffn_moe(x, mod, ln2, lnf, router_t, *ffw, final)
            streams[s] = x
    return tuple(streams)
```

```python
import functools
import math

import jax
import jax.numpy as jnp
from jax import lax
from jax.experimental import pallas as pl
from jax.experimental.pallas import tpu as pltpu

F32 = jnp.float32
BF16 = jnp.bfloat16
HIGHEST = lax.Precision.HIGHEST

HEAD_DIM = 64
LANES = 128
FNET_WIDTH = 256
RWKV_WIDTH = 768
LORA_TAIL = 384
N_MOD = 6
N_EXPERTS = 8
RMS_EPS = 1e-6
GN_EPS = 64e-5
WKV_CHUNK = 128
MOE_ROW_BLOCK = 512
GROUP_ALIGN = 8
GROUP_CHUNK = 64
VMEM_LIMIT = 56 * 1024 * 1024


def _params(sem, vmem=VMEM_LIMIT):
    return pltpu.CompilerParams(dimension_semantics=sem, vmem_limit_bytes=vmem)


def _const_spec(shape):
    nd = len(shape)
    return pl.BlockSpec(shape, lambda *_: (0,) * nd, pipeline_mode=pl.Buffered(1))


def _dot(a, b):
    return jnp.dot(a.astype(BF16), b.astype(BF16), preferred_element_type=F32)


def _dot_nt(a, b):
    return lax.dot_general(a.astype(BF16), b.astype(BF16), (((1,), (1,)), ((), ())),
                           preferred_element_type=F32)


def _dot_tn(a, b):
    return lax.dot_general(a.astype(BF16), b.astype(BF16), (((0,), (0,)), ((), ())),
                           preferred_element_type=F32)


def _split_bf16(x, parts):
    out, rem = [], x
    for _ in range(parts):
        p = rem.astype(BF16)
        out.append(p)
        rem = rem - p.astype(F32)
    return out


def _dot_exact_rhs(x, m, parts):
    acc = None
    for p in _split_bf16(x, parts):
        t = jnp.dot(p, m, preferred_element_type=F32)
        acc = t if acc is None else acc + t
    return acc


def _dot_f32(a, b):
    return jnp.dot(a, b, precision=HIGHEST, preferred_element_type=F32)


def _dot_3pass(a, w_ref):
    ah, al = _split_bf16(a, 2)
    d = lambda x, y: jnp.dot(x, y, preferred_element_type=F32)
    return d(ah, w_ref[0]) + (d(al, w_ref[0]) + d(ah, w_ref[1]))


def _norm_mod(x, gain, shift, scale):
    ms = jnp.mean(x * x, axis=-1, keepdims=True)
    return x * lax.rsqrt(ms + RMS_EPS) * gain * (1.0 + scale) + shift


def _silu(x):
    return x * jax.nn.sigmoid(x)


def _head_reduce(x, m):
    cols = [_dot_exact_rhs(x[:, j:j + LANES], m, 2) for j in range(0, x.shape[1], LANES)]
    return jnp.concatenate(cols, axis=1)


def _row_tile(t, want):
    tm = min(t, want)
    assert t % tm == 0
    return tm


def _sub_blocks(rows, size=512):
    size = min(size, rows)
    assert rows % size == 0
    return [slice(r, r + size) for r in range(0, rows, size)]


def _col_chunks(width, size):
    return [slice(c, min(c + size, width)) for c in range(0, width, size)]


def _ada_kernel(c_ref, w_ref, b_ref, o_ref):
    o_ref[0] = _dot_f32(_silu(c_ref[...]), w_ref[0]) + b_ref[0]


def _ada(c_all, w_ada, b_ada):
    depth, d, n = w_ada.shape
    bc = c_all.shape[0]
    tn = n // 4
    return pl.pallas_call(
        _ada_kernel,
        grid=(depth, n // tn),
        in_specs=[pl.BlockSpec((bc, d), lambda l, j: (0, 0)),
                  pl.BlockSpec((1, d, tn), lambda l, j: (l, 0, j)),
                  pl.BlockSpec((1, 1, tn), lambda l, j: (l, 0, j))],
        out_specs=pl.BlockSpec((1, bc, tn), lambda l, j: (l, 0, j)),
        out_shape=jax.ShapeDtypeStruct((depth, bc, n), F32),
        compiler_params=_params(("parallel", "parallel")),
        name="ada",
    )(c_all, w_ada, b_ada.reshape(depth, 1, n))


def _inproj_kernel(x_ref, mod_ref, ln_ref, wa_ref, wm_ref, wt_ref, wg_ref, cs_ref,
                   zc_ref, zs_ref, um_ref, ut_ref, g_ref):
    for rows in _sub_blocks(x_ref.shape[1]):
        h = _norm_mod(x_ref[0, rows, :], ln_ref[...], mod_ref[0, 0:1, :],
                      mod_ref[0, 1:2, :]).astype(BF16)
        ua = jnp.dot(h, wa_ref[...], preferred_element_type=F32)
        zz = jnp.dot(ua.astype(BF16), cs_ref[...], preferred_element_type=F32)
        zc_ref[0, rows, :] = zz[:, :FNET_WIDTH]
        zs_ref[0, rows, :] = zz[:, FNET_WIDTH:]
        for cols in _col_chunks(wm_ref.shape[1], 768):
            um_ref[0, rows, cols] = jnp.dot(
                h, wm_ref[:, cols], preferred_element_type=F32).astype(BF16)
        ut_ref[0, rows, :] = jnp.dot(h, wt_ref[...], preferred_element_type=F32)
        for cols in _col_chunks(wg_ref.shape[1], 512):
            g_ref[0, rows, cols] = jax.nn.sigmoid(
                jnp.dot(h, wg_ref[:, cols], preferred_element_type=F32)).astype(BF16)


def _inproj(x, mod, ln, wa, wm, wt, wg, cs):
    b, t, d = x.shape
    tm = _row_tile(t, 1024)
    row = lambda w: pl.BlockSpec((1, tm, w), lambda i, j: (i, j, 0))
    outs = [(FNET_WIDTH, F32), (FNET_WIDTH, F32), (wm.shape[1], BF16), (wt.shape[1], F32),
            (wg.shape[1], BF16)]
    return pl.pallas_call(
        _inproj_kernel,
        grid=(b, t // tm),
        in_specs=[row(d), pl.BlockSpec((1, N_MOD, d), lambda i, j: (i, 0, 0)),
                  _const_spec(ln.shape), _const_spec(wa.shape), _const_spec(wm.shape),
                  _const_spec(wt.shape), _const_spec(wg.shape), _const_spec(cs.shape)],
        out_specs=[row(w) for w, _ in outs],
        out_shape=[jax.ShapeDtypeStruct((b, t, w), dt) for w, dt in outs],
        compiler_params=_params(("parallel", "parallel")),
        name="inproj",
    )(x, mod, ln, wa, wm, wt, wg, cs)


FNET_T2 = 64
FNET_ROWS = 8


def _rows_at(ref, j):
    return ref[0, :, j:j + 1, :].reshape(ref.shape[1], ref.shape[3])


def _fnet_stage1_kernel(d_ref, zc_ref, zs_ref, twc_ref, tws_ref, gr_ref, gi_ref):
    t2, w = zc_ref.shape[1], zc_ref.shape[3]
    for j in range(zc_ref.shape[2]):
        data = jnp.concatenate([_rows_at(zc_ref, j), _rows_at(zs_ref, j)], axis=0)
        g = _dot(d_ref[...], data)
        g_r, g_i = g[:t2], g[t2:]
        c = jnp.concatenate([twc_ref[j]] * (w // LANES), axis=1)
        s = jnp.concatenate([tws_ref[j]] * (w // LANES), axis=1)
        gr_ref[0, j] = g_r * c + g_i * s
        gi_ref[0, j] = g_i * c - g_r * s


def _fnet_stage2_kernel(d_ref, gr_ref, gi_ref, o_ref, *, scale):
    t1, w = gr_ref.shape[1], gr_ref.shape[3]
    for j in range(gr_ref.shape[2]):
        data = jnp.concatenate([_rows_at(gr_ref, j), _rows_at(gi_ref, j)], axis=0)
        o_ref[0, :, j:j + 1, :] = (_dot(d_ref[...], data) * scale).reshape(t1, 1, w)


def _dft_tables(t):
    t2 = FNET_T2
    t1 = t // t2
    assert t1 * t2 == t

    def cos_sin(rows, cols, n):
        k = (lax.iota(jnp.int32, rows)[:, None] * lax.iota(jnp.int32, cols)[None, :]) % n
        ang = k.astype(F32) * (2.0 * math.pi / n)
        return jnp.cos(ang), jnp.sin(ang)

    c2, s2 = cos_sin(t2, t2, t2)
    c1, s1 = cos_sin(t1, t1, t1)
    twc, tws = cos_sin(t1, t2, t)
    lanes = lambda m: jnp.broadcast_to(m[:, :, None], (t1, t2, LANES))
    return {
        "stage1": jnp.block([[c2, -s2], [-s2, -c2]]).astype(BF16),
        "stage2": jnp.concatenate([c1, s1], axis=1).astype(BF16),
        "twc": lanes(twc), "tws": lanes(tws),
    }


def _fnet(zc, zs, tab):
    b, t, w = zc.shape
    t2 = FNET_T2
    t1 = t // t2
    n1 = min(FNET_ROWS, t1)
    in_spec = pl.BlockSpec((1, t2, n1, w), lambda i, j: (i, 0, j, 0))
    tw_spec = pl.BlockSpec((n1, t2, LANES), lambda i, j: (j, 0, 0))
    g_shape = jax.ShapeDtypeStruct((b, t1, t2, w), F32)
    g_r, g_i = pl.pallas_call(
        _fnet_stage1_kernel,
        grid=(b, t1 // n1),
        in_specs=[_const_spec(tab["stage1"].shape), in_spec, in_spec, tw_spec, tw_spec],
        out_specs=[pl.BlockSpec((1, n1, t2, w), lambda i, j: (i, j, 0, 0))] * 2,
        out_shape=[g_shape, g_shape],
        compiler_params=_params(("parallel", "parallel")),
        name="fnet_stage1",
    )(tab["stage1"], zc.reshape(b, t2, t1, w), zs.reshape(b, t2, t1, w), tab["twc"], tab["tws"])
    n2 = min(FNET_ROWS, t2)
    spec = pl.BlockSpec((1, t1, n2, w), lambda i, j: (i, 0, j, 0))
    y = pl.pallas_call(
        functools.partial(_fnet_stage2_kernel, scale=1.0 / math.sqrt(t * HEAD_DIM)),
        grid=(b, t2 // n2),
        in_specs=[_const_spec(tab["stage2"].shape), spec, spec],
        out_specs=spec,
        out_shape=jax.ShapeDtypeStruct((b, t1, t2, w), F32),
        compiler_params=_params(("parallel", "parallel")),
        name="fnet_stage2",
    )(tab["stage2"], g_r, g_i)
    return y.reshape(b, t, w)


def _conv3(main, prev_row, next_row, w):
    tm = main.shape[0]
    ri = lax.broadcasted_iota(jnp.int32, main.shape, 0)
    up = jnp.where(ri == 0, prev_row, pltpu.roll(main, 1, 0))
    dn = jnp.where(ri == tm - 1, next_row, pltpu.roll(main, tm - 1, 0))
    return w[0:1] * up + w[1:2] * main + w[2:3] * dn


def _prep_kernel(um_ref, ump_ref, umn_ref, ut_ref, utp_ref, utn_ref, cwm_ref, cwt_ref,
                 w0_ref, w2_ref, a0_ref, a2_ref, g2_ref, kk_ref, ka_ref, rk_ref, hs_ref, shift_ref,
                 r_ref, k_ref, v_ref, a_ref, b_ref, ef_ref, eb_ref, g_ref, bonus_ref):
    j = pl.program_id(1)
    first = j == 0
    last = j == pl.num_programs(1) - 1
    hp = ump_ref.shape[1]

    def conv_main(c0, c1):
        main = um_ref[0, :, c0:c1]
        tm = main.shape[0]
        shifted = jnp.dot(shift_ref[...], main, preferred_element_type=F32)
        ri = lax.broadcasted_iota(jnp.int32, main.shape, 0)
        prev = jnp.where(first, 0.0, ump_ref[0, hp - 1:hp, c0:c1].astype(F32))
        nxt = jnp.where(last, 0.0, umn_ref[0, 0:1, c0:c1].astype(F32))
        up = jnp.where(ri == 0, prev, shifted[:tm])
        dn = jnp.where(ri == tm - 1, nxt, shifted[tm:])
        w = cwm_ref[:, c0:c1]
        return w[0:1] * up + w[1:2] * main.astype(F32) + w[2:3] * dn

    w = RWKV_WIDTH
    r = conv_main(0, w)
    k = conv_main(w, 2 * w)
    v = conv_main(2 * w, 3 * w)
    tp = utp_ref.shape[1]
    tail = _conv3(ut_ref[0], jnp.where(first, 0.0, utp_ref[0, tp - 1:tp, :]),
                  jnp.where(last, 0.0, utn_ref[0, 0:1, :]), cwt_ref[...])
    xw, xa, xg = tail[:, 0:LANES], tail[:, LANES:2 * LANES], tail[:, 2 * LANES:3 * LANES]

    e = jax.nn.sigmoid(w0_ref[...] + _dot_3pass(jnp.tanh(xw), w2_ref)) * math.exp(-0.5)
    ef_ref[0] = e[:, :w]
    eb_ref[0] = e[:, w:]
    a = jax.nn.sigmoid(a0_ref[...] + _dot_3pass(xa, a2_ref))
    g_ref[0] = _dot_3pass(jax.nn.sigmoid(xg), g2_ref).astype(BF16)

    kk = k * kk_ref[...]
    kk = kk * lax.rsqrt(_head_reduce(kk * kk, hs_ref[...]) + 1e-12)
    k = k * (1.0 + (a - 1.0) * ka_ref[...])
    r_ref[0] = r.astype(BF16)
    k_ref[0] = k.astype(BF16)
    v_ref[0] = v.astype(BF16)
    a_ref[0] = (-kk).astype(BF16)
    b_ref[0] = (kk * a).astype(BF16)
    bonus_ref[0] = (_head_reduce(r * k * rk_ref[...], hs_ref[...]) * v).astype(BF16)


def _prep(um, ut, p):
    b, t, wm = um.shape
    wt = ut.shape[2]
    tm = _row_tile(t, 256)
    hm, ht = 16, 8
    nm, nt = tm // hm, tm // ht
    main = lambda w: pl.BlockSpec((1, tm, w), lambda i, j: (i, j, 0))
    prev = lambda h, n, w: pl.BlockSpec((1, h, w), lambda i, j: (i, jnp.maximum(j * n - 1, 0), 0))
    nxt = lambda h, n, w: pl.BlockSpec(
        (1, h, w), lambda i, j: (i, jnp.minimum((j + 1) * n, t // h - 1), 0))
    consts = [p["conv_main"], p["conv_tail"], p["decay_w0"], p["decay_w2"], p["iclr_a0"],
              p["iclr_a2"], p["gate_g2"], p["k_k"], p["k_a"], p["r_k"], p["head_sum"],
              jnp.concatenate([jnp.eye(tm, k=-1, dtype=BF16), jnp.eye(tm, k=1, dtype=BF16)], axis=0)]
    outs = [BF16] * 5 + [F32, F32, BF16, BF16]
    return pl.pallas_call(
        _prep_kernel,
        grid=(b, t // tm),
        in_specs=[main(wm), prev(hm, nm, wm), nxt(hm, nm, wm),
                  main(wt), prev(ht, nt, wt), nxt(ht, nt, wt)]
                 + [_const_spec(c.shape) for c in consts],
        out_specs=[main(RWKV_WIDTH) for _ in outs],
        out_shape=[jax.ShapeDtypeStruct((b, t, RWKV_WIDTH), dt) for dt in outs],
        compiler_params=_params(("parallel", "parallel")),
        name="rwkv_prep",
    )(um, um, um, ut, ut, ut, *consts)


def _wkv_kernel(r_ref, k_ref, v_ref, a_ref, b_ref, e_ref, *rest, reverse):
    *add_refs, o_ref, s_ref = rest
    add_ref = add_refs[0] if add_refs else None
    c = r_ref.shape[1]
    pairs = range(s_ref.shape[0])
    heads = range(LANES // HEAD_DIM)
    chains = [(p, h) for p in pairs for h in heads]

    @pl.when(pl.program_id(1) == 0)
    def _():
        s_ref[...] = jnp.zeros_like(s_ref)

    row = lax.broadcasted_iota(jnp.int32, (c, c), 0)
    col = lax.broadcasted_iota(jnp.int32, (c, c), 1)
    if reverse:
        incl, strict = col >= row, col > row
    else:
        incl, strict = col <= row, col < row
    tri = jnp.where(incl, 1.0, 0.0).astype(BF16)
    end = 0 if reverse else c - 1
    lane = lax.broadcasted_iota(jnp.int32, (1, LANES), 1)
    in_head = [(lane >= h * HEAD_DIM) & (lane < (h + 1) * HEAD_DIM) for h in heads]
    same_head = (lax.broadcasted_iota(jnp.int32, (LANES, LANES), 0) // HEAD_DIM
                 == lax.broadcasted_iota(jnp.int32, (LANES, LANES), 1) // HEAD_DIM)

    per_row = r_ref.shape[2] // LANES

    def tile(ref, p):
        return ref[p // per_row, :, (p % per_row) * LANES:(p % per_row + 1) * LANES]

    tri2 = jnp.concatenate([tri, tri], axis=1)

    def running_sum(e):
        return jnp.dot(tri2, jnp.concatenate(_split_bf16(e, 2), axis=0), preferred_element_type=F32)

    cum = [running_sum(tile(e_ref, p)) for p in pairs]
    total = [x[end:end + 1, :] for x in cum]
    ref = [x[c // 2:c // 2 + 1, :] for x in cum]
    dl = [cum[p] - ref[p] for p in pairs]
    g_inv = [jnp.exp(dl[p]) for p in pairs]
    at = [tile(a_ref, p).astype(F32) * jnp.exp(tile(e_ref, p) - dl[p]) for p in pairs]
    rt = [tile(r_ref, p).astype(F32) * jnp.exp(-dl[p]) for p in pairs]
    bt = [tile(b_ref, p).astype(F32) * g_inv[p] for p in pairs]
    kt = [tile(k_ref, p).astype(F32) * g_inv[p] for p in pairs]
    g_end = [jnp.exp(ref[p] - total[p]) for p in pairs]
    bh = [(bt[p] * g_end[p]).astype(BF16) for p in pairs]
    kh = [(kt[p] * g_end[p]).astype(BF16) for p in pairs]
    bk = [jnp.concatenate([bt[p], kt[p]], axis=0).astype(BF16) for p in pairs]
    ar = [jnp.concatenate([at[p], rt[p]], axis=0) for p in pairs]
    vb = [tile(v_ref, p) for p in pairs]

    prod = [_dot_nt(jnp.where(in_head[h], ar[p], 0.0), bk[p]) for p, h in chains]
    l_ab = [jnp.where(strict, x[:c, :c], 0.0) for x in prod]
    l_ak = [jnp.where(strict, x[:c, c:], 0.0) for x in prod]
    m_rb = [jnp.where(incl, x[c:, :c], 0.0).astype(BF16) for x in prod]
    m_rk = [jnp.where(incl, x[c:, c:], 0.0).astype(BF16) for x in prod]
    del prod
    eye = jnp.where(row == col, 1.0, 0.0)
    t = [eye + jnp.where((row >> 1) == (col >> 1), x, 0.0) for x in l_ab]
    nch = range(len(chains))
    for sh in range(1, (c - 1).bit_length()):
        m = 1 << sh
        joins = ((row >> (sh + 1)) == (col >> (sh + 1))) & ((row >> sh) != (col >> sh))
        e_k = [jnp.where(joins, l_ab[i], 0.0) for i in nch]
        if m < 8:
            et = [_dot(e_k[i], t[i]) for i in nch]
            t = [t[i] + _dot(t[i], et[i]) for i in nch]
            continue
        live = [slice(b0 + (0 if reverse else m), b0 + (m if reverse else 2 * m))
                for b0 in range(0, c, 2 * m)]

        def take(v):
            return jnp.concatenate([v[rs] for rs in live], axis=0)

        def spread(v):
            zero = jnp.zeros((m, v.shape[1]), v.dtype)
            parts = []
            for j in range(len(live)):
                blk = v[j * m:(j + 1) * m]
                parts += [blk, zero] if reverse else [zero, blk]
            return jnp.concatenate(parts, axis=0)

        et = [_dot(take(e_k[i]), t[i]) for i in nch]
        upd = [_dot(take(t[i]), spread(et[i])) for i in nch]
        t = [t[i] + spread(upd[i]) for i in nch]
    lakv = [jnp.dot(l_ak[i].astype(BF16), vb[p], preferred_element_type=F32)
            for i, (p, h) in enumerate(chains)]
    x = [_dot(t[i], jnp.concatenate([at[p], lakv[i]], axis=1)) for i, (p, h) in enumerate(chains)]
    x_a = [v[:, :LANES] for v in x]
    x_u = [v[:, LANES:] for v in x]
    y_a = [jnp.dot(m_rb[i], x_a[i].astype(BF16), preferred_element_type=F32) for i in nch]
    y_o = [jnp.dot(jnp.concatenate([m_rb[i], m_rk[i]], axis=1),
                   jnp.concatenate([x_u[i].astype(BF16), vb[p]], axis=0),
                   preferred_element_type=F32) for i, (p, h) in enumerate(chains)]

    def pick(vals, p):
        out = vals[p * len(heads)]
        for h in heads[1:]:
            out = jnp.where(in_head[h], vals[p * len(heads) + h], out)
        return out

    a_solved = [pick(x_a, p) for p in pairs]
    u = [pick(x_u, p) for p in pairs]
    r_hat = [rt[p] + pick(y_a, p) for p in pairs]
    o_intra = [pick(y_o, p) for p in pairs]
    s = [s_ref[p] for p in pairs]
    s_hat = [(s[p] * jnp.exp(-ref[p])).astype(BF16) for p in pairs]
    out = [_dot_nt(r_hat[p], s_hat[p]) + o_intra[p] for p in pairs]
    g_t = [jnp.where(same_head, _dot_tn(a_solved[p], bh[p]), 0.0) for p in pairs]
    h_t = [jnp.where(same_head,
                     _dot_tn(jnp.concatenate([u[p].astype(BF16), vb[p]], axis=0),
                             jnp.concatenate([bh[p], kh[p]], axis=0)), 0.0) for p in pairs]
    s_new = [s[p] * jnp.exp(-total[p]) + _dot(s_hat[p], g_t[p]) + h_t[p] for p in pairs]
    for p in pairs:
        if add_ref is not None:
            out[p] = out[p] + tile(add_ref, p).astype(F32)
        o_ref[p // per_row, :, (p % per_row) * LANES:(p % per_row + 1) * LANES] = (
            out[p].astype(o_ref.dtype))
        s_ref[p] = s_new[p]


def _wkv(r, k, v, a, b, e, reverse, add=None):
    bsz, t, w = r.shape
    c = min(WKV_CHUNK, t)
    nc = t // c
    rows = math.gcd(bsz, 2)
    if reverse:
        idx = lambda i, j: (i, nc - 1 - j, 0)
    else:
        idx = lambda i, j: (i, j, 0)
    spec = pl.BlockSpec((rows, c, w), idx)
    return pl.pallas_call(
        functools.partial(_wkv_kernel, reverse=reverse),
        grid=(bsz // rows, nc),
        in_specs=[spec] * (6 if add is None else 7),
        out_specs=spec,
        out_shape=jax.ShapeDtypeStruct((bsz, t, w), BF16),
        scratch_shapes=[pltpu.VMEM((rows * (w // LANES), LANES, LANES), F32)],
        compiler_params=_params(("parallel", "arbitrary")),
        name="wkv_bwd" if reverse else "wkv_fwd",
    )(r, k, v, a, b, e, *(() if add is None else (add,)))


def _mixout_kernel(o_ref, bonus_ref, g_ref, yf_ref, gates_ref, x_ref, mod_ref,
                   lnw_ref, lnb_ref, hm_ref, wfo_ref, wro_ref, wo_ref, out_ref):
    d = x_ref.shape[2]
    for rows in _sub_blocks(x_ref.shape[1]):
        o = o_ref[0, rows, :].astype(F32)
        dev = o - _head_reduce(o, hm_ref[...])
        var = _head_reduce(dev * dev, hm_ref[...])
        o = dev * lax.rsqrt(var + GN_EPS) * lnw_ref[...] + lnb_ref[...]
        y = (o + bonus_ref[0, rows, :].astype(F32)) * g_ref[0, rows, :].astype(F32)
        y_b = _dot(y, wro_ref[...])
        y_a = _dot(yf_ref[0, rows, :], wfo_ref[...])
        merged = (gates_ref[0, rows, :d].astype(F32) * y_a
                  + gates_ref[0, rows, d:].astype(F32) * y_b)
        out_ref[0, rows, :] = x_ref[0, rows, :] + mod_ref[0, 2:3, :] * _dot(merged, wo_ref[...])


def _mixout(o, bonus, g, yf, gates, x, mod, p):
    b, t, d = x.shape
    tm = _row_tile(t, 1024)
    row = lambda a: pl.BlockSpec((1, tm, a.shape[2]), lambda i, j: (i, j, 0))
    consts = [p["lnx_w"], p["lnx_b"], p["head_mean"], p["w_fnet_out"], p["w_rwkv_out"], p["w_o"]]
    acts = [o, bonus, g, yf, gates, x]
    return pl.pallas_call(
        _mixout_kernel,
        grid=(b, t // tm),
        in_specs=[row(a) for a in acts] + [pl.BlockSpec((1, N_MOD, d), lambda i, j: (i, 0, 0))]
                 + [_const_spec(c.shape) for c in consts],
        out_specs=pl.BlockSpec((1, tm, d), lambda i, j: (i, j, 0)),
        out_shape=jax.ShapeDtypeStruct((b, t, d), F32),
        compiler_params=_params(("parallel", "parallel")),
        name="mixout",
    )(*acts, mod, *consts)


def _ffn_kernel(x_ref, mod_ref, ln_ref, lnf_ref, wg_ref, wu_ref, wd_ref, out_ref, *, final_norm):
    for rows in _sub_blocks(x_ref.shape[1]):
        x = x_ref[0, rows, :]
        h = _norm_mod(x, ln_ref[...], mod_ref[0, 3:4, :], mod_ref[0, 4:5, :]).astype(BF16)
        acc = None
        for cols in _col_chunks(wg_ref.shape[1], 768):
            gate = jnp.dot(h, wg_ref[:, cols], preferred_element_type=F32)
            up = jnp.dot(h, wu_ref[:, cols], preferred_element_type=F32)
            part = _dot(_silu(gate) * up, wd_ref[cols, :])
            acc = part if acc is None else acc + part
        out = x + mod_ref[0, 5:6, :] * acc
        if final_norm:
            out = _norm_mod(out, lnf_ref[...], 0.0, 0.0)
        out_ref[0, rows, :] = out


def _ffn_dense(x, mod, ln, ln_final, wg, wu, wd, final_norm):
    b, t, d = x.shape
    tm = _row_tile(t, 1024)
    consts = [ln, ln_final, wg, wu, wd]
    return pl.pallas_call(
        functools.partial(_ffn_kernel, final_norm=final_norm),
        grid=(b, t // tm),
        in_specs=[pl.BlockSpec((1, tm, d), lambda i, j: (i, j, 0)),
                  pl.BlockSpec((1, N_MOD, d), lambda i, j: (i, 0, 0))]
                 + [_const_spec(c.shape) for c in consts],
        out_specs=pl.BlockSpec((1, tm, d), lambda i, j: (i, j, 0)),
        out_shape=jax.ShapeDtypeStruct((b, t, d), F32),
        compiler_params=_params(("parallel", "parallel")),
        name="ffn_dense",
    )(x, mod, *consts)


def _route_kernel(x_ref, mod_ref, ln_ref, rw_ref, h_ref, route_ref, cnt_ref):
    h = _norm_mod(x_ref[0], ln_ref[...], mod_ref[0, 3:4, :], mod_ref[0, 4:5, :])
    h_ref[0] = h
    tm = h.shape[0]
    logits = lax.dot_general(rw_ref[...], h, (((1,), (1,)), ((), ())), precision=HIGHEST,
                             preferred_element_type=F32)
    ne = logits.shape[0]
    ei = lax.broadcasted_iota(jnp.int32, logits.shape, 0)
    m1 = jnp.max(logits, axis=0, keepdims=True)
    i1 = jnp.min(jnp.where(logits == m1, ei, ne), axis=0, keepdims=True)
    rest = jnp.where(ei == i1, -jnp.inf, logits)
    m2 = jnp.max(rest, axis=0, keepdims=True)
    i2 = jnp.min(jnp.where(rest == m2, ei, ne), axis=0, keepdims=True)
    t2 = jnp.exp(m2 - m1)
    w1 = 1.0 / (1.0 + t2)
    w2 = t2 / (1.0 + t2)
    oh1 = jnp.where(ei == i1, 1.0, 0.0)
    oh2 = jnp.where(ei == i2, 1.0, 0.0)
    before = (lax.broadcasted_iota(jnp.int32, (tm, tm), 0)
              < lax.broadcasted_iota(jnp.int32, (tm, tm), 1))
    before = jnp.where(before, 1.0, 0.0).astype(BF16)
    cs1 = jnp.dot(oh1.astype(BF16), before, preferred_element_type=F32)
    cs2 = jnp.dot(oh2.astype(BF16), before, preferred_element_type=F32)
    n1 = jnp.sum(oh1, axis=1, keepdims=True)
    n2 = jnp.sum(oh2, axis=1, keepdims=True)
    size = jnp.broadcast_to(jnp.floor((n1 + n2 + (GROUP_ALIGN - 1)) * (1.0 / GROUP_ALIGN))
                            * GROUP_ALIGN, (ne, LANES))
    sub = lax.broadcasted_iota(jnp.int32, size.shape, 0)
    start = size
    for sh in (1, 2, 4):
        start = start + jnp.where(sub >= sh, pltpu.roll(start, sh, 0), 0.0)
    start = start - size
    pos1 = jnp.sum(oh1 * (cs1 + start[:, 0:1]), axis=0, keepdims=True)
    pos2 = jnp.sum(oh2 * (cs2 + n1 + start[:, 0:1]), axis=0, keepdims=True)
    zero = jnp.zeros_like(w1)
    route_ref[0, 0] = jnp.concatenate(
        [i1.astype(F32), i2.astype(F32), w1, w2, pos1, pos2, zero, zero], axis=0)
    cnt_ref[0, 0] = jnp.concatenate([size, start], axis=0)


def _route(x, mod, ln, router_t):
    b, t, d = x.shape
    tm = _row_tile(t, 512)
    nt = t // tm
    ne = router_t.shape[0]
    return pl.pallas_call(
        _route_kernel,
        grid=(b, nt),
        in_specs=[pl.BlockSpec((1, tm, d), lambda i, j: (i, j, 0)),
                  pl.BlockSpec((1, N_MOD, d), lambda i, j: (i, 0, 0)),
                  _const_spec(ln.shape), _const_spec(router_t.shape)],
        out_specs=[pl.BlockSpec((1, tm, d), lambda i, j: (i, j, 0)),
                   pl.BlockSpec((1, 1, 8, tm), lambda i, j: (i, j, 0, 0)),
                   pl.BlockSpec((1, 1, 2 * ne, LANES), lambda i, j: (i, j, 0, 0))],
        out_shape=[jax.ShapeDtypeStruct((b, t, d), F32),
                   jax.ShapeDtypeStruct((b, nt, 8, tm), F32),
                   jax.ShapeDtypeStruct((b, nt, 2 * ne, LANES), F32)],
        compiler_params=_params(("parallel", "parallel")),
        name="moe_route",
    )(x, mod, ln, router_t)


def _row_copy(src, dst, sem):
    return pltpu.make_async_copy(src, dst, sem)


def _sorted_rows(tm):
    return 2 * tm + LANES


def _group_copies(tab_ref, ne, make_copy, issue):
    per = GROUP_CHUNK // GROUP_ALIGN
    for e in range(ne):
        local, units, glob = tab_ref[0, 0, e], tab_ref[0, 0, ne + e], tab_ref[0, 0, 2 * ne + e]
        big = units // per

        def chunk(c, _, rows, first):
            off = pl.multiple_of(first + c * rows, GROUP_ALIGN)
            cp = make_copy(pl.ds(pl.multiple_of(local + off, GROUP_ALIGN), rows),
                           pl.ds(pl.multiple_of(glob + off, GROUP_ALIGN), rows))
            cp.start() if issue else cp.wait()
            return 0

        lax.fori_loop(0, big, functools.partial(chunk, rows=GROUP_CHUNK, first=0), 0)
        lax.fori_loop(0, units - big * per,
                      functools.partial(chunk, rows=GROUP_ALIGN, first=big * GROUP_CHUNK), 0)


def _dispatch_kernel(tab_ref, route_ref, h_ref, rows_in_ref, rows_ref, sorted_ref, sem):
    del rows_in_ref
    pos = route_ref[0, 4:6, :].astype(jnp.int32)
    j = lax.broadcasted_iota(jnp.int32, (sorted_ref.shape[0], h_ref.shape[0]), 0)
    pick = jnp.where((j == pos[0:1, :]) | (j == pos[1:2, :]), 1.0, 0.0).astype(BF16)
    sorted_ref[...] = jnp.dot(pick, h_ref[...].astype(BF16), preferred_element_type=F32)
    ne = tab_ref.shape[2] // 3
    make = lambda loc, glob: _row_copy(sorted_ref.at[loc], rows_ref.at[glob], sem)
    _group_copies(tab_ref, ne, make, True)
    _group_copies(tab_ref, ne, make, False)


def _dispatch(h, route, table, n_rows):
    n, d = h.shape
    tm = route.shape[2]
    rows = jnp.zeros((n_rows, d), h.dtype)
    return pl.pallas_call(
        _dispatch_kernel,
        grid=(n // tm,),
        in_specs=[pl.BlockSpec((1, 1, table.shape[2]), lambda i: (i, 0, 0),
                               memory_space=pltpu.SMEM),
                  pl.BlockSpec((1, 8, tm), lambda i: (i, 0, 0)),
                  pl.BlockSpec((tm, d), lambda i: (i, 0)),
                  pl.BlockSpec(memory_space=pl.ANY)],
        out_specs=pl.BlockSpec(memory_space=pl.ANY),
        out_shape=jax.ShapeDtypeStruct((n_rows, d), h.dtype),
        scratch_shapes=[pltpu.VMEM((_sorted_rows(tm), d), F32), pltpu.SemaphoreType.DMA],
        input_output_aliases={3: 0},
        compiler_params=pltpu.CompilerParams(dimension_semantics=("arbitrary",),
                                             vmem_limit_bytes=VMEM_LIMIT, has_side_effects=True),
        name="moe_dispatch",
    )(table, route, h, rows)


def _experts_kernel(be_ref, nb_ref, x_ref, wg_ref, wu_ref, wd_ref, y_ref):
    del be_ref

    @pl.when(pl.program_id(0) < nb_ref[0])
    def _():
        x = x_ref[...].astype(BF16)
        f = wg_ref.shape[2]
        step = f // 7
        acc = None
        for c0 in range(0, f, step):
            gate = jnp.dot(x, wg_ref[0, :, c0:c0 + step], preferred_element_type=F32)
            up = jnp.dot(x, wu_ref[0, :, c0:c0 + step], preferred_element_type=F32)
            part = _dot(_silu(gate) * up, wd_ref[0, c0:c0 + step, :])
            acc = part if acc is None else acc + part
        y_ref[...] = acc

    @pl.when(pl.program_id(0) >= nb_ref[0])
    def _():
        y_ref[...] = jnp.zeros_like(y_ref)


def _experts(rows, block_e, n_used, wg, wu, wd):
    n_rows, d = rows.shape
    rb = MOE_ROW_BLOCK
    f = wg.shape[2]
    wspec = lambda shape: pl.BlockSpec((1,) + shape, lambda i, be, nb: (be[i], 0, 0),
                                       pipeline_mode=pl.Buffered(1))
    return pl.pallas_call(
        _experts_kernel,
        grid_spec=pltpu.PrefetchScalarGridSpec(
            num_scalar_prefetch=2,
            grid=(n_rows // rb,),
            in_specs=[pl.BlockSpec((rb, d), lambda i, be, nb: (i, 0)),
                      wspec((d, f)), wspec((d, f)), wspec((f, d))],
            out_specs=pl.BlockSpec((rb, d), lambda i, be, nb: (i, 0))),
        out_shape=jax.ShapeDtypeStruct((n_rows, d), F32),
        compiler_params=_params(("arbitrary",)),
        name="moe_experts",
    )(block_e, n_used, rows, wg, wu, wd)


def _combine_kernel(tab_ref, route_ref, y_hbm, x_ref, mod_ref, lnf_ref, out_ref, sorted_ref, sem,
                    *, final_norm):
    @pl.when((pl.program_id(0) == 0) & (pl.program_id(1) == 0))
    def _():
        sorted_ref[...] = jnp.zeros_like(sorted_ref)

    ne = tab_ref.shape[2] // 3
    make = lambda loc, glob: _row_copy(y_hbm.at[glob], sorted_ref.at[loc], sem)
    _group_copies(tab_ref, ne, make, True)
    _group_copies(tab_ref, ne, make, False)
    pos = route_ref[0, 4:6, :].astype(jnp.int32)
    wts = route_ref[0, 2:4, :]
    j = lax.broadcasted_iota(jnp.int32, (sorted_ref.shape[0], x_ref.shape[1]), 0)
    q_t = (jnp.where(j == pos[0:1, :], wts[0:1, :], 0.0)
           + jnp.where(j == pos[1:2, :], wts[1:2, :], 0.0))
    q_hi, q_lo = _split_bf16(q_t, 2)
    z = sorted_ref[...].astype(BF16)
    f = _dot_tn(q_hi, z) + _dot_tn(q_lo, z)
    out = x_ref[0] + mod_ref[0, 5:6, :] * f
    if final_norm:
        out = _norm_mod(out, lnf_ref[...], 0.0, 0.0)
    out_ref[0] = out


def _combine(y_rows, route, table, x, mod, ln_final, final_norm):
    b, t, d = x.shape
    tm = route.shape[2]
    nt = t // tm
    return pl.pallas_call(
        functools.partial(_combine_kernel, final_norm=final_norm),
        grid=(b, nt),
        in_specs=[pl.BlockSpec((1, 1, table.shape[2]), lambda i, j: (i * nt + j, 0, 0),
                               memory_space=pltpu.SMEM),
                  pl.BlockSpec((1, 8, tm), lambda i, j: (i * nt + j, 0, 0)),
                  pl.BlockSpec(memory_space=pl.ANY),
                  pl.BlockSpec((1, tm, d), lambda i, j: (i, j, 0)),
                  pl.BlockSpec((1, N_MOD, d), lambda i, j: (i, 0, 0)),
                  _const_spec(ln_final.shape)],
        out_specs=pl.BlockSpec((1, tm, d), lambda i, j: (i, j, 0)),
        out_shape=jax.ShapeDtypeStruct((b, t, d), F32),
        scratch_shapes=[pltpu.VMEM((_sorted_rows(tm), d), F32), pltpu.SemaphoreType.DMA],
        compiler_params=_params(("arbitrary", "arbitrary")),
        name="moe_combine",
    )(table, route, y_rows, x, mod, ln_final)


def _ffn_moe(x, mod, ln, ln_final, router_t, wg, wu, wd, final_norm):
    b, t, d = x.shape
    n = b * t
    ne = router_t.shape[0]
    rb = MOE_ROW_BLOCK
    h, route, cnt = _route(x, mod, ln, router_t)
    nt, tm = route.shape[1], route.shape[3]
    tiles = b * nt
    cnt = cnt[..., 0].reshape(tiles, 2 * ne).astype(jnp.int32)
    size, local = cnt[:, :ne], cnt[:, ne:]
    totals = jnp.sum(size, axis=0)
    padded = ((totals + rb - 1) // rb) * rb
    padded_end = jnp.cumsum(padded)
    base = (padded_end - padded)[None, :] + jnp.cumsum(size, axis=0) - size
    table = jnp.concatenate([local, size // GROUP_ALIGN, base], axis=1).reshape(tiles, 1, 3 * ne)
    route = route.reshape(tiles, 8, tm)
    n_blocks = -(-(n * 2 + tiles * ne * (GROUP_ALIGN - 1)) // rb) + ne
    block_e = jnp.minimum(
        jnp.searchsorted(padded_end, jnp.arange(n_blocks, dtype=jnp.int32) * rb, side="right"),
        ne - 1).astype(jnp.int32)
    n_used = (padded_end[-1:] // rb).astype(jnp.int32)
    rows = _dispatch(h.reshape(n, d), route, table, n_blocks * rb)
    y_rows = _experts(rows, block_e, n_used, wg, wu, wd)
    return _combine(y_rows, route, table, x, mod, ln_final, final_norm)


def _block_diag(m, n):
    return jnp.kron(jnp.eye(n, dtype=m.dtype), m)


def _layer_params(i, w_in, conv_w, decay_w0, decay_w2, iclr_a0, iclr_a2, gate_g2, k_k, k_a, r_k,
                  lnx_w, lnx_b, w_fnet_out, w_rwkv_out, w_o):
    w = RWKV_WIDTH
    lora = decay_w2.shape[2]
    tail0 = FNET_WIDTH + 3 * w

    def reorder_tail(m):
        pad = jnp.zeros(m.shape[:-1] + (LANES - lora,), m.dtype)
        return jnp.concatenate([m[..., :3 * lora], pad, m[..., 3 * lora:]], axis=-1)

    row = lambda v: v.reshape(1, -1)
    hi_lo = lambda m: jnp.stack(_split_bf16(m, 2))
    wi = w_in[i]
    rwkv_cols = conv_w.shape[2]
    zeros = jnp.zeros((lora, w), F32)
    return {
        "w_a": wi[:, :FNET_WIDTH].astype(BF16),
        "w_main": wi[:, FNET_WIDTH:tail0].astype(BF16),
        "w_tail": reorder_tail(wi[:, tail0:FNET_WIDTH + rwkv_cols]).astype(BF16),
        "w_gates": wi[:, FNET_WIDTH + rwkv_cols:].astype(BF16),
        "conv_main": conv_w[i][:, :3 * w],
        "conv_tail": reorder_tail(conv_w[i][:, 3 * w:]),
        "decay_w0": decay_w0[i].reshape(1, 2 * w),
        "decay_w2": hi_lo(jnp.concatenate(
            [jnp.concatenate([decay_w2[i, 0], zeros], axis=1),
             jnp.concatenate([zeros, decay_w2[i, 1]], axis=1)], axis=0)),
        "iclr_a0": row(iclr_a0[i]),
        "iclr_a2": hi_lo(jnp.concatenate([iclr_a2[i], jnp.zeros((LANES - lora, w), F32)], axis=0)),
        "gate_g2": hi_lo(gate_g2[i]),
        "k_k": row(k_k[i]), "k_a": row(k_a[i]), "r_k": row(r_k[i]),
        "lnx_w": row(lnx_w[i]), "lnx_b": row(lnx_b[i]),
        "head_sum": _block_diag(jnp.ones((HEAD_DIM, HEAD_DIM), BF16), LANES // HEAD_DIM),
        "head_mean": _block_diag(jnp.full((HEAD_DIM, HEAD_DIM), 1.0 / HEAD_DIM, BF16),
                                 LANES // HEAD_DIM),
        "w_fnet_out": w_fnet_out[i].astype(BF16),
        "w_rwkv_out": w_rwkv_out[i].astype(BF16),
        "w_o": w_o[i].astype(BF16),
    }


def _channel_dft():
    k = jnp.arange(HEAD_DIM, dtype=jnp.int32)
    ang = ((k[:, None] * k[None, :]) % HEAD_DIM).astype(F32) * (2.0 * math.pi / HEAD_DIM)
    groups = FNET_WIDTH // HEAD_DIM
    return jnp.concatenate([_block_diag(jnp.cos(ang), groups), _block_diag(jnp.sin(ang), groups)],
                           axis=1).astype(BF16)


def _token_mixer(x, mod, ln, p, cs, dft_tab):
    zc, zs, um, ut, gates = _inproj(x, mod, ln, p["w_a"], p["w_main"], p["w_tail"], p["w_gates"], cs)
    yf = _fnet(zc, zs, dft_tab)
    r, k, v, a, b, e_f, e_b, g, bonus = _prep(um, ut, p)
    o = _wkv(r, k, v, a, b, e_b, True, add=_wkv(r, k, v, a, b, e_f, False))
    return _mixout(o, bonus, g, yf, gates, x, mod, p)


def kernel(x_prompt, x_sample, c_prompt, c_sample, w_ada, b_ada, ln_mix, w_in, conv_w, decay_w0, decay_w2, iclr_a0, iclr_a2, gate_g2, k_k, k_a, r_k, lnx_w, lnx_b, w_fnet_out, w_rwkv_out, w_o, ln_ffn, ff_w_gate, ff_w_up, ff_w_down, router_w, moe_w_gate, moe_w_up, moe_w_down, ln_final):
    depth, d = ln_mix.shape
    streams = [x_prompt, x_sample]
    nb = [x.shape[0] for x in streams]
    mod_all = _ada(jnp.concatenate([c_prompt, c_sample], axis=0), w_ada, b_ada)
    cs = _channel_dft()
    dfts = {}
    for x in streams:
        if x.shape[1] not in dfts:
            dfts[x.shape[1]] = _dft_tables(x.shape[1])
    lnf = ln_final.reshape(1, d)
    for i in range(depth):
        p = _layer_params(i, w_in, conv_w, decay_w0, decay_w2, iclr_a0, iclr_a2, gate_g2, k_k, k_a,
                          r_k, lnx_w, lnx_b, w_fnet_out, w_rwkv_out, w_o)
        j = i // 2
        final = i == depth - 1
        if i % 2 == 0:
            ffw = (ff_w_gate[j].astype(BF16), ff_w_up[j].astype(BF16), ff_w_down[j].astype(BF16))
        else:
            ffw = (moe_w_gate[j].astype(BF16), moe_w_up[j].astype(BF16), moe_w_down[j].astype(BF16))
            router_t = router_w[j].T
        off = 0
        for s, x in enumerate(streams):
            mod = mod_all[i, off:off + nb[s]].reshape(nb[s], N_MOD, d)
            off += nb[s]
            x = _token_mixer(x, mod, ln_mix[i].reshape(1, d), p, cs, dfts[x.shape[1]])
            ln2 = ln_ffn[i].reshape(1, d)
            if i % 2 == 0:
                x = _ffn_dense(x, mod, ln2, lnf, *ffw, final)
            else:
                x = _ffn_moe(x, mod, ln2, lnf, router_t, *ffw, final)
            streams[s] = x
    return tuple(streams)
```

```python
import functools
import math

import jax
import jax.numpy as jnp
from jax import lax
from jax.experimental import pallas as pl
from jax.experimental.pallas import tpu as pltpu

F32 = jnp.float32
BF16 = jnp.bfloat16
HIGHEST = lax.Precision.HIGHEST

HEAD_DIM = 64
LANES = 128
FNET_WIDTH = 256
RWKV_WIDTH = 768
LORA_TAIL = 384
N_MOD = 6
N_EXPERTS = 8
RMS_EPS = 1e-6
GN_EPS = 64e-5
WKV_CHUNK = 128
MOE_ROW_BLOCK = 512
GROUP_ALIGN = 8
GROUP_CHUNK = 64
VMEM_LIMIT = 56 * 1024 * 1024


def _params(sem, vmem=VMEM_LIMIT):
    return pltpu.CompilerParams(dimension_semantics=sem, vmem_limit_bytes=vmem)


def _const_spec(shape):
    nd = len(shape)
    return pl.BlockSpec(shape, lambda *_: (0,) * nd, pipeline_mode=pl.Buffered(1))


def _dot(a, b):
    return jnp.dot(a.astype(BF16), b.astype(BF16), preferred_element_type=F32)


def _dot_nt(a, b):
    return lax.dot_general(a.astype(BF16), b.astype(BF16), (((1,), (1,)), ((), ())),
                           preferred_element_type=F32)


def _dot_tn(a, b):
    return lax.dot_general(a.astype(BF16), b.astype(BF16), (((0,), (0,)), ((), ())),
                           preferred_element_type=F32)


def _split_bf16(x, parts):
    out, rem = [], x
    for _ in range(parts):
        p = rem.astype(BF16)
        out.append(p)
        rem = rem - p.astype(F32)
    return out


def _dot_exact_rhs(x, m, parts):
    acc = None
    for p in _split_bf16(x, parts):
        t = jnp.dot(p, m, preferred_element_type=F32)
        acc = t if acc is None else acc + t
    return acc


def _dot_f32(a, b):
    return jnp.dot(a, b, precision=HIGHEST, preferred_element_type=F32)


def _dot_3pass(a, w_ref):
    ah, al = _split_bf16(a, 2)
    d = lambda x, y: jnp.dot(x, y, preferred_element_type=F32)
    return d(ah, w_ref[0]) + (d(al, w_ref[0]) + d(ah, w_ref[1]))


def _norm_mod(x, gain, shift, scale):
    ms = jnp.mean(x * x, axis=-1, keepdims=True)
    return x * lax.rsqrt(ms + RMS_EPS) * gain * (1.0 + scale) + shift


def _silu(x):
    return x * jax.nn.sigmoid(x)


def _head_reduce(x, m):
    cols = [_dot_exact_rhs(x[:, j:j + LANES], m, 2) for j in range(0, x.shape[1], LANES)]
    return jnp.concatenate(cols, axis=1)


def _row_tile(t, want):
    tm = min(t, want)
    assert t % tm == 0
    return tm


def _sub_blocks(rows, size=512):
    size = min(size, rows)
    assert rows % size == 0
    return [slice(r, r + size) for r in range(0, rows, size)]


def _col_chunks(width, size):
    return [slice(c, min(c + size, width)) for c in range(0, width, size)]


def _ada_kernel(c_ref, w_ref, b_ref, o_ref):
    o_ref[0] = _dot_f32(_silu(c_ref[...]), w_ref[0]) + b_ref[0]


def _ada(c_all, w_ada, b_ada):
    depth, d, n = w_ada.shape
    bc = c_all.shape[0]
    tn = n // 4
    return pl.pallas_call(
        _ada_kernel,
        grid=(depth, n // tn),
        in_specs=[pl.BlockSpec((bc, d), lambda l, j: (0, 0)),
                  pl.BlockSpec((1, d, tn), lambda l, j: (l, 0, j)),
                  pl.BlockSpec((1, 1, tn), lambda l, j: (l, 0, j))],
        out_specs=pl.BlockSpec((1, bc, tn), lambda l, j: (l, 0, j)),
        out_shape=jax.ShapeDtypeStruct((depth, bc, n), F32),
        compiler_params=_params(("parallel", "parallel")),
        name="ada",
    )(c_all, w_ada, b_ada.reshape(depth, 1, n))


def _inproj_kernel(x_ref, mod_ref, ln_ref, wa_ref, wm_ref, wt_ref, wg_ref, cs_ref,
                   zc_ref, zs_ref, um_ref, ut_ref, g_ref):
    for rows in _sub_blocks(x_ref.shape[1]):
        h = _norm_mod(x_ref[0, rows, :], ln_ref[...], mod_ref[0, 0:1, :],
                      mod_ref[0, 1:2, :]).astype(BF16)
        ua = jnp.dot(h, wa_ref[...], preferred_element_type=F32)
        zz = jnp.dot(ua.astype(BF16), cs_ref[...], preferred_element_type=F32)
        zc_ref[0, rows, :] = zz[:, :FNET_WIDTH]
        zs_ref[0, rows, :] = zz[:, FNET_WIDTH:]
        for cols in _col_chunks(wm_ref.shape[1], 768):
            um_ref[0, rows, cols] = jnp.dot(
                h, wm_ref[:, cols], preferred_element_type=F32).astype(BF16)
        ut_ref[0, rows, :] = jnp.dot(h, wt_ref[...], preferred_element_type=F32)
        for cols in _col_chunks(wg_ref.shape[1], 512):
            g_ref[0, rows, cols] = jax.nn.sigmoid(
                jnp.dot(h, wg_ref[:, cols], preferred_element_type=F32)).astype(BF16)


def _inproj(x, mod, ln, wa, wm, wt, wg, cs):
    b, t, d = x.shape
    tm = _row_tile(t, 1024)
    row = lambda w: pl.BlockSpec((1, tm, w), lambda i, j: (i, j, 0))
    outs = [(FNET_WIDTH, F32), (FNET_WIDTH, F32), (wm.shape[1], BF16), (wt.shape[1], F32),
            (wg.shape[1], BF16)]
    return pl.pallas_call(
        _inproj_kernel,
        grid=(b, t // tm),
        in_specs=[row(d), pl.BlockSpec((1, N_MOD, d), lambda i, j: (i, 0, 0)),
                  _const_spec(ln.shape), _const_spec(wa.shape), _const_spec(wm.shape),
                  _const_spec(wt.shape), _const_spec(wg.shape), _const_spec(cs.shape)],
        out_specs=[row(w) for w, _ in outs],
        out_shape=[jax.ShapeDtypeStruct((b, t, w), dt) for w, dt in outs],
        compiler_params=_params(("parallel", "parallel")),
        name="inproj",
    )(x, mod, ln, wa, wm, wt, wg, cs)


FNET_T2 = 64
FNET_ROWS = 8


def _rows_at(ref, j):
    return ref[0, :, j:j + 1, :].reshape(ref.shape[1], ref.shape[3])


def _fnet_stage1_kernel(d_ref, zc_ref, zs_ref, twc_ref, tws_ref, gr_ref, gi_ref):
    t2, w = zc_ref.shape[1], zc_ref.shape[3]
    for j in range(zc_ref.shape[2]):
        data = jnp.concatenate([_rows_at(zc_ref, j), _rows_at(zs_ref, j)], axis=0)
        g = _dot(d_ref[...], data)
        g_r, g_i = g[:t2], g[t2:]
        c = jnp.concatenate([twc_ref[j]] * (w // LANES), axis=1)
        s = jnp.concatenate([tws_ref[j]] * (w // LANES), axis=1)
        gr_ref[0, j] = g_r * c + g_i * s
        gi_ref[0, j] = g_i * c - g_r * s


def _fnet_stage2_kernel(d_ref, gr_ref, gi_ref, o_ref, *, scale):
    t1, w = gr_ref.shape[1], gr_ref.shape[3]
    for j in range(gr_ref.shape[2]):
        data = jnp.concatenate([_rows_at(gr_ref, j), _rows_at(gi_ref, j)], axis=0)
        o_ref[0, :, j:j + 1, :] = (_dot(d_ref[...], data) * scale).reshape(t1, 1, w)


def _dft_tables(t):
    t2 = FNET_T2
    t1 = t // t2
    assert t1 * t2 == t

    def cos_sin(rows, cols, n):
        k = (lax.iota(jnp.int32, rows)[:, None] * lax.iota(jnp.int32, cols)[None, :]) % n
        ang = k.astype(F32) * (2.0 * math.pi / n)
        return jnp.cos(ang), jnp.sin(ang)

    c2, s2 = cos_sin(t2, t2, t2)
    c1, s1 = cos_sin(t1, t1, t1)
    twc, tws = cos_sin(t1, t2, t)
    lanes = lambda m: jnp.broadcast_to(m[:, :, None], (t1, t2, LANES))
    return {
        "stage1": jnp.block([[c2, -s2], [-s2, -c2]]).astype(BF16),
        "stage2": jnp.concatenate([c1, s1], axis=1).astype(BF16),
        "twc": lanes(twc), "tws": lanes(tws),
    }


def _fnet(zc, zs, tab):
    b, t, w = zc.shape
    t2 = FNET_T2
    t1 = t // t2
    n1 = min(FNET_ROWS, t1)
    in_spec = pl.BlockSpec((1, t2, n1, w), lambda i, j: (i, 0, j, 0))
    tw_spec = pl.BlockSpec((n1, t2, LANES), lambda i, j: (j, 0, 0))
    g_shape = jax.ShapeDtypeStruct((b, t1, t2, w), F32)
    g_r, g_i = pl.pallas_call(
        _fnet_stage1_kernel,
        grid=(b, t1 // n1),
        in_specs=[_const_spec(tab["stage1"].shape), in_spec, in_spec, tw_spec, tw_spec],
        out_specs=[pl.BlockSpec((1, n1, t2, w), lambda i, j: (i, j, 0, 0))] * 2,
        out_shape=[g_shape, g_shape],
        compiler_params=_params(("parallel", "parallel")),
        name="fnet_stage1",
    )(tab["stage1"], zc.reshape(b, t2, t1, w), zs.reshape(b, t2, t1, w), tab["twc"], tab["tws"])
    n2 = min(FNET_ROWS, t2)
    spec = pl.BlockSpec((1, t1, n2, w), lambda i, j: (i, 0, j, 0))
    y = pl.pallas_call(
        functools.partial(_fnet_stage2_kernel, scale=1.0 / math.sqrt(t * HEAD_DIM)),
        grid=(b, t2 // n2),
        in_specs=[_const_spec(tab["stage2"].shape), spec, spec],
        out_specs=spec,
        out_shape=jax.ShapeDtypeStruct((b, t1, t2, w), F32),
        compiler_params=_params(("parallel", "parallel")),
        name="fnet_stage2",
    )(tab["stage2"], g_r, g_i)
    return y.reshape(b, t, w)


def _conv3(main, prev_row, next_row, w):
    tm = main.shape[0]
    ri = lax.broadcasted_iota(jnp.int32, main.shape, 0)
    up = jnp.where(ri == 0, prev_row, pltpu.roll(main, 1, 0))
    dn = jnp.where(ri == tm - 1, next_row, pltpu.roll(main, tm - 1, 0))
    return w[0:1] * up + w[1:2] * main + w[2:3] * dn


def _prep_kernel(um_ref, ump_ref, umn_ref, ut_ref, utp_ref, utn_ref, cwm_ref, cwt_ref,
                 w0_ref, w2_ref, a0_ref, a2_ref, g2_ref, kk_ref, ka_ref, rk_ref, hs_ref, shift_ref,
                 r_ref, k_ref, v_ref, a_ref, b_ref, ef_ref, eb_ref, g_ref, bonus_ref):
    j = pl.program_id(1)
    first = j == 0
    last = j == pl.num_programs(1) - 1
    hp = ump_ref.shape[1]

    def conv_main(c0, c1):
        main = um_ref[0, :, c0:c1]
        tm = main.shape[0]
        shifted = jnp.dot(shift_ref[...], main, preferred_element_type=F32)
        ri = lax.broadcasted_iota(jnp.int32, main.shape, 0)
        prev = jnp.where(first, 0.0, ump_ref[0, hp - 1:hp, c0:c1].astype(F32))
        nxt = jnp.where(last, 0.0, umn_ref[0, 0:1, c0:c1].astype(F32))
        up = jnp.where(ri == 0, prev, shifted[:tm])
        dn = jnp.where(ri == tm - 1, nxt, shifted[tm:])
        w = cwm_ref[:, c0:c1]
        return w[0:1] * up + w[1:2] * main.astype(F32) + w[2:3] * dn

    w = RWKV_WIDTH
    r = conv_main(0, w)
    k = conv_main(w, 2 * w)
    v = conv_main(2 * w, 3 * w)
    tp = utp_ref.shape[1]
    tail = _conv3(ut_ref[0], jnp.where(first, 0.0, utp_ref[0, tp - 1:tp, :]),
                  jnp.where(last, 0.0, utn_ref[0, 0:1, :]), cwt_ref[...])
    xw, xa, xg = tail[:, 0:LANES], tail[:, LANES:2 * LANES], tail[:, 2 * LANES:3 * LANES]

    e = jax.nn.sigmoid(w0_ref[...] + _dot_3pass(jnp.tanh(xw), w2_ref)) * math.exp(-0.5)
    ef_ref[0] = e[:, :w]
    eb_ref[0] = e[:, w:]
    a = jax.nn.sigmoid(a0_ref[...] + _dot_3pass(xa, a2_ref))
    g_ref[0] = _dot_3pass(jax.nn.sigmoid(xg), g2_ref).astype(BF16)

    kk = k * kk_ref[...]
    kk = kk * lax.rsqrt(_head_reduce(kk * kk, hs_ref[...]) + 1e-12)
    k = k * (1.0 + (a - 1.0) * ka_ref[...])
    r_ref[0] = r.astype(BF16)
    k_ref[0] = k.astype(BF16)
    v_ref[0] = v.astype(BF16)
    a_ref[0] = (-kk).astype(BF16)
    b_ref[0] = (kk * a).astype(BF16)
    bonus_ref[0] = (_head_reduce(r * k * rk_ref[...], hs_ref[...]) * v).astype(BF16)


def _prep(um, ut, p):
    b, t, wm = um.shape
    wt = ut.shape[2]
    tm = _row_tile(t, 256)
    hm, ht = 16, 8
    nm, nt = tm // hm, tm // ht
    main = lambda w: pl.BlockSpec((1, tm, w), lambda i, j: (i, j, 0))
    prev = lambda h, n, w: pl.BlockSpec((1, h, w), lambda i, j: (i, jnp.maximum(j * n - 1, 0), 0))
    nxt = lambda h, n, w: pl.BlockSpec(
        (1, h, w), lambda i, j: (i, jnp.minimum((j + 1) * n, t // h - 1), 0))
    consts = [p["conv_main"], p["conv_tail"], p["decay_w0"], p["decay_w2"], p["iclr_a0"],
              p["iclr_a2"], p["gate_g2"], p["k_k"], p["k_a"], p["r_k"], p["head_sum"],
              jnp.concatenate([jnp.eye(tm, k=-1, dtype=BF16), jnp.eye(tm, k=1, dtype=BF16)], axis=0)]
    outs = [BF16] * 5 + [F32, F32, BF16, BF16]
    return pl.pallas_call(
        _prep_kernel,
        grid=(b, t // tm),
        in_specs=[main(wm), prev(hm, nm, wm), nxt(hm, nm, wm),
                  main(wt), prev(ht, nt, wt), nxt(ht, nt, wt)]
                 + [_const_spec(c.shape) for c in consts],
        out_specs=[main(RWKV_WIDTH) for _ in outs],
        out_shape=[jax.ShapeDtypeStruct((b, t, RWKV_WIDTH), dt) for dt in outs],
        compiler_params=_params(("parallel", "parallel")),
        name="rwkv_prep",
    )(um, um, um, ut, ut, ut, *consts)


def _wkv_kernel(r_ref, k_ref, v_ref, a_ref, b_ref, e_ref, *rest, reverse):
    *add_refs, o_ref, s_ref = rest
    add_ref = add_refs[0] if add_refs else None
    c = r_ref.shape[1]
    pairs = range(s_ref.shape[0])
    heads = range(LANES // HEAD_DIM)
    chains = [(p, h) for p in pairs for h in heads]

    @pl.when(pl.program_id(1) == 0)
    def _():
        s_ref[...] = jnp.zeros_like(s_ref)

    row = lax.broadcasted_iota(jnp.int32, (c, c), 0)
    col = lax.broadcasted_iota(jnp.int32, (c, c), 1)
    if reverse:
        incl, strict = col >= row, col > row
    else:
        incl, strict = col <= row, col < row
    tri = jnp.where(incl, 1.0, 0.0).astype(BF16)
    end = 0 if reverse else c - 1
    lane = lax.broadcasted_iota(jnp.int32, (1, LANES), 1)
    in_head = [(lane >= h * HEAD_DIM) & (lane < (h + 1) * HEAD_DIM) for h in heads]
    same_head = (lax.broadcasted_iota(jnp.int32, (LANES, LANES), 0) // HEAD_DIM
                 == lax.broadcasted_iota(jnp.int32, (LANES, LANES), 1) // HEAD_DIM)

    per_row = r_ref.shape[2] // LANES

    def tile(ref, p):
        return ref[p // per_row, :, (p % per_row) * LANES:(p % per_row + 1) * LANES]

    tri2 = jnp.concatenate([tri, tri], axis=1)

    def running_sum(e):
        return jnp.dot(tri2, jnp.concatenate(_split_bf16(e, 2), axis=0), preferred_element_type=F32)

    cum = [running_sum(tile(e_ref, p)) for p in pairs]
    total = [x[end:end + 1, :] for x in cum]
    ref = [x[c // 2:c // 2 + 1, :] for x in cum]
    dl = [cum[p] - ref[p] for p in pairs]
    g_inv = [jnp.exp(dl[p]) for p in pairs]
    at = [tile(a_ref, p).astype(F32) * jnp.exp(tile(e_ref, p) - dl[p]) for p in pairs]
    rt = [tile(r_ref, p).astype(F32) * jnp.exp(-dl[p]) for p in pairs]
    bt = [tile(b_ref, p).astype(F32) * g_inv[p] for p in pairs]
    kt = [tile(k_ref, p).astype(F32) * g_inv[p] for p in pairs]
    g_end = [jnp.exp(ref[p] - total[p]) for p in pairs]
    bh = [(bt[p] * g_end[p]).astype(BF16) for p in pairs]
    kh = [(kt[p] * g_end[p]).astype(BF16) for p in pairs]
    bk = [jnp.concatenate([bt[p], kt[p]], axis=0).astype(BF16) for p in pairs]
    ar = [jnp.concatenate([at[p], rt[p]], axis=0) for p in pairs]
    vb = [tile(v_ref, p) for p in pairs]

    prod = [_dot_nt(jnp.where(in_head[h], ar[p], 0.0), bk[p]) for p, h in chains]
    l_ab = [jnp.where(strict, x[:c, :c], 0.0) for x in prod]
    l_ak = [jnp.where(strict, x[:c, c:], 0.0) for x in prod]
    m_rb = [jnp.where(incl, x[c:, :c], 0.0).astype(BF16) for x in prod]
    m_rk = [jnp.where(incl, x[c:, c:], 0.0).astype(BF16) for x in prod]
    del prod
    eye = jnp.where(row == col, 1.0, 0.0)
    t = [eye + jnp.where((row >> 1) == (col >> 1), x, 0.0) for x in l_ab]
    nch = range(len(chains))
    for sh in range(1, (c - 1).bit_length()):
        m = 1 << sh
        joins = ((row >> (sh + 1)) == (col >> (sh + 1))) & ((row >> sh) != (col >> sh))
        e_k = [jnp.where(joins, l_ab[i], 0.0) for i in nch]
        if m < 8:
            et = [_dot(e_k[i], t[i]) for i in nch]
            t = [t[i] + _dot(t[i], et[i]) for i in nch]
            continue
        live = [slice(b0 + (0 if reverse else m), b0 + (m if reverse else 2 * m))
                for b0 in range(0, c, 2 * m)]

        def take(v):
            return jnp.concatenate([v[rs] for rs in live], axis=0)

        def spread(v):
            zero = jnp.zeros((m, v.shape[1]), v.dtype)
            parts = []
            for j in range(len(live)):
                blk = v[j * m:(j + 1) * m]
                parts += [blk, zero] if reverse else [zero, blk]
            return jnp.concatenate(parts, axis=0)

        et = [_dot(take(e_k[i]), t[i]) for i in nch]
        upd = [_dot(take(t[i]), spread(et[i])) for i in nch]
        t = [t[i] + spread(upd[i]) for i in nch]
    lakv = [jnp.dot(l_ak[i].astype(BF16), vb[p], preferred_element_type=F32)
            for i, (p, h) in enumerate(chains)]
    x = [_dot(t[i], jnp.concatenate([at[p], lakv[i]], axis=1)) for i, (p, h) in enumerate(chains)]
    x_a = [v[:, :LANES] for v in x]
    x_u = [v[:, LANES:] for v in x]
    y_a = [jnp.dot(m_rb[i], x_a[i].astype(BF16), preferred_element_type=F32) for i in nch]
    y_o = [jnp.dot(jnp.concatenate([m_rb[i], m_rk[i]], axis=1),
                   jnp.concatenate([x_u[i].astype(BF16), vb[p]], axis=0),
                   preferred_element_type=F32) for i, (p, h) in enumerate(chains)]

    def pick(vals, p):
        out = vals[p * len(heads)]
        for h in heads[1:]:
            out = jnp.where(in_head[h], vals[p * len(heads) + h], out)
        return out

    a_solved = [pick(x_a, p) for p in pairs]
    u = [pick(x_u, p) for p in pairs]
    r_hat = [rt[p] + pick(y_a, p) for p in pairs]
    o_intra = [pick(y_o, p) for p in pairs]
    s = [s_ref[p] for p in pairs]
    s_hat = [(s[p] * jnp.exp(-ref[p])).astype(BF16) for p in pairs]
    out = [_dot_nt(r_hat[p], s_hat[p]) + o_intra[p] for p in pairs]
    g_t = [jnp.where(same_head, _dot_tn(a_solved[p], bh[p]), 0.0) for p in pairs]
    h_t = [jnp.where(same_head,
                     _dot_tn(jnp.concatenate([u[p].astype(BF16), vb[p]], axis=0),
                             jnp.concatenate([bh[p], kh[p]], axis=0)), 0.0) for p in pairs]
    s_new = [s[p] * jnp.exp(-total[p]) + _dot(s_hat[p], g_t[p]) + h_t[p] for p in pairs]
    for p in pairs:
        if add_ref is not None:
            out[p] = out[p] + tile(add_ref, p).astype(F32)
        o_ref[p // per_row, :, (p % per_row) * LANES:(p % per_row + 1) * LANES] = (
            out[p].astype(o_ref.dtype))
        s_ref[p] = s_new[p]


def _wkv(r, k, v, a, b, e, reverse, add=None):
    bsz, t, w = r.shape
    c = min(WKV_CHUNK, t)
    nc = t // c
    rows = math.gcd(bsz, 2)
    if reverse:
        idx = lambda i, j: (i, nc - 1 - j, 0)
    else:
        idx = lambda i, j: (i, j, 0)
    spec = pl.BlockSpec((rows, c, w), idx)
    return pl.pallas_call(
        functools.partial(_wkv_kernel, reverse=reverse),
        grid=(bsz // rows, nc),
        in_specs=[spec] * (6 if add is None else 7),
        out_specs=spec,
        out_shape=jax.ShapeDtypeStruct((bsz, t, w), BF16),
        scratch_shapes=[pltpu.VMEM((rows * (w // LANES), LANES, LANES), F32)],
        compiler_params=_params(("parallel", "arbitrary")),
        name="wkv_bwd" if reverse else "wkv_fwd",
    )(r, k, v, a, b, e, *(() if add is None else (add,)))


def _mixout_kernel(o_ref, bonus_ref, g_ref, yf_ref, gates_ref, x_ref, mod_ref,
                   lnw_ref, lnb_ref, hm_ref, wfo_ref, wro_ref, wo_ref, out_ref):
    d = x_ref.shape[2]
    for rows in _sub_blocks(x_ref.shape[1]):
        o = o_ref[0, rows, :].astype(F32)
        dev = o - _head_reduce(o, hm_ref[...])
        var = _head_reduce(dev * dev, hm_ref[...])
        o = dev * lax.rsqrt(var + GN_EPS) * lnw_ref[...] + lnb_ref[...]
        y = (o + bonus_ref[0, rows, :].astype(F32)) * g_ref[0, rows, :].astype(F32)
        y_b = _dot(y, wro_ref[...])
        y_a = _dot(yf_ref[0, rows, :], wfo_ref[...])
        merged = (gates_ref[0, rows, :d].astype(F32) * y_a
                  + gates_ref[0, rows, d:].astype(F32) * y_b)
        out_ref[0, rows, :] = x_ref[0, rows, :] + mod_ref[0, 2:3, :] * _dot(merged, wo_ref[...])


def _mixout(o, bonus, g, yf, gates, x, mod, p):
    b, t, d = x.shape
    tm = _row_tile(t, 1024)
    row = lambda a: pl.BlockSpec((1, tm, a.shape[2]), lambda i, j: (i, j, 0))
    consts = [p["lnx_w"], p["lnx_b"], p["head_mean"], p["w_fnet_out"], p["w_rwkv_out"], p["w_o"]]
    acts = [o, bonus, g, yf, gates, x]
    return pl.pallas_call(
        _mixout_kernel,
        grid=(b, t // tm),
        in_specs=[row(a) for a in acts] + [pl.BlockSpec((1, N_MOD, d), lambda i, j: (i, 0, 0))]
                 + [_const_spec(c.shape) for c in consts],
        out_specs=pl.BlockSpec((1, tm, d), lambda i, j: (i, j, 0)),
        out_shape=jax.ShapeDtypeStruct((b, t, d), F32),
        compiler_params=_params(("parallel", "parallel")),
        name="mixout",
    )(*acts, mod, *consts)


def _ffn_kernel(x_ref, mod_ref, ln_ref, lnf_ref, wg_ref, wu_ref, wd_ref, out_ref, *, final_norm):
    for rows in _sub_blocks(x_ref.shape[1]):
        x = x_ref[0, rows, :]
        h = _norm_mod(x, ln_ref[...], mod_ref[0, 3:4, :], mod_ref[0, 4:5, :]).astype(BF16)
        acc = None
        for cols in _col_chunks(wg_ref.shape[1], 768):
            gate = jnp.dot(h, wg_ref[:, cols], preferred_element_type=F32)
            up = jnp.dot(h, wu_ref[:, cols], preferred_element_type=F32)
            part = _dot(_silu(gate) * up, wd_ref[cols, :])
            acc = part if acc is None else acc + part
        out = x + mod_ref[0, 5:6, :] * acc
        if final_norm:
            out = _norm_mod(out, lnf_ref[...], 0.0, 0.0)
        out_ref[0, rows, :] = out


def _ffn_dense(x, mod, ln, ln_final, wg, wu, wd, final_norm):
    b, t, d = x.shape
    tm = _row_tile(t, 1024)
    consts = [ln, ln_final, wg, wu, wd]
    return pl.pallas_call(
        functools.partial(_ffn_kernel, final_norm=final_norm),
        grid=(b, t // tm),
        in_specs=[pl.BlockSpec((1, tm, d), lambda i, j: (i, j, 0)),
                  pl.BlockSpec((1, N_MOD, d), lambda i, j: (i, 0, 0))]
                 + [_const_spec(c.shape) for c in consts],
        out_specs=pl.BlockSpec((1, tm, d), lambda i, j: (i, j, 0)),
        out_shape=jax.ShapeDtypeStruct((b, t, d), F32),
        compiler_params=_params(("parallel", "parallel")),
        name="ffn_dense",
    )(x, mod, *consts)


def _route_kernel(x_ref, mod_ref, ln_ref, rw_ref, h_ref, route_ref, cnt_ref):
    h = _norm_mod(x_ref[0], ln_ref[...], mod_ref[0, 3:4, :], mod_ref[0, 4:5, :])
    h_ref[0] = h
    tm = h.shape[0]
    logits = lax.dot_general(rw_ref[...], h, (((1,), (1,)), ((), ())), precision=HIGHEST,
                             preferred_element_type=F32)
    ne = logits.shape[0]
    ei = lax.broadcasted_iota(jnp.int32, logits.shape, 0)
    m1 = jnp.max(logits, axis=0, keepdims=True)
    i1 = jnp.min(jnp.where(logits == m1, ei, ne), axis=0, keepdims=True)
    rest = jnp.where(ei == i1, -jnp.inf, logits)
    m2 = jnp.max(rest, axis=0, keepdims=True)
    i2 = jnp.min(jnp.where(rest == m2, ei, ne), axis=0, keepdims=True)
    t2 = jnp.exp(m2 - m1)
    w1 = 1.0 / (1.0 + t2)
    w2 = t2 / (1.0 + t2)
    oh1 = jnp.where(ei == i1, 1.0, 0.0)
    oh2 = jnp.where(ei == i2, 1.0, 0.0)
    before = (lax.broadcasted_iota(jnp.int32, (tm, tm), 0)
              < lax.broadcasted_iota(jnp.int32, (tm, tm), 1))
    before = jnp.where(before, 1.0, 0.0).astype(BF16)
    cs1 = jnp.dot(oh1.astype(BF16), before, preferred_element_type=F32)
    cs2 = jnp.dot(oh2.astype(BF16), before, preferred_element_type=F32)
    n1 = jnp.sum(oh1, axis=1, keepdims=True)
    n2 = jnp.sum(oh2, axis=1, keepdims=True)
    size = jnp.broadcast_to(jnp.floor((n1 + n2 + (GROUP_ALIGN - 1)) * (1.0 / GROUP_ALIGN))
                            * GROUP_ALIGN, (ne, LANES))
    sub = lax.broadcasted_iota(jnp.int32, size.shape, 0)
    start = size
    for sh in (1, 2, 4):
        start = start + jnp.where(sub >= sh, pltpu.roll(start, sh, 0), 0.0)
    start = start - size
    pos1 = jnp.sum(oh1 * (cs1 + start[:, 0:1]), axis=0, keepdims=True)
    pos2 = jnp.sum(oh2 * (cs2 + n1 + start[:, 0:1]), axis=0, keepdims=True)
    zero = jnp.zeros_like(w1)
    route_ref[0, 0] = jnp.concatenate(
        [i1.astype(F32), i2.astype(F32), w1, w2, pos1, pos2, zero, zero], axis=0)
    cnt_ref[0, 0] = jnp.concatenate([size, start], axis=0)


def _route(x, mod, ln, router_t):
    b, t, d = x.shape
    tm = _row_tile(t, 512)
    nt = t // tm
    ne = router_t.shape[0]
    return pl.pallas_call(
        _route_kernel,
        grid=(b, nt),
        in_specs=[pl.BlockSpec((1, tm, d), lambda i, j: (i, j, 0)),
                  pl.BlockSpec((1, N_MOD, d), lambda i, j: (i, 0, 0)),
                  _const_spec(ln.shape), _const_spec(router_t.shape)],
        out_specs=[pl.BlockSpec((1, tm, d), lambda i, j: (i, j, 0)),
                   pl.BlockSpec((1, 1, 8, tm), lambda i, j: (i, j, 0, 0)),
                   pl.BlockSpec((1, 1, 2 * ne, LANES), lambda i, j: (i, j, 0, 0))],
        out_shape=[jax.ShapeDtypeStruct((b, t, d), F32),
                   jax.ShapeDtypeStruct((b, nt, 8, tm), F32),
                   jax.ShapeDtypeStruct((b, nt, 2 * ne, LANES), F32)],
        compiler_params=_params(("parallel", "parallel")),
        name="moe_route",
    )(x, mod, ln, router_t)


def _row_copy(src, dst, sem):
    return pltpu.make_async_copy(src, dst, sem)


def _sorted_rows(tm):
    return 2 * tm + LANES


def _group_copies(tab_ref, ne, make_copy, issue):
    per = GROUP_CHUNK // GROUP_ALIGN
    for e in range(ne):
        local, units, glob = tab_ref[0, 0, e], tab_ref[0, 0, ne + e], tab_ref[0, 0, 2 * ne + e]
        big = units // per

        def chunk(c, _, rows, first):
            off = pl.multiple_of(first + c * rows, GROUP_ALIGN)
            cp = make_copy(pl.ds(pl.multiple_of(local + off, GROUP_ALIGN), rows),
                           pl.ds(pl.multiple_of(glob + off, GROUP_ALIGN), rows))
            cp.start() if issue else cp.wait()
            return 0

        lax.fori_loop(0, big, functools.partial(chunk, rows=GROUP_CHUNK, first=0), 0)
        lax.fori_loop(0, units - big * per,
                      functools.partial(chunk, rows=GROUP_ALIGN, first=big * GROUP_CHUNK), 0)


def _dispatch_kernel(tab_ref, prev_tab_ref, route_ref, h_ref, rows_in_ref, rows_ref, sorted_ref, sem):
    del rows_in_ref
    i = pl.program_id(0)
    slot = i % 2
    ne = tab_ref.shape[2] // 3
    make = lambda s: (lambda loc, glob: _row_copy(sorted_ref.at[s, loc], rows_ref.at[glob],
                                                  sem.at[s]))
    pos = route_ref[0, 4:6, :].astype(jnp.int32)
    j = lax.broadcasted_iota(jnp.int32, (sorted_ref.shape[1], h_ref.shape[0]), 0)
    pick = jnp.where((j == pos[0:1, :]) | (j == pos[1:2, :]), 1.0, 0.0).astype(BF16)
    sorted_ref[slot] = jnp.dot(pick, h_ref[...].astype(BF16), preferred_element_type=F32)
    _group_copies(tab_ref, ne, make(slot), True)

    @pl.when(i > 0)
    def _():
        _group_copies(prev_tab_ref, ne, make(1 - slot), False)

    @pl.when(i == pl.num_programs(0) - 1)
    def _():
        _group_copies(tab_ref, ne, make(slot), False)


def _dispatch(h, route, table, n_rows):
    n, d = h.shape
    tm = route.shape[2]
    rows = jnp.zeros((n_rows, d), h.dtype)
    return pl.pallas_call(
        _dispatch_kernel,
        grid=(n // tm,),
        in_specs=[pl.BlockSpec((1, 1, table.shape[2]), lambda i: (i, 0, 0),
                               memory_space=pltpu.SMEM),
                  pl.BlockSpec((1, 1, table.shape[2]), lambda i: (jnp.maximum(i - 1, 0), 0, 0),
                               memory_space=pltpu.SMEM),
                  pl.BlockSpec((1, 8, tm), lambda i: (i, 0, 0)),
                  pl.BlockSpec((tm, d), lambda i: (i, 0)),
                  pl.BlockSpec(memory_space=pl.ANY)],
        out_specs=pl.BlockSpec(memory_space=pl.ANY),
        out_shape=jax.ShapeDtypeStruct((n_rows, d), h.dtype),
        scratch_shapes=[pltpu.VMEM((2, _sorted_rows(tm), d), F32), pltpu.SemaphoreType.DMA((2,))],
        input_output_aliases={4: 0},
        compiler_params=pltpu.CompilerParams(dimension_semantics=("arbitrary",),
                                             vmem_limit_bytes=VMEM_LIMIT, has_side_effects=True),
        name="moe_dispatch",
    )(table, table, route, h, rows)


def _experts_kernel(be_ref, nb_ref, x_ref, wg_ref, wu_ref, wd_ref, y_ref):
    del be_ref

    @pl.when(pl.program_id(0) < nb_ref[0])
    def _():
        x = x_ref[...].astype(BF16)
        f = wg_ref.shape[2]
        step = f // 7
        acc = None
        for c0 in range(0, f, step):
            gate = jnp.dot(x, wg_ref[0, :, c0:c0 + step], preferred_element_type=F32)
            up = jnp.dot(x, wu_ref[0, :, c0:c0 + step], preferred_element_type=F32)
            part = _dot(_silu(gate) * up, wd_ref[0, c0:c0 + step, :])
            acc = part if acc is None else acc + part
        y_ref[...] = acc

    @pl.when(pl.program_id(0) >= nb_ref[0])
    def _():
        y_ref[...] = jnp.zeros_like(y_ref)


def _experts(rows, block_e, n_used, wg, wu, wd):
    n_rows, d = rows.shape
    rb = MOE_ROW_BLOCK
    f = wg.shape[2]
    wspec = lambda shape: pl.BlockSpec((1,) + shape, lambda i, be, nb: (be[i], 0, 0),
                                       pipeline_mode=pl.Buffered(1))
    return pl.pallas_call(
        _experts_kernel,
        grid_spec=pltpu.PrefetchScalarGridSpec(
            num_scalar_prefetch=2,
            grid=(n_rows // rb,),
            in_specs=[pl.BlockSpec((rb, d), lambda i, be, nb: (i, 0)),
                      wspec((d, f)), wspec((d, f)), wspec((f, d))],
            out_specs=pl.BlockSpec((rb, d), lambda i, be, nb: (i, 0))),
        out_shape=jax.ShapeDtypeStruct((n_rows, d), F32),
        compiler_params=_params(("arbitrary",)),
        name="moe_experts",
    )(block_e, n_used, rows, wg, wu, wd)


def _combine_kernel(tab_ref, next_tab_ref, route_ref, y_hbm, x_ref, mod_ref, lnf_ref, out_ref,
                    sorted_ref, sem, *, final_norm):
    step = pl.program_id(0) * pl.num_programs(1) + pl.program_id(1)
    slot = step % 2
    ne = tab_ref.shape[2] // 3
    make = lambda s: (lambda loc, glob: _row_copy(y_hbm.at[glob], sorted_ref.at[s, loc],
                                                  sem.at[s]))

    @pl.when(step == 0)
    def _():
        sorted_ref[...] = jnp.zeros_like(sorted_ref)
        _group_copies(tab_ref, ne, make(slot), True)

    @pl.when(step + 1 < pl.num_programs(0) * pl.num_programs(1))
    def _():
        _group_copies(next_tab_ref, ne, make(1 - slot), True)

    _group_copies(tab_ref, ne, make(slot), False)
    pos = route_ref[0, 4:6, :].astype(jnp.int32)
    wts = route_ref[0, 2:4, :]
    j = lax.broadcasted_iota(jnp.int32, (sorted_ref.shape[1], x_ref.shape[1]), 0)
    q_t = (jnp.where(j == pos[0:1, :], wts[0:1, :], 0.0)
           + jnp.where(j == pos[1:2, :], wts[1:2, :], 0.0))
    q_hi, q_lo = _split_bf16(q_t, 2)
    z = sorted_ref[slot].astype(BF16)
    f = _dot_tn(q_hi, z) + _dot_tn(q_lo, z)
    out = x_ref[0] + mod_ref[0, 5:6, :] * f
    if final_norm:
        out = _norm_mod(out, lnf_ref[...], 0.0, 0.0)
    out_ref[0] = out


def _combine(y_rows, route, table, x, mod, ln_final, final_norm):
    b, t, d = x.shape
    tm = route.shape[2]
    nt = t // tm
    return pl.pallas_call(
        functools.partial(_combine_kernel, final_norm=final_norm),
        grid=(b, nt),
        in_specs=[pl.BlockSpec((1, 1, table.shape[2]), lambda i, j: (i * nt + j, 0, 0),
                               memory_space=pltpu.SMEM),
                  pl.BlockSpec((1, 1, table.shape[2]),
                               lambda i, j: (jnp.minimum(i * nt + j + 1, b * nt - 1), 0, 0),
                               memory_space=pltpu.SMEM),
                  pl.BlockSpec((1, 8, tm), lambda i, j: (i * nt + j, 0, 0)),
                  pl.BlockSpec(memory_space=pl.ANY),
                  pl.BlockSpec((1, tm, d), lambda i, j: (i, j, 0)),
                  pl.BlockSpec((1, N_MOD, d), lambda i, j: (i, 0, 0)),
                  _const_spec(ln_final.shape)],
        out_specs=pl.BlockSpec((1, tm, d), lambda i, j: (i, j, 0)),
        out_shape=jax.ShapeDtypeStruct((b, t, d), F32),
        scratch_shapes=[pltpu.VMEM((2, _sorted_rows(tm), d), F32), pltpu.SemaphoreType.DMA((2,))],
        compiler_params=_params(("arbitrary", "arbitrary")),
        name="moe_combine",
    )(table, table, route, y_rows, x, mod, ln_final)


def _ffn_moe(x, mod, ln, ln_final, router_t, wg, wu, wd, final_norm):
    b, t, d = x.shape
    n = b * t
    ne = router_t.shape[0]
    rb = MOE_ROW_BLOCK
    h, route, cnt = _route(x, mod, ln, router_t)
    nt, tm = route.shape[1], route.shape[3]
    tiles = b * nt
    cnt = cnt[..., 0].reshape(tiles, 2 * ne).astype(jnp.int32)
    size, local = cnt[:, :ne], cnt[:, ne:]
    totals = jnp.sum(size, axis=0)
    padded = ((totals + rb - 1) // rb) * rb
    padded_end = jnp.cumsum(padded)
    base = (padded_end - padded)[None, :] + jnp.cumsum(size, axis=0) - size
    table = jnp.concatenate([local, size // GROUP_ALIGN, base], axis=1).reshape(tiles, 1, 3 * ne)
    route = route.reshape(tiles, 8, tm)
    n_blocks = -(-(n * 2 + tiles * ne * (GROUP_ALIGN - 1)) // rb) + ne
    block_e = jnp.minimum(
        jnp.searchsorted(padded_end, jnp.arange(n_blocks, dtype=jnp.int32) * rb, side="right"),
        ne - 1).astype(jnp.int32)
    n_used = (padded_end[-1:] // rb).astype(jnp.int32)
    rows = _dispatch(h.reshape(n, d), route, table, n_blocks * rb)
    y_rows = _experts(rows, block_e, n_used, wg, wu, wd)
    return _combine(y_rows, route, table, x, mod, ln_final, final_norm)


def _block_diag(m, n):
    return jnp.kron(jnp.eye(n, dtype=m.dtype), m)


def _layer_params(i, w_in, conv_w, decay_w0, decay_w2, iclr_a0, iclr_a2, gate_g2, k_k, k_a, r_k,
                  lnx_w, lnx_b, w_fnet_out, w_rwkv_out, w_o):
    w = RWKV_WIDTH
    lora = decay_w2.shape[2]
    tail0 = FNET_WIDTH + 3 * w

    def reorder_tail(m):
        pad = jnp.zeros(m.shape[:-1] + (LANES - lora,), m.dtype)
        return jnp.concatenate([m[..., :3 * lora], pad, m[..., 3 * lora:]], axis=-1)

    row = lambda v: v.reshape(1, -1)
    hi_lo = lambda m: jnp.stack(_split_bf16(m, 2))
    wi = w_in[i]
    rwkv_cols = conv_w.shape[2]
    zeros = jnp.zeros((lora, w), F32)
    return {
        "w_a": wi[:, :FNET_WIDTH].astype(BF16),
        "w_main": wi[:, FNET_WIDTH:tail0].astype(BF16),
        "w_tail": reorder_tail(wi[:, tail0:FNET_WIDTH + rwkv_cols]).astype(BF16),
        "w_gates": wi[:, FNET_WIDTH + rwkv_cols:].astype(BF16),
        "conv_main": conv_w[i][:, :3 * w],
        "conv_tail": reorder_tail(conv_w[i][:, 3 * w:]),
        "decay_w0": decay_w0[i].reshape(1, 2 * w),
        "decay_w2": hi_lo(jnp.concatenate(
            [jnp.concatenate([decay_w2[i, 0], zeros], axis=1),
             jnp.concatenate([zeros, decay_w2[i, 1]], axis=1)], axis=0)),
        "iclr_a0": row(iclr_a0[i]),
        "iclr_a2": hi_lo(jnp.concatenate([iclr_a2[i], jnp.zeros((LANES - lora, w), F32)], axis=0)),
        "gate_g2": hi_lo(gate_g2[i]),
        "k_k": row(k_k[i]), "k_a": row(k_a[i]), "r_k": row(r_k[i]),
        "lnx_w": row(lnx_w[i]), "lnx_b": row(lnx_b[i]),
        "head_sum": _block_diag(jnp.ones((HEAD_DIM, HEAD_DIM), BF16), LANES // HEAD_DIM),
        "head_mean": _block_diag(jnp.full((HEAD_DIM, HEAD_DIM), 1.0 / HEAD_DIM, BF16),
                                 LANES // HEAD_DIM),
        "w_fnet_out": w_fnet_out[i].astype(BF16),
        "w_rwkv_out": w_rwkv_out[i].astype(BF16),
        "w_o": w_o[i].astype(BF16),
    }


def _channel_dft():
    k = jnp.arange(HEAD_DIM, dtype=jnp.int32)
    ang = ((k[:, None] * k[None, :]) % HEAD_DIM).astype(F32) * (2.0 * math.pi / HEAD_DIM)
    groups = FNET_WIDTH // HEAD_DIM
    return jnp.concatenate([_block_diag(jnp.cos(ang), groups), _block_diag(jnp.sin(ang), groups)],
                           axis=1).astype(BF16)


def _token_mixer(x, mod, ln, p, cs, dft_tab):
    zc, zs, um, ut, gates = _inproj(x, mod, ln, p["w_a"], p["w_main"], p["w_tail"], p["w_gates"], cs)
    yf = _fnet(zc, zs, dft_tab)
    r, k, v, a, b, e_f, e_b, g, bonus = _prep(um, ut, p)
    o = _wkv(r, k, v, a, b, e_b, True, add=_wkv(r, k, v, a, b, e_f, False))
    return _mixout(o, bonus, g, yf, gates, x, mod, p)


def kernel(x_prompt, x_sample, c_prompt, c_sample, w_ada, b_ada, ln_mix, w_in, conv_w, decay_w0, decay_w2, iclr_a0, iclr_a2, gate_g2, k_k, k_a, r_k, lnx_w, lnx_b, w_fnet_out, w_rwkv_out, w_o, ln_ffn, ff_w_gate, ff_w_up, ff_w_down, router_w, moe_w_gate, moe_w_up, moe_w_down, ln_final):
    depth, d = ln_mix.shape
    streams = [x_prompt, x_sample]
    nb = [x.shape[0] for x in streams]
    mod_all = _ada(jnp.concatenate([c_prompt, c_sample], axis=0), w_ada, b_ada)
    cs = _channel_dft()
    dfts = {}
    for x in streams:
        if x.shape[1] not in dfts:
            dfts[x.shape[1]] = _dft_tables(x.shape[1])
    lnf = ln_final.reshape(1, d)
    for i in range(depth):
        p = _layer_params(i, w_in, conv_w, decay_w0, decay_w2, iclr_a0, iclr_a2, gate_g2, k_k, k_a,
                          r_k, lnx_w, lnx_b, w_fnet_out, w_rwkv_out, w_o)
        j = i // 2
        final = i == depth - 1
        if i % 2 == 0:
            ffw = (ff_w_gate[j].astype(BF16), ff_w_up[j].astype(BF16), ff_w_down[j].astype(BF16))
        else:
            ffw = (moe_w_gate[j].astype(BF16), moe_w_up[j].astype(BF16), moe_w_down[j].astype(BF16))
            router_t = router_w[j].T
        off = 0
        for s, x in enumerate(streams):
            mod = mod_all[i, off:off + nb[s]].reshape(nb[s], N_MOD, d)
            off += nb[s]
            x = _token_mixer(x, mod, ln_mix[i].reshape(1, d), p, cs, dfts[x.shape[1]])
            ln2 = ln_ffn[i].reshape(1, d)
            if i % 2 == 0:
                x = _ffn_dense(x, mod, ln2, lnf, *ffw, final)
            else:
                x = _ffn_moe(x, mod, ln2, lnf, router_t, *ffw, final)
            streams[s] = x
    return tuple(streams)
```

```python
import functools
import math

import jax
import jax.numpy as jnp
from jax import lax
from jax.experimental import pallas as pl
from jax.experimental.pallas import tpu as pltpu

F32 = jnp.float32
BF16 = jnp.bfloat16
HIGHEST = lax.Precision.HIGHEST

HEAD_DIM = 64
LANES = 128
FNET_WIDTH = 256
RWKV_WIDTH = 768
LORA_TAIL = 384
N_MOD = 6
N_EXPERTS = 8
RMS_EPS = 1e-6
GN_EPS = 64e-5
WKV_CHUNK = 128
MOE_ROW_BLOCK = 512
GROUP_ALIGN = 8
GROUP_CHUNK = 64
VMEM_LIMIT = 56 * 1024 * 1024


def _params(sem, vmem=VMEM_LIMIT):
    return pltpu.CompilerParams(dimension_semantics=sem, vmem_limit_bytes=vmem)


def _const_spec(shape):
    nd = len(shape)
    return pl.BlockSpec(shape, lambda *_: (0,) * nd, pipeline_mode=pl.Buffered(1))


def _dot(a, b):
    return jnp.dot(a.astype(BF16), b.astype(BF16), preferred_element_type=F32)


def _dot_nt(a, b):
    return lax.dot_general(a.astype(BF16), b.astype(BF16), (((1,), (1,)), ((), ())),
                           preferred_element_type=F32)


def _dot_tn(a, b):
    return lax.dot_general(a.astype(BF16), b.astype(BF16), (((0,), (0,)), ((), ())),
                           preferred_element_type=F32)


def _split_bf16(x, parts):
    out, rem = [], x
    for _ in range(parts):
        p = rem.astype(BF16)
        out.append(p)
        rem = rem - p.astype(F32)
    return out


def _dot_exact_rhs(x, m, parts):
    acc = None
    for p in _split_bf16(x, parts):
        t = jnp.dot(p, m, preferred_element_type=F32)
        acc = t if acc is None else acc + t
    return acc


def _dot_f32(a, b):
    return jnp.dot(a, b, precision=HIGHEST, preferred_element_type=F32)


def _dot_3pass(a, w_ref):
    ah, al = _split_bf16(a, 2)
    d = lambda x, y: jnp.dot(x, y, preferred_element_type=F32)
    return d(ah, w_ref[0]) + (d(al, w_ref[0]) + d(ah, w_ref[1]))


def _norm_mod(x, gain, shift, scale):
    ms = jnp.mean(x * x, axis=-1, keepdims=True)
    return x * lax.rsqrt(ms + RMS_EPS) * gain * (1.0 + scale) + shift


def _silu(x):
    return x * jax.nn.sigmoid(x)


def _head_reduce(x, m):
    cols = [_dot_exact_rhs(x[:, j:j + LANES], m, 2) for j in range(0, x.shape[1], LANES)]
    return jnp.concatenate(cols, axis=1)


def _row_tile(t, want):
    tm = min(t, want)
    assert t % tm == 0
    return tm


def _sub_blocks(rows, size=512):
    size = min(size, rows)
    assert rows % size == 0
    return [slice(r, r + size) for r in range(0, rows, size)]


def _col_chunks(width, size):
    return [slice(c, min(c + size, width)) for c in range(0, width, size)]


def _ada_kernel(c_ref, w_ref, b_ref, o_ref):
    o_ref[0] = _dot_f32(_silu(c_ref[...]), w_ref[0]) + b_ref[0]


def _ada(c_all, w_ada, b_ada):
    depth, d, n = w_ada.shape
    bc = c_all.shape[0]
    tn = n // 4
    return pl.pallas_call(
        _ada_kernel,
        grid=(depth, n // tn),
        in_specs=[pl.BlockSpec((bc, d), lambda l, j: (0, 0)),
                  pl.BlockSpec((1, d, tn), lambda l, j: (l, 0, j)),
                  pl.BlockSpec((1, 1, tn), lambda l, j: (l, 0, j))],
        out_specs=pl.BlockSpec((1, bc, tn), lambda l, j: (l, 0, j)),
        out_shape=jax.ShapeDtypeStruct((depth, bc, n), F32),
        compiler_params=_params(("parallel", "parallel")),
        name="ada",
    )(c_all, w_ada, b_ada.reshape(depth, 1, n))


def _inproj_kernel(x_ref, mod_ref, ln_ref, wa_ref, wm_ref, wt_ref, wg_ref, cs_ref,
                   zc_ref, zs_ref, um_ref, ut_ref, g_ref):
    for rows in _sub_blocks(x_ref.shape[1]):
        h = _norm_mod(x_ref[0, rows, :], ln_ref[...], mod_ref[0, 0:1, :],
                      mod_ref[0, 1:2, :]).astype(BF16)
        ua = jnp.dot(h, wa_ref[...], preferred_element_type=F32)
        zz = jnp.dot(ua.astype(BF16), cs_ref[...], preferred_element_type=F32)
        zc_ref[0, rows, :] = zz[:, :FNET_WIDTH]
        zs_ref[0, rows, :] = zz[:, FNET_WIDTH:]
        for cols in _col_chunks(wm_ref.shape[1], 768):
            um_ref[0, rows, cols] = jnp.dot(
                h, wm_ref[:, cols], preferred_element_type=F32).astype(BF16)
        ut_ref[0, rows, :] = jnp.dot(h, wt_ref[...], preferred_element_type=F32)
        for cols in _col_chunks(wg_ref.shape[1], 512):
            g_ref[0, rows, cols] = jax.nn.sigmoid(
                jnp.dot(h, wg_ref[:, cols], preferred_element_type=F32)).astype(BF16)


def _inproj(x, mod, ln, wa, wm, wt, wg, cs):
    b, t, d = x.shape
    tm = _row_tile(t, 1024)
    row = lambda w: pl.BlockSpec((1, tm, w), lambda i, j: (i, j, 0))
    outs = [(FNET_WIDTH, F32), (FNET_WIDTH, F32), (wm.shape[1], BF16), (wt.shape[1], F32),
            (wg.shape[1], BF16)]
    return pl.pallas_call(
        _inproj_kernel,
        grid=(b, t // tm),
        in_specs=[row(d), pl.BlockSpec((1, N_MOD, d), lambda i, j: (i, 0, 0)),
                  _const_spec(ln.shape), _const_spec(wa.shape), _const_spec(wm.shape),
                  _const_spec(wt.shape), _const_spec(wg.shape), _const_spec(cs.shape)],
        out_specs=[row(w) for w, _ in outs],
        out_shape=[jax.ShapeDtypeStruct((b, t, w), dt) for w, dt in outs],
        compiler_params=_params(("parallel", "parallel")),
        name="inproj",
    )(x, mod, ln, wa, wm, wt, wg, cs)


FNET_T2 = 64
FNET_ROWS = 16


def _rows_at(ref, j):
    return ref[0, :, j:j + 1, :].reshape(ref.shape[1], ref.shape[3])


def _fnet_stage1_kernel(d_ref, zc_ref, zs_ref, twc_ref, tws_ref, gr_ref, gi_ref):
    t2, w = zc_ref.shape[1], zc_ref.shape[3]
    for j in range(zc_ref.shape[2]):
        data = jnp.concatenate([_rows_at(zc_ref, j), _rows_at(zs_ref, j)], axis=0)
        g = _dot(d_ref[...], data)
        g_r, g_i = g[:t2], g[t2:]
        c = jnp.concatenate([twc_ref[j]] * (w // LANES), axis=1)
        s = jnp.concatenate([tws_ref[j]] * (w // LANES), axis=1)
        gr_ref[0, j] = g_r * c + g_i * s
        gi_ref[0, j] = g_i * c - g_r * s


def _fnet_stage2_kernel(d_ref, gr_ref, gi_ref, o_ref, *, scale):
    t1, w = gr_ref.shape[1], gr_ref.shape[3]
    for j in range(gr_ref.shape[2]):
        data = jnp.concatenate([_rows_at(gr_ref, j), _rows_at(gi_ref, j)], axis=0)
        o_ref[0, :, j:j + 1, :] = (_dot(d_ref[...], data) * scale).reshape(t1, 1, w)


def _dft_tables(t):
    t2 = FNET_T2
    t1 = t // t2
    assert t1 * t2 == t

    def cos_sin(rows, cols, n):
        k = (lax.iota(jnp.int32, rows)[:, None] * lax.iota(jnp.int32, cols)[None, :]) % n
        ang = k.astype(F32) * (2.0 * math.pi / n)
        return jnp.cos(ang), jnp.sin(ang)

    c2, s2 = cos_sin(t2, t2, t2)
    c1, s1 = cos_sin(t1, t1, t1)
    twc, tws = cos_sin(t1, t2, t)
    lanes = lambda m: jnp.broadcast_to(m[:, :, None], (t1, t2, LANES))
    return {
        "stage1": jnp.block([[c2, -s2], [-s2, -c2]]).astype(BF16),
        "stage2": jnp.concatenate([c1, s1], axis=1).astype(BF16),
        "twc": lanes(twc), "tws": lanes(tws),
    }


def _fnet(zc, zs, tab):
    b, t, w = zc.shape
    t2 = FNET_T2
    t1 = t // t2
    n1 = min(FNET_ROWS, t1)
    in_spec = pl.BlockSpec((1, t2, n1, w), lambda i, j: (i, 0, j, 0))
    tw_spec = pl.BlockSpec((n1, t2, LANES), lambda i, j: (j, 0, 0))
    g_shape = jax.ShapeDtypeStruct((b, t1, t2, w), F32)
    g_r, g_i = pl.pallas_call(
        _fnet_stage1_kernel,
        grid=(b, t1 // n1),
        in_specs=[_const_spec(tab["stage1"].shape), in_spec, in_spec, tw_spec, tw_spec],
        out_specs=[pl.BlockSpec((1, n1, t2, w), lambda i, j: (i, j, 0, 0))] * 2,
        out_shape=[g_shape, g_shape],
        compiler_params=_params(("parallel", "parallel")),
        name="fnet_stage1",
    )(tab["stage1"], zc.reshape(b, t2, t1, w), zs.reshape(b, t2, t1, w), tab["twc"], tab["tws"])
    n2 = min(FNET_ROWS, t2)
    spec = pl.BlockSpec((1, t1, n2, w), lambda i, j: (i, 0, j, 0))
    y = pl.pallas_call(
        functools.partial(_fnet_stage2_kernel, scale=1.0 / math.sqrt(t * HEAD_DIM)),
        grid=(b, t2 // n2),
        in_specs=[_const_spec(tab["stage2"].shape), spec, spec],
        out_specs=spec,
        out_shape=jax.ShapeDtypeStruct((b, t1, t2, w), F32),
        compiler_params=_params(("parallel", "parallel")),
        name="fnet_stage2",
    )(tab["stage2"], g_r, g_i)
    return y.reshape(b, t, w)


def _conv3(main, prev_row, next_row, w):
    tm = main.shape[0]
    ri = lax.broadcasted_iota(jnp.int32, main.shape, 0)
    up = jnp.where(ri == 0, prev_row, pltpu.roll(main, 1, 0))
    dn = jnp.where(ri == tm - 1, next_row, pltpu.roll(main, tm - 1, 0))
    return w[0:1] * up + w[1:2] * main + w[2:3] * dn


def _prep_kernel(um_ref, ump_ref, umn_ref, ut_ref, utp_ref, utn_ref, cwm_ref, cwt_ref,
                 w0_ref, w2_ref, a0_ref, a2_ref, g2_ref, kk_ref, ka_ref, rk_ref, hs_ref, shift_ref,
                 r_ref, k_ref, v_ref, a_ref, b_ref, ef_ref, eb_ref, g_ref, bonus_ref):
    j = pl.program_id(1)
    first = j == 0
    last = j == pl.num_programs(1) - 1
    hp = ump_ref.shape[1]

    def conv_main(c0, c1):
        main = um_ref[0, :, c0:c1]
        tm = main.shape[0]
        shifted = jnp.dot(shift_ref[...], main, preferred_element_type=F32)
        ri = lax.broadcasted_iota(jnp.int32, main.shape, 0)
        prev = jnp.where(first, 0.0, ump_ref[0, hp - 1:hp, c0:c1].astype(F32))
        nxt = jnp.where(last, 0.0, umn_ref[0, 0:1, c0:c1].astype(F32))
        up = jnp.where(ri == 0, prev, shifted[:tm])
        dn = jnp.where(ri == tm - 1, nxt, shifted[tm:])
        w = cwm_ref[:, c0:c1]
        return w[0:1] * up + w[1:2] * main.astype(F32) + w[2:3] * dn

    w = RWKV_WIDTH
    r = conv_main(0, w)
    k = conv_main(w, 2 * w)
    v = conv_main(2 * w, 3 * w)
    tp = utp_ref.shape[1]
    tail = _conv3(ut_ref[0], jnp.where(first, 0.0, utp_ref[0, tp - 1:tp, :]),
                  jnp.where(last, 0.0, utn_ref[0, 0:1, :]), cwt_ref[...])
    xw, xa, xg = tail[:, 0:LANES], tail[:, LANES:2 * LANES], tail[:, 2 * LANES:3 * LANES]

    e = jax.nn.sigmoid(w0_ref[...] + _dot_3pass(jnp.tanh(xw), w2_ref)) * math.exp(-0.5)
    ef_ref[0] = e[:, :w]
    eb_ref[0] = e[:, w:]
    a = jax.nn.sigmoid(a0_ref[...] + _dot_3pass(xa, a2_ref))
    g_ref[0] = _dot_3pass(jax.nn.sigmoid(xg), g2_ref).astype(BF16)

    kk = k * kk_ref[...]
    kk = kk * lax.rsqrt(_head_reduce(kk * kk, hs_ref[...]) + 1e-12)
    k = k * (1.0 + (a - 1.0) * ka_ref[...])
    r_ref[0] = r.astype(BF16)
    k_ref[0] = k.astype(BF16)
    v_ref[0] = v.astype(BF16)
    a_ref[0] = (-kk).astype(BF16)
    b_ref[0] = (kk * a).astype(BF16)
    bonus_ref[0] = (_head_reduce(r * k * rk_ref[...], hs_ref[...]) * v).astype(BF16)


def _prep(um, ut, p):
    b, t, wm = um.shape
    wt = ut.shape[2]
    tm = _row_tile(t, 256)
    hm, ht = 16, 8
    nm, nt = tm // hm, tm // ht
    main = lambda w: pl.BlockSpec((1, tm, w), lambda i, j: (i, j, 0))
    prev = lambda h, n, w: pl.BlockSpec((1, h, w), lambda i, j: (i, jnp.maximum(j * n - 1, 0), 0))
    nxt = lambda h, n, w: pl.BlockSpec(
        (1, h, w), lambda i, j: (i, jnp.minimum((j + 1) * n, t // h - 1), 0))
    consts = [p["conv_main"], p["conv_tail"], p["decay_w0"], p["decay_w2"], p["iclr_a0"],
              p["iclr_a2"], p["gate_g2"], p["k_k"], p["k_a"], p["r_k"], p["head_sum"],
              jnp.concatenate([jnp.eye(tm, k=-1, dtype=BF16), jnp.eye(tm, k=1, dtype=BF16)], axis=0)]
    outs = [BF16] * 5 + [F32, F32, BF16, BF16]
    return pl.pallas_call(
        _prep_kernel,
        grid=(b, t // tm),
        in_specs=[main(wm), prev(hm, nm, wm), nxt(hm, nm, wm),
                  main(wt), prev(ht, nt, wt), nxt(ht, nt, wt)]
                 + [_const_spec(c.shape) for c in consts],
        out_specs=[main(RWKV_WIDTH) for _ in outs],
        out_shape=[jax.ShapeDtypeStruct((b, t, RWKV_WIDTH), dt) for dt in outs],
        compiler_params=_params(("parallel", "parallel")),
        name="rwkv_prep",
    )(um, um, um, ut, ut, ut, *consts)


def _wkv_kernel(r_ref, k_ref, v_ref, a_ref, b_ref, e_ref, *rest, reverse):
    *add_refs, o_ref, s_ref = rest
    add_ref = add_refs[0] if add_refs else None
    c = r_ref.shape[1]
    pairs = range(s_ref.shape[0])
    heads = range(LANES // HEAD_DIM)
    chains = [(p, h) for p in pairs for h in heads]

    @pl.when(pl.program_id(1) == 0)
    def _():
        s_ref[...] = jnp.zeros_like(s_ref)

    row = lax.broadcasted_iota(jnp.int32, (c, c), 0)
    col = lax.broadcasted_iota(jnp.int32, (c, c), 1)
    if reverse:
        incl, strict = col >= row, col > row
    else:
        incl, strict = col <= row, col < row
    tri = jnp.where(incl, 1.0, 0.0).astype(BF16)
    end = 0 if reverse else c - 1
    lane = lax.broadcasted_iota(jnp.int32, (1, LANES), 1)
    in_head = [(lane >= h * HEAD_DIM) & (lane < (h + 1) * HEAD_DIM) for h in heads]
    same_head = (lax.broadcasted_iota(jnp.int32, (LANES, LANES), 0) // HEAD_DIM
                 == lax.broadcasted_iota(jnp.int32, (LANES, LANES), 1) // HEAD_DIM)

    per_row = r_ref.shape[2] // LANES

    def tile(ref, p):
        return ref[p // per_row, :, (p % per_row) * LANES:(p % per_row + 1) * LANES]

    tri2 = jnp.concatenate([tri, tri], axis=1)

    def running_sum(e):
        return jnp.dot(tri2, jnp.concatenate(_split_bf16(e, 2), axis=0), preferred_element_type=F32)

    cum = [running_sum(tile(e_ref, p)) for p in pairs]
    total = [x[end:end + 1, :] for x in cum]
    ref = [x[c // 2:c // 2 + 1, :] for x in cum]
    dl = [cum[p] - ref[p] for p in pairs]
    g_inv = [jnp.exp(dl[p]) for p in pairs]
    at = [tile(a_ref, p).astype(F32) * jnp.exp(tile(e_ref, p) - dl[p]) for p in pairs]
    rt = [tile(r_ref, p).astype(F32) * jnp.exp(-dl[p]) for p in pairs]
    bt = [tile(b_ref, p).astype(F32) * g_inv[p] for p in pairs]
    kt = [tile(k_ref, p).astype(F32) * g_inv[p] for p in pairs]
    g_end = [jnp.exp(ref[p] - total[p]) for p in pairs]
    bh = [(bt[p] * g_end[p]).astype(BF16) for p in pairs]
    kh = [(kt[p] * g_end[p]).astype(BF16) for p in pairs]
    bk = [jnp.concatenate([bt[p], kt[p]], axis=0).astype(BF16) for p in pairs]
    ar = [jnp.concatenate([at[p], rt[p]], axis=0) for p in pairs]
    vb = [tile(v_ref, p) for p in pairs]

    prod = [_dot_nt(jnp.where(in_head[h], ar[p], 0.0), bk[p]) for p, h in chains]
    l_ab = [jnp.where(strict, x[:c, :c], 0.0) for x in prod]
    l_ak = [jnp.where(strict, x[:c, c:], 0.0) for x in prod]
    m_rb = [jnp.where(incl, x[c:, :c], 0.0).astype(BF16) for x in prod]
    m_rk = [jnp.where(incl, x[c:, c:], 0.0).astype(BF16) for x in prod]
    del prod
    eye = jnp.where(row == col, 1.0, 0.0)
    t = [eye + jnp.where((row >> 1) == (col >> 1), x, 0.0) for x in l_ab]
    nch = range(len(chains))
    for sh in range(1, (c - 1).bit_length()):
        m = 1 << sh
        joins = ((row >> (sh + 1)) == (col >> (sh + 1))) & ((row >> sh) != (col >> sh))
        e_k = [jnp.where(joins, l_ab[i], 0.0) for i in nch]
        if m < 8:
            et = [_dot(e_k[i], t[i]) for i in nch]
            t = [t[i] + _dot(t[i], et[i]) for i in nch]
            continue
        live = [slice(b0 + (0 if reverse else m), b0 + (m if reverse else 2 * m))
                for b0 in range(0, c, 2 * m)]

        def take(v):
            return jnp.concatenate([v[rs] for rs in live], axis=0)

        def spread(v):
            zero = jnp.zeros((m, v.shape[1]), v.dtype)
            parts = []
            for j in range(len(live)):
                blk = v[j * m:(j + 1) * m]
                parts += [blk, zero] if reverse else [zero, blk]
            return jnp.concatenate(parts, axis=0)

        et = [_dot(take(e_k[i]), t[i]) for i in nch]
        upd = [_dot(take(t[i]), spread(et[i])) for i in nch]
        t = [t[i] + spread(upd[i]) for i in nch]
    lakv = [jnp.dot(l_ak[i].astype(BF16), vb[p], preferred_element_type=F32)
            for i, (p, h) in enumerate(chains)]
    x = [_dot(t[i], jnp.concatenate([at[p], lakv[i]], axis=1)) for i, (p, h) in enumerate(chains)]
    x_a = [v[:, :LANES] for v in x]
    x_u = [v[:, LANES:] for v in x]
    y_a = [jnp.dot(m_rb[i], x_a[i].astype(BF16), preferred_element_type=F32) for i in nch]
    y_o = [jnp.dot(jnp.concatenate([m_rb[i], m_rk[i]], axis=1),
                   jnp.concatenate([x_u[i].astype(BF16), vb[p]], axis=0),
                   preferred_element_type=F32) for i, (p, h) in enumerate(chains)]

    def pick(vals, p):
        out = vals[p * len(heads)]
        for h in heads[1:]:
            out = jnp.where(in_head[h], vals[p * len(heads) + h], out)
        return out

    a_solved = [pick(x_a, p) for p in pairs]
    u = [pick(x_u, p) for p in pairs]
    r_hat = [rt[p] + pick(y_a, p) for p in pairs]
    o_intra = [pick(y_o, p) for p in pairs]
    s = [s_ref[p] for p in pairs]
    s_hat = [(s[p] * jnp.exp(-ref[p])).astype(BF16) for p in pairs]
    out = [_dot_nt(r_hat[p], s_hat[p]) + o_intra[p] for p in pairs]
    g_t = [jnp.where(same_head, _dot_tn(a_solved[p], bh[p]), 0.0) for p in pairs]
    h_t = [jnp.where(same_head,
                     _dot_tn(jnp.concatenate([u[p].astype(BF16), vb[p]], axis=0),
                             jnp.concatenate([bh[p], kh[p]], axis=0)), 0.0) for p in pairs]
    s_new = [s[p] * jnp.exp(-total[p]) + _dot(s_hat[p], g_t[p]) + h_t[p] for p in pairs]
    for p in pairs:
        if add_ref is not None:
            out[p] = out[p] + tile(add_ref, p).astype(F32)
        o_ref[p // per_row, :, (p % per_row) * LANES:(p % per_row + 1) * LANES] = (
            out[p].astype(o_ref.dtype))
        s_ref[p] = s_new[p]


def _wkv(r, k, v, a, b, e, reverse, add=None):
    bsz, t, w = r.shape
    c = min(WKV_CHUNK, t)
    nc = t // c
    rows = math.gcd(bsz, 2)
    if reverse:
        idx = lambda i, j: (i, nc - 1 - j, 0)
    else:
        idx = lambda i, j: (i, j, 0)
    spec = pl.BlockSpec((rows, c, w), idx)
    return pl.pallas_call(
        functools.partial(_wkv_kernel, reverse=reverse),
        grid=(bsz // rows, nc),
        in_specs=[spec] * (6 if add is None else 7),
        out_specs=spec,
        out_shape=jax.ShapeDtypeStruct((bsz, t, w), BF16),
        scratch_shapes=[pltpu.VMEM((rows * (w // LANES), LANES, LANES), F32)],
        compiler_params=_params(("parallel", "arbitrary")),
        name="wkv_bwd" if reverse else "wkv_fwd",
    )(r, k, v, a, b, e, *(() if add is None else (add,)))


def _ffn_rows(x, mod_ref, ln_ref, lnf_ref, wg_ref, wu_ref, wd_ref, final_norm):
    h = _norm_mod(x, ln_ref[...], mod_ref[0, 3:4, :], mod_ref[0, 4:5, :]).astype(BF16)
    acc = None
    for cols in _col_chunks(wg_ref.shape[1], 768):
        gate = jnp.dot(h, wg_ref[:, cols], preferred_element_type=F32)
        up = jnp.dot(h, wu_ref[:, cols], preferred_element_type=F32)
        part = _dot(_silu(gate) * up, wd_ref[cols, :])
        acc = part if acc is None else acc + part
    out = x + mod_ref[0, 5:6, :] * acc
    if final_norm:
        out = _norm_mod(out, lnf_ref[...], 0.0, 0.0)
    return out


def _mixout_kernel(o_ref, bonus_ref, g_ref, yf_ref, gates_ref, x_ref, mod_ref,
                   lnw_ref, lnb_ref, hm_ref, wfo_ref, wro_ref, wo_ref, *rest, ffn_final):
    *ffn_refs, out_ref = rest
    d = x_ref.shape[2]
    for rows in _sub_blocks(x_ref.shape[1]):
        o = o_ref[0, rows, :].astype(F32)
        dev = o - _head_reduce(o, hm_ref[...])
        var = _head_reduce(dev * dev, hm_ref[...])
        o = dev * lax.rsqrt(var + GN_EPS) * lnw_ref[...] + lnb_ref[...]
        y = (o + bonus_ref[0, rows, :].astype(F32)) * g_ref[0, rows, :].astype(F32)
        y_b = _dot(y, wro_ref[...])
        y_a = _dot(yf_ref[0, rows, :], wfo_ref[...])
        merged = (gates_ref[0, rows, :d].astype(F32) * y_a
                  + gates_ref[0, rows, d:].astype(F32) * y_b)
        x = x_ref[0, rows, :] + mod_ref[0, 2:3, :] * _dot(merged, wo_ref[...])
        if ffn_refs:
            x = _ffn_rows(x, mod_ref, *ffn_refs, ffn_final)
        out_ref[0, rows, :] = x


def _mixout(o, bonus, g, yf, gates, x, mod, p, ffn=None):
    b, t, d = x.shape
    tm = _row_tile(t, 1024 if ffn is None else 512)
    row = lambda a: pl.BlockSpec((1, tm, a.shape[2]), lambda i, j: (i, j, 0))
    consts = [p["lnx_w"], p["lnx_b"], p["head_mean"], p["w_fnet_out"], p["w_rwkv_out"], p["w_o"]]
    consts += [] if ffn is None else list(ffn[:5])
    acts = [o, bonus, g, yf, gates, x]
    return pl.pallas_call(
        functools.partial(_mixout_kernel, ffn_final=ffn is not None and ffn[5]),
        grid=(b, t // tm),
        in_specs=[row(a) for a in acts] + [pl.BlockSpec((1, N_MOD, d), lambda i, j: (i, 0, 0))]
                 + [_const_spec(c.shape) for c in consts],
        out_specs=pl.BlockSpec((1, tm, d), lambda i, j: (i, j, 0)),
        out_shape=jax.ShapeDtypeStruct((b, t, d), F32),
        compiler_params=_params(("parallel", "parallel")),
        name="mixout" if ffn is None else "mixout_ffn",
    )(*acts, mod, *consts)


def _route_kernel(x_ref, mod_ref, ln_ref, rw_ref, h_ref, route_ref, cnt_ref, *, ne):
    h = _norm_mod(x_ref[0], ln_ref[...], mod_ref[0, 3:4, :], mod_ref[0, 4:5, :])
    h_ref[0] = h
    tm = h.shape[0]
    logits = lax.dot_general(rw_ref[...], h, (((1,), (1,)), ((), ())), precision=HIGHEST,
                             preferred_element_type=F32)
    ei = lax.broadcasted_iota(jnp.int32, logits.shape, 0)
    m1 = jnp.max(logits, axis=0, keepdims=True)
    i1 = jnp.min(jnp.where(logits == m1, ei, ne), axis=0, keepdims=True)
    rest = jnp.where(ei == i1, -jnp.inf, logits)
    m2 = jnp.max(rest, axis=0, keepdims=True)
    i2 = jnp.min(jnp.where(rest == m2, ei, ne), axis=0, keepdims=True)
    t2 = jnp.exp(m2 - m1)
    w1 = 1.0 / (1.0 + t2)
    w2 = t2 / (1.0 + t2)
    oh1 = jnp.where(ei == i1, 1.0, 0.0)
    oh2 = jnp.where(ei == i2, 1.0, 0.0)
    before = (lax.broadcasted_iota(jnp.int32, (tm, tm), 0)
              < lax.broadcasted_iota(jnp.int32, (tm, tm), 1))
    before = jnp.where(before, 1.0, 0.0).astype(BF16)
    cs1 = jnp.dot(oh1.astype(BF16), before, preferred_element_type=F32)
    cs2 = jnp.dot(oh2.astype(BF16), before, preferred_element_type=F32)
    n1 = jnp.sum(oh1, axis=1, keepdims=True)
    n2 = jnp.sum(oh2, axis=1, keepdims=True)
    size = jnp.broadcast_to(jnp.floor((n1 + n2 + (GROUP_ALIGN - 1)) * (1.0 / GROUP_ALIGN))
                            * GROUP_ALIGN, (ne, LANES))
    sub = lax.broadcasted_iota(jnp.int32, size.shape, 0)
    start = size
    for sh in (1, 2, 4):
        start = start + jnp.where(sub >= sh, pltpu.roll(start, sh, 0), 0.0)
    start = start - size
    pos1 = jnp.sum(oh1 * (cs1 + start[:, 0:1]), axis=0, keepdims=True)
    pos2 = jnp.sum(oh2 * (cs2 + n1 + start[:, 0:1]), axis=0, keepdims=True)
    zero = jnp.zeros_like(w1)
    route_ref[0, 0] = jnp.concatenate(
        [i1.astype(F32), i2.astype(F32), w1, w2, pos1, pos2, zero, zero], axis=0)
    cnt_ref[0, 0] = jnp.concatenate([size, start], axis=0)


def _route(x, mod, ln, router):
    b, t, d = x.shape
    tm = _row_tile(t, 512)
    nt = t // tm
    ne = router.shape[1]
    assert ne == 8
    router = router.T
    return pl.pallas_call(
        functools.partial(_route_kernel, ne=ne),
        grid=(b, nt),
        in_specs=[pl.BlockSpec((1, tm, d), lambda i, j: (i, j, 0)),
                  pl.BlockSpec((1, N_MOD, d), lambda i, j: (i, 0, 0)),
                  _const_spec(ln.shape), _const_spec(router.shape)],
        out_specs=[pl.BlockSpec((1, tm, d), lambda i, j: (i, j, 0)),
                   pl.BlockSpec((1, 1, 8, tm), lambda i, j: (i, j, 0, 0)),
                   pl.BlockSpec((1, 1, 2 * ne, LANES), lambda i, j: (i, j, 0, 0))],
        out_shape=[jax.ShapeDtypeStruct((b, t, d), F32),
                   jax.ShapeDtypeStruct((b, nt, 8, tm), F32),
                   jax.ShapeDtypeStruct((b, nt, 2 * ne, LANES), F32)],
        compiler_params=_params(("parallel", "parallel")),
        name="moe_route",
    )(x, mod, ln, router)


def _row_copy(src, dst, sem):
    return pltpu.make_async_copy(src, dst, sem)


def _sorted_rows(tm):
    return 2 * tm + LANES


def _group_copies(tab_ref, ne, make_copy, issue):
    per = GROUP_CHUNK // GROUP_ALIGN
    for e in range(ne):
        local, units, glob = tab_ref[0, 0, e], tab_ref[0, 0, ne + e], tab_ref[0, 0, 2 * ne + e]
        big = units // per

        def chunk(c, _, rows, first):
            off = pl.multiple_of(first + c * rows, GROUP_ALIGN)
            cp = make_copy(pl.ds(pl.multiple_of(local + off, GROUP_ALIGN), rows),
                           pl.ds(pl.multiple_of(glob + off, GROUP_ALIGN), rows))
            cp.start() if issue else cp.wait()
            return 0

        lax.fori_loop(0, big, functools.partial(chunk, rows=GROUP_CHUNK, first=0), 0)
        lax.fori_loop(0, units - big * per,
                      functools.partial(chunk, rows=GROUP_ALIGN, first=big * GROUP_CHUNK), 0)


def _dispatch_kernel(tab_ref, prev_tab_ref, route_ref, h_ref, rows_in_ref, rows_ref, sorted_ref, sem):
    del rows_in_ref
    i = pl.program_id(0)
    slot = i % 2
    ne = tab_ref.shape[2] // 3
    make = lambda s: (lambda loc, glob: _row_copy(sorted_ref.at[s, loc], rows_ref.at[glob],
                                                  sem.at[s]))
    pos = route_ref[0, 4:6, :].astype(jnp.int32)
    j = lax.broadcasted_iota(jnp.int32, (sorted_ref.shape[1], h_ref.shape[0]), 0)
    pick = jnp.where((j == pos[0:1, :]) | (j == pos[1:2, :]), 1.0, 0.0).astype(BF16)
    sorted_ref[slot] = jnp.dot(pick, h_ref[...].astype(BF16), preferred_element_type=F32)
    _group_copies(tab_ref, ne, make(slot), True)

    @pl.when(i > 0)
    def _():
        _group_copies(prev_tab_ref, ne, make(1 - slot), False)

    @pl.when(i == pl.num_programs(0) - 1)
    def _():
        _group_copies(tab_ref, ne, make(slot), False)


def _dispatch(h, route, table, n_rows):
    n, d = h.shape
    tm = route.shape[2]
    rows = jnp.zeros((n_rows, d), h.dtype)
    return pl.pallas_call(
        _dispatch_kernel,
        grid=(n // tm,),
        in_specs=[pl.BlockSpec((1, 1, table.shape[2]), lambda i: (i, 0, 0),
                               memory_space=pltpu.SMEM),
                  pl.BlockSpec((1, 1, table.shape[2]), lambda i: (jnp.maximum(i - 1, 0), 0, 0),
                               memory_space=pltpu.SMEM),
                  pl.BlockSpec((1, 8, tm), lambda i: (i, 0, 0)),
                  pl.BlockSpec((tm, d), lambda i: (i, 0)),
                  pl.BlockSpec(memory_space=pl.ANY)],
        out_specs=pl.BlockSpec(memory_space=pl.ANY),
        out_shape=jax.ShapeDtypeStruct((n_rows, d), h.dtype),
        scratch_shapes=[pltpu.VMEM((2, _sorted_rows(tm), d), F32), pltpu.SemaphoreType.DMA((2,))],
        input_output_aliases={4: 0},
        compiler_params=pltpu.CompilerParams(dimension_semantics=("arbitrary",),
                                             vmem_limit_bytes=VMEM_LIMIT, has_side_effects=True),
        name="moe_dispatch",
    )(table, table, route, h, rows)


def _experts_kernel(be_ref, nb_ref, x_ref, wg_ref, wu_ref, wd_ref, y_ref):
    del be_ref

    @pl.when(pl.program_id(0) < nb_ref[0])
    def _():
        x = x_ref[...].astype(BF16)
        f = wg_ref.shape[2]
        step = f // 7
        acc = None
        for c0 in range(0, f, step):
            gate = jnp.dot(x, wg_ref[0, :, c0:c0 + step], preferred_element_type=F32)
            up = jnp.dot(x, wu_ref[0, :, c0:c0 + step], preferred_element_type=F32)
            part = _dot(_silu(gate) * up, wd_ref[0, c0:c0 + step, :])
            acc = part if acc is None else acc + part
        y_ref[...] = acc

    @pl.when(pl.program_id(0) >= nb_ref[0])
    def _():
        y_ref[...] = jnp.zeros_like(y_ref)


def _experts(rows, block_e, n_used, wg, wu, wd):
    n_rows, d = rows.shape
    rb = MOE_ROW_BLOCK
    f = wg.shape[2]
    wspec = lambda shape: pl.BlockSpec((1,) + shape, lambda i, be, nb: (be[i], 0, 0),
                                       pipeline_mode=pl.Buffered(1))
    return pl.pallas_call(
        _experts_kernel,
        grid_spec=pltpu.PrefetchScalarGridSpec(
            num_scalar_prefetch=2,
            grid=(n_rows // rb,),
            in_specs=[pl.BlockSpec((rb, d), lambda i, be, nb: (i, 0)),
                      wspec((d, f)), wspec((d, f)), wspec((f, d))],
            out_specs=pl.BlockSpec((rb, d), lambda i, be, nb: (i, 0))),
        out_shape=jax.ShapeDtypeStruct((n_rows, d), F32),
        compiler_params=_params(("arbitrary",)),
        name="moe_experts",
    )(block_e, n_used, rows, wg, wu, wd)


def _combine_kernel(tab_ref, next_tab_ref, route_ref, y_hbm, x_ref, mod_ref, lnf_ref, out_ref,
                    sorted_ref, sem, *, final_norm):
    step = pl.program_id(0) * pl.num_programs(1) + pl.program_id(1)
    slot = step % 2
    ne = tab_ref.shape[2] // 3
    make = lambda s: (lambda loc, glob: _row_copy(y_hbm.at[glob], sorted_ref.at[s, loc],
                                                  sem.at[s]))

    @pl.when(step == 0)
    def _():
        sorted_ref[...] = jnp.zeros_like(sorted_ref)
        _group_copies(tab_ref, ne, make(slot), True)

    @pl.when(step + 1 < pl.num_programs(0) * pl.num_programs(1))
    def _():
        _group_copies(next_tab_ref, ne, make(1 - slot), True)

    _group_copies(tab_ref, ne, make(slot), False)
    pos = route_ref[0, 4:6, :].astype(jnp.int32)
    wts = route_ref[0, 2:4, :]
    j = lax.broadcasted_iota(jnp.int32, (sorted_ref.shape[1], x_ref.shape[1]), 0)
    q_t = (jnp.where(j == pos[0:1, :], wts[0:1, :], 0.0)
           + jnp.where(j == pos[1:2, :], wts[1:2, :], 0.0))
    q_hi, q_lo = _split_bf16(q_t, 2)
    z = sorted_ref[slot].astype(BF16)
    f = _dot_tn(q_hi, z) + _dot_tn(q_lo, z)
    out = x_ref[0] + mod_ref[0, 5:6, :] * f
    if final_norm:
        out = _norm_mod(out, lnf_ref[...], 0.0, 0.0)
    out_ref[0] = out


def _combine(y_rows, route, table, x, mod, ln_final, final_norm):
    b, t, d = x.shape
    tm = route.shape[2]
    nt = t // tm
    return pl.pallas_call(
        functools.partial(_combine_kernel, final_norm=final_norm),
        grid=(b, nt),
        in_specs=[pl.BlockSpec((1, 1, table.shape[2]), lambda i, j: (i * nt + j, 0, 0),
                               memory_space=pltpu.SMEM),
                  pl.BlockSpec((1, 1, table.shape[2]),
                               lambda i, j: (jnp.minimum(i * nt + j + 1, b * nt - 1), 0, 0),
                               memory_space=pltpu.SMEM),
                  pl.BlockSpec((1, 8, tm), lambda i, j: (i * nt + j, 0, 0)),
                  pl.BlockSpec(memory_space=pl.ANY),
                  pl.BlockSpec((1, tm, d), lambda i, j: (i, j, 0)),
                  pl.BlockSpec((1, N_MOD, d), lambda i, j: (i, 0, 0)),
                  _const_spec(ln_final.shape)],
        out_specs=pl.BlockSpec((1, tm, d), lambda i, j: (i, j, 0)),
        out_shape=jax.ShapeDtypeStruct((b, t, d), F32),
        scratch_shapes=[pltpu.VMEM((2, _sorted_rows(tm), d), F32), pltpu.SemaphoreType.DMA((2,))],
        compiler_params=_params(("arbitrary", "arbitrary")),
        name="moe_combine",
    )(table, table, route, y_rows, x, mod, ln_final)


def _ffn_moe(x, mod, ln, ln_final, router, wg, wu, wd, final_norm):
    b, t, d = x.shape
    n = b * t
    ne = router.shape[1]
    rb = MOE_ROW_BLOCK
    h, route, cnt = _route(x, mod, ln, router)
    nt, tm = route.shape[1], route.shape[3]
    tiles = b * nt
    cnt = cnt[..., 0].reshape(tiles, 2 * ne).astype(jnp.int32)
    size, local = cnt[:, :ne], cnt[:, ne:]
    totals = jnp.sum(size, axis=0)
    padded = ((totals + rb - 1) // rb) * rb
    padded_end = jnp.cumsum(padded)
    base = (padded_end - padded)[None, :] + jnp.cumsum(size, axis=0) - size
    table = jnp.concatenate([local, size // GROUP_ALIGN, base], axis=1).reshape(tiles, 1, 3 * ne)
    route = route.reshape(tiles, 8, tm)
    n_blocks = -(-(n * 2 + tiles * ne * (GROUP_ALIGN - 1)) // rb) + ne
    block_e = jnp.minimum(
        jnp.searchsorted(padded_end, jnp.arange(n_blocks, dtype=jnp.int32) * rb, side="right"),
        ne - 1).astype(jnp.int32)
    n_used = (padded_end[-1:] // rb).astype(jnp.int32)
    rows = _dispatch(h.reshape(n, d), route, table, n_blocks * rb)
    y_rows = _experts(rows, block_e, n_used, wg, wu, wd)
    return _combine(y_rows, route, table, x, mod, ln_final, final_norm)


def _block_diag(m, n):
    return jnp.kron(jnp.eye(n, dtype=m.dtype), m)


def _layer_params(i, w_in, conv_w, decay_w0, decay_w2, iclr_a0, iclr_a2, gate_g2, k_k, k_a, r_k,
                  lnx_w, lnx_b, w_fnet_out, w_rwkv_out, w_o):
    w = RWKV_WIDTH
    lora = decay_w2.shape[2]
    tail0 = FNET_WIDTH + 3 * w

    def reorder_tail(m):
        pad = jnp.zeros(m.shape[:-1] + (LANES - lora,), m.dtype)
        return jnp.concatenate([m[..., :3 * lora], pad, m[..., 3 * lora:]], axis=-1)

    row = lambda v: v.reshape(1, -1)
    hi_lo = lambda m: jnp.stack(_split_bf16(m, 2))
    wi = w_in[i]
    rwkv_cols = conv_w.shape[2]
    zeros = jnp.zeros((lora, w), F32)
    return {
        "w_a": wi[:, :FNET_WIDTH].astype(BF16),
        "w_main": wi[:, FNET_WIDTH:tail0].astype(BF16),
        "w_tail": reorder_tail(wi[:, tail0:FNET_WIDTH + rwkv_cols]).astype(BF16),
        "w_gates": wi[:, FNET_WIDTH + rwkv_cols:].astype(BF16),
        "conv_main": conv_w[i][:, :3 * w],
        "conv_tail": reorder_tail(conv_w[i][:, 3 * w:]),
        "decay_w0": decay_w0[i].reshape(1, 2 * w),
        "decay_w2": hi_lo(jnp.concatenate(
            [jnp.concatenate([decay_w2[i, 0], zeros], axis=1),
             jnp.concatenate([zeros, decay_w2[i, 1]], axis=1)], axis=0)),
        "iclr_a0": row(iclr_a0[i]),
        "iclr_a2": hi_lo(jnp.concatenate([iclr_a2[i], jnp.zeros((LANES - lora, w), F32)], axis=0)),
        "gate_g2": hi_lo(gate_g2[i]),
        "k_k": row(k_k[i]), "k_a": row(k_a[i]), "r_k": row(r_k[i]),
        "lnx_w": row(lnx_w[i]), "lnx_b": row(lnx_b[i]),
        "head_sum": _block_diag(jnp.ones((HEAD_DIM, HEAD_DIM), BF16), LANES // HEAD_DIM),
        "head_mean": _block_diag(jnp.full((HEAD_DIM, HEAD_DIM), 1.0 / HEAD_DIM, BF16),
                                 LANES // HEAD_DIM),
        "w_fnet_out": w_fnet_out[i].astype(BF16),
        "w_rwkv_out": w_rwkv_out[i].astype(BF16),
        "w_o": w_o[i].astype(BF16),
    }


def _channel_dft():
    k = jnp.arange(HEAD_DIM, dtype=jnp.int32)
    ang = ((k[:, None] * k[None, :]) % HEAD_DIM).astype(F32) * (2.0 * math.pi / HEAD_DIM)
    groups = FNET_WIDTH // HEAD_DIM
    return jnp.concatenate([_block_diag(jnp.cos(ang), groups), _block_diag(jnp.sin(ang), groups)],
                           axis=1).astype(BF16)


def _token_mixer(x, mod, ln, p, cs, dft_tab, ffn=None):
    zc, zs, um, ut, gates = _inproj(x, mod, ln, p["w_a"], p["w_main"], p["w_tail"], p["w_gates"], cs)
    yf = _fnet(zc, zs, dft_tab)
    r, k, v, a, b, e_f, e_b, g, bonus = _prep(um, ut, p)
    o = _wkv(r, k, v, a, b, e_b, True, add=_wkv(r, k, v, a, b, e_f, False))
    return _mixout(o, bonus, g, yf, gates, x, mod, p, ffn)


def kernel(x_prompt, x_sample, c_prompt, c_sample, w_ada, b_ada, ln_mix, w_in, conv_w, decay_w0, decay_w2, iclr_a0, iclr_a2, gate_g2, k_k, k_a, r_k, lnx_w, lnx_b, w_fnet_out, w_rwkv_out, w_o, ln_ffn, ff_w_gate, ff_w_up, ff_w_down, router_w, moe_w_gate, moe_w_up, moe_w_down, ln_final):
    depth, d = ln_mix.shape
    streams = [x_prompt, x_sample]
    nb = [x.shape[0] for x in streams]
    mod_all = _ada(jnp.concatenate([c_prompt, c_sample], axis=0), w_ada, b_ada)
    cs = _channel_dft()
    dfts = {}
    for x in streams:
        if x.shape[1] not in dfts:
            dfts[x.shape[1]] = _dft_tables(x.shape[1])
    lnf = ln_final.reshape(1, d)
    for i in range(depth):
        p = _layer_params(i, w_in, conv_w, decay_w0, decay_w2, iclr_a0, iclr_a2, gate_g2, k_k, k_a,
                          r_k, lnx_w, lnx_b, w_fnet_out, w_rwkv_out, w_o)
        j = i // 2
        final = i == depth - 1
        if i % 2 == 0:
            ffw = (ff_w_gate[j].astype(BF16), ff_w_up[j].astype(BF16), ff_w_down[j].astype(BF16))
        else:
            ffw = (moe_w_gate[j].astype(BF16), moe_w_up[j].astype(BF16), moe_w_down[j].astype(BF16))
            router = router_w[j]
        off = 0
        for s, x in enumerate(streams):
            mod = mod_all[i, off:off + nb[s]].reshape(nb[s], N_MOD, d)
            off += nb[s]
            ln1, ln2 = ln_mix[i].reshape(1, d), ln_ffn[i].reshape(1, d)
            if i % 2 == 0:
                x = _token_mixer(x, mod, ln1, p, cs, dfts[x.shape[1]], (ln2, lnf, *ffw, final))
            else:
                x = _token_mixer(x, mod, ln1, p, cs, dfts[x.shape[1]])
                x = _ffn_moe(x, mod, ln2, lnf, router, *ffw, final)
            streams[s] = x
    return tuple(streams)
```

```python
import functools
import math

import jax
import jax.numpy as jnp
from jax import lax
from jax.experimental import pallas as pl
from jax.experimental.pallas import tpu as pltpu

F32 = jnp.float32
BF16 = jnp.bfloat16
HIGHEST = lax.Precision.HIGHEST

HEAD_DIM = 64
LANES = 128
FNET_WIDTH = 256
RWKV_WIDTH = 768
LORA_TAIL = 384
N_MOD = 6
N_EXPERTS = 8
RMS_EPS = 1e-6
GN_EPS = 64e-5
WKV_CHUNK = 128
MOE_ROW_BLOCK = 512
GROUP_ALIGN = 8
GROUP_CHUNK = 64
VMEM_LIMIT = 56 * 1024 * 1024


def _params(sem, vmem=VMEM_LIMIT):
    return pltpu.CompilerParams(dimension_semantics=sem, vmem_limit_bytes=vmem)


def _const_spec(shape):
    nd = len(shape)
    return pl.BlockSpec(shape, lambda *_: (0,) * nd, pipeline_mode=pl.Buffered(1))


def _dot(a, b):
    return jnp.dot(a.astype(BF16), b.astype(BF16), preferred_element_type=F32)


def _dot_nt(a, b):
    return lax.dot_general(a.astype(BF16), b.astype(BF16), (((1,), (1,)), ((), ())),
                           preferred_element_type=F32)


def _dot_tn(a, b):
    return lax.dot_general(a.astype(BF16), b.astype(BF16), (((0,), (0,)), ((), ())),
                           preferred_element_type=F32)


def _split_bf16(x, parts):
    out, rem = [], x
    for _ in range(parts):
        p = rem.astype(BF16)
        out.append(p)
        rem = rem - p.astype(F32)
    return out


def _dot_exact_rhs(x, m, parts):
    acc = None
    for p in _split_bf16(x, parts):
        t = jnp.dot(p, m, preferred_element_type=F32)
        acc = t if acc is None else acc + t
    return acc


def _dot_f32(a, b):
    return jnp.dot(a, b, precision=HIGHEST, preferred_element_type=F32)


def _dot_3pass(a, w_ref):
    ah, al = _split_bf16(a, 2)
    d = lambda x, y: jnp.dot(x, y, preferred_element_type=F32)
    return d(ah, w_ref[0]) + (d(al, w_ref[0]) + d(ah, w_ref[1]))


def _norm_mod(x, gain, shift, scale):
    ms = jnp.mean(x * x, axis=-1, keepdims=True)
    return x * lax.rsqrt(ms + RMS_EPS) * gain * (1.0 + scale) + shift


def _silu(x):
    return x * jax.nn.sigmoid(x)


def _head_reduce(x, m, parts=2):
    cols = [_dot_exact_rhs(x[:, j:j + LANES], m, parts) for j in range(0, x.shape[1], LANES)]
    return jnp.concatenate(cols, axis=1)


def _row_tile(t, want):
    tm = min(t, want)
    assert t % tm == 0
    return tm


def _sub_blocks(rows, size=512):
    size = min(size, rows)
    assert rows % size == 0
    return [slice(r, r + size) for r in range(0, rows, size)]


def _col_chunks(width, size):
    return [slice(c, min(c + size, width)) for c in range(0, width, size)]


def _ada_kernel(c_ref, w_ref, b_ref, o_ref):
    o_ref[0] = _dot_f32(_silu(c_ref[...]), w_ref[0]) + b_ref[0]


def _ada(c_all, w_ada, b_ada):
    depth, d, n = w_ada.shape
    bc = c_all.shape[0]
    tn = n // 4
    return pl.pallas_call(
        _ada_kernel,
        grid=(depth, n // tn),
        in_specs=[pl.BlockSpec((bc, d), lambda l, j: (0, 0)),
                  pl.BlockSpec((1, d, tn), lambda l, j: (l, 0, j)),
                  pl.BlockSpec((1, 1, tn), lambda l, j: (l, 0, j))],
        out_specs=pl.BlockSpec((1, bc, tn), lambda l, j: (l, 0, j)),
        out_shape=jax.ShapeDtypeStruct((depth, bc, n), F32),
        compiler_params=_params(("parallel", "parallel")),
        name="ada",
    )(c_all, w_ada, b_ada.reshape(depth, 1, n))


def _inproj_kernel(x_ref, mod_ref, ln_ref, wa_ref, wm_ref, wt_ref, wg_ref, cs_ref,
                   zc_ref, zs_ref, um_ref, ut_ref, g_ref):
    for rows in _sub_blocks(x_ref.shape[1]):
        h = _norm_mod(x_ref[0, rows, :], ln_ref[...], mod_ref[0, 0:1, :],
                      mod_ref[0, 1:2, :]).astype(BF16)
        ua = jnp.dot(h, wa_ref[...], preferred_element_type=F32)
        zz = jnp.dot(ua.astype(BF16), cs_ref[...], preferred_element_type=F32)
        zc_ref[0, rows, :] = zz[:, :FNET_WIDTH]
        zs_ref[0, rows, :] = zz[:, FNET_WIDTH:]
        for cols in _col_chunks(wm_ref.shape[1], 1280):
            um_ref[0, rows, cols] = jnp.dot(
                h, wm_ref[:, cols], preferred_element_type=F32).astype(BF16)
        ut_ref[0, rows, :] = jnp.dot(h, wt_ref[...], preferred_element_type=F32)
        for cols in _col_chunks(wg_ref.shape[1], 1024):
            g_ref[0, rows, cols] = jax.nn.sigmoid(
                jnp.dot(h, wg_ref[:, cols], preferred_element_type=F32)).astype(BF16)


def _inproj(x, mod, ln, wa, wm, wt, wg, cs):
    b, t, d = x.shape
    tm = _row_tile(t, 1024)
    row = lambda w: pl.BlockSpec((1, tm, w), lambda i, j: (i, j, 0))
    outs = [(FNET_WIDTH, F32), (FNET_WIDTH, F32), (wm.shape[1], BF16), (wt.shape[1], F32),
            (wg.shape[1], BF16)]
    return pl.pallas_call(
        _inproj_kernel,
        grid=(b, t // tm),
        in_specs=[row(d), pl.BlockSpec((1, N_MOD, d), lambda i, j: (i, 0, 0)),
                  _const_spec(ln.shape), _const_spec(wa.shape), _const_spec(wm.shape),
                  _const_spec(wt.shape), _const_spec(wg.shape), _const_spec(cs.shape)],
        out_specs=[row(w) for w, _ in outs],
        out_shape=[jax.ShapeDtypeStruct((b, t, w), dt) for w, dt in outs],
        compiler_params=_params(("parallel", "parallel")),
        name="inproj",
    )(x, mod, ln, wa, wm, wt, wg, cs)


FNET_T2 = 64
FNET_ROWS = 32


def _rows_at(ref, j):
    return ref[0, :, j:j + 1, :].reshape(ref.shape[1], ref.shape[3])


def _fnet_stage1_kernel(d_ref, zc_ref, zs_ref, twc_ref, tws_ref, gr_ref, gi_ref):
    t2, w = zc_ref.shape[1], zc_ref.shape[3]
    for j in range(zc_ref.shape[2]):
        data = jnp.concatenate([_rows_at(zc_ref, j), _rows_at(zs_ref, j)], axis=0)
        g = _dot(d_ref[...], data)
        g_r, g_i = g[:t2], g[t2:]
        c = jnp.concatenate([twc_ref[j]] * (w // LANES), axis=1)
        s = jnp.concatenate([tws_ref[j]] * (w // LANES), axis=1)
        gr_ref[0, j] = g_r * c + g_i * s
        gi_ref[0, j] = g_i * c - g_r * s


def _fnet_stage2_kernel(d_ref, gr_ref, gi_ref, o_ref, *, scale):
    t1, w = gr_ref.shape[1], gr_ref.shape[3]
    for j in range(gr_ref.shape[2]):
        data = jnp.concatenate([_rows_at(gr_ref, j), _rows_at(gi_ref, j)], axis=0)
        o_ref[0, :, j:j + 1, :] = (_dot(d_ref[...], data) * scale).reshape(t1, 1, w)


def _dft_tables(t):
    t2 = FNET_T2
    t1 = t // t2
    assert t1 * t2 == t

    def cos_sin(rows, cols, n):
        k = (lax.iota(jnp.int32, rows)[:, None] * lax.iota(jnp.int32, cols)[None, :]) % n
        ang = k.astype(F32) * (2.0 * math.pi / n)
        return jnp.cos(ang), jnp.sin(ang)

    c2, s2 = cos_sin(t2, t2, t2)
    c1, s1 = cos_sin(t1, t1, t1)
    twc, tws = cos_sin(t1, t2, t)
    lanes = lambda m: jnp.broadcast_to(m[:, :, None], (t1, t2, LANES))
    return {
        "stage1": jnp.block([[c2, -s2], [-s2, -c2]]).astype(BF16),
        "stage2": jnp.concatenate([c1, s1], axis=1).astype(BF16),
        "twc": lanes(twc), "tws": lanes(tws),
    }


def _fnet(zc, zs, tab):
    b, t, w = zc.shape
    t2 = FNET_T2
    t1 = t // t2
    n1 = min(FNET_ROWS, t1)
    in_spec = pl.BlockSpec((1, t2, n1, w), lambda i, j: (i, 0, j, 0))
    tw_spec = pl.BlockSpec((n1, t2, LANES), lambda i, j: (j, 0, 0))
    g_shape = jax.ShapeDtypeStruct((b, t1, t2, w), F32)
    g_r, g_i = pl.pallas_call(
        _fnet_stage1_kernel,
        grid=(b, t1 // n1),
        in_specs=[_const_spec(tab["stage1"].shape), in_spec, in_spec, tw_spec, tw_spec],
        out_specs=[pl.BlockSpec((1, n1, t2, w), lambda i, j: (i, j, 0, 0))] * 2,
        out_shape=[g_shape, g_shape],
        compiler_params=_params(("parallel", "parallel")),
        name="fnet_stage1",
    )(tab["stage1"], zc.reshape(b, t2, t1, w), zs.reshape(b, t2, t1, w), tab["twc"], tab["tws"])
    n2 = min(FNET_ROWS, t2)
    spec = pl.BlockSpec((1, t1, n2, w), lambda i, j: (i, 0, j, 0))
    y = pl.pallas_call(
        functools.partial(_fnet_stage2_kernel, scale=1.0 / math.sqrt(t * HEAD_DIM)),
        grid=(b, t2 // n2),
        in_specs=[_const_spec(tab["stage2"].shape), spec, spec],
        out_specs=spec,
        out_shape=jax.ShapeDtypeStruct((b, t1, t2, w), F32),
        compiler_params=_params(("parallel", "parallel")),
        name="fnet_stage2",
    )(tab["stage2"], g_r, g_i)
    return y.reshape(b, t, w)


def _conv3(main, prev_row, next_row, w):
    tm = main.shape[0]
    ri = lax.broadcasted_iota(jnp.int32, main.shape, 0)
    up = jnp.where(ri == 0, prev_row, pltpu.roll(main, 1, 0))
    dn = jnp.where(ri == tm - 1, next_row, pltpu.roll(main, tm - 1, 0))
    return w[0:1] * up + w[1:2] * main + w[2:3] * dn


def _prep_kernel(um_ref, ump_ref, umn_ref, ut_ref, utp_ref, utn_ref, cwm_ref, cwt_ref,
                 w0_ref, w2_ref, a0_ref, a2_ref, g2_ref, kk_ref, ka_ref, rk_ref, hs_ref, shift_ref,
                 r_ref, k_ref, v_ref, a_ref, b_ref, ef_ref, eb_ref, g_ref, bonus_ref):
    j = pl.program_id(1)
    first = j == 0
    last = j == pl.num_programs(1) - 1
    hp = ump_ref.shape[1]

    def conv_main(c0, c1):
        main = um_ref[0, :, c0:c1]
        tm = main.shape[0]
        shifted = jnp.dot(shift_ref[...], main, preferred_element_type=F32)
        sub = 8
        ri = lax.broadcasted_iota(jnp.int32, (sub, main.shape[1]), 0)
        prev = jnp.where(first, 0.0, ump_ref[0, hp - 1:hp, c0:c1].astype(F32))
        nxt = jnp.where(last, 0.0, umn_ref[0, 0:1, c0:c1].astype(F32))
        up = jnp.concatenate([jnp.where(ri == 0, prev, shifted[:sub]), shifted[sub:tm]], axis=0)
        dn = jnp.concatenate([shifted[tm:2 * tm - sub],
                              jnp.where(ri == sub - 1, nxt, shifted[2 * tm - sub:])], axis=0)
        w = cwm_ref[:, c0:c1]
        return w[0:1] * up + w[1:2] * main.astype(F32) + w[2:3] * dn

    w = RWKV_WIDTH
    r = conv_main(0, w)
    k = conv_main(w, 2 * w)
    v = conv_main(2 * w, 3 * w)
    tp = utp_ref.shape[1]
    tail = _conv3(ut_ref[0], jnp.where(first, 0.0, utp_ref[0, tp - 1:tp, :]),
                  jnp.where(last, 0.0, utn_ref[0, 0:1, :]), cwt_ref[...])
    xw, xa, xg = tail[:, 0:LANES], tail[:, LANES:2 * LANES], tail[:, 2 * LANES:3 * LANES]

    e = jax.nn.sigmoid(w0_ref[...] + _dot_3pass(jnp.tanh(xw), w2_ref)) * math.exp(-0.5)
    ef_ref[0] = e[:, :w]
    eb_ref[0] = e[:, w:]
    a = jax.nn.sigmoid(a0_ref[...] + _dot_3pass(xa, a2_ref))
    g_ref[0] = _dot_3pass(jax.nn.sigmoid(xg), g2_ref).astype(BF16)

    kk = k * kk_ref[...]
    kk = kk * lax.rsqrt(_head_reduce(kk * kk, hs_ref[...], parts=1) + 1e-12)
    k = k * (1.0 + (a - 1.0) * ka_ref[...])
    r_ref[0] = r.astype(BF16)
    k_ref[0] = k.astype(BF16)
    v_ref[0] = v.astype(BF16)
    a_ref[0] = (-kk).astype(BF16)
    b_ref[0] = (kk * a).astype(BF16)
    bonus_ref[0] = (_head_reduce(r * k * rk_ref[...], hs_ref[...]) * v).astype(BF16)


def _prep(um, ut, p):
    b, t, wm = um.shape
    wt = ut.shape[2]
    tm = _row_tile(t, 256)
    hm, ht = 16, 8
    nm, nt = tm // hm, tm // ht
    main = lambda w: pl.BlockSpec((1, tm, w), lambda i, j: (i, j, 0))
    prev = lambda h, n, w: pl.BlockSpec((1, h, w), lambda i, j: (i, jnp.maximum(j * n - 1, 0), 0))
    nxt = lambda h, n, w: pl.BlockSpec(
        (1, h, w), lambda i, j: (i, jnp.minimum((j + 1) * n, t // h - 1), 0))
    consts = [p["conv_main"], p["conv_tail"], p["decay_w0"], p["decay_w2"], p["iclr_a0"],
              p["iclr_a2"], p["gate_g2"], p["k_k"], p["k_a"], p["r_k"], p["head_sum"],
              jnp.concatenate([jnp.eye(tm, k=-1, dtype=BF16), jnp.eye(tm, k=1, dtype=BF16)], axis=0)]
    outs = [BF16] * 5 + [F32, F32, BF16, BF16]
    return pl.pallas_call(
        _prep_kernel,
        grid=(b, t // tm),
        in_specs=[main(wm), prev(hm, nm, wm), nxt(hm, nm, wm),
                  main(wt), prev(ht, nt, wt), nxt(ht, nt, wt)]
                 + [_const_spec(c.shape) for c in consts],
        out_specs=[main(RWKV_WIDTH) for _ in outs],
        out_shape=[jax.ShapeDtypeStruct((b, t, RWKV_WIDTH), dt) for dt in outs],
        compiler_params=_params(("parallel", "parallel")),
        name="rwkv_prep",
    )(um, um, um, ut, ut, ut, *consts)


def _wkv_kernel(r_ref, k_ref, v_ref, a_ref, b_ref, e_ref, *rest, reverse):
    *add_refs, o_ref, s_ref = rest
    add_ref = add_refs[0] if add_refs else None
    c = r_ref.shape[1]
    pairs = range(s_ref.shape[0])
    heads = range(LANES // HEAD_DIM)
    chains = [(p, h) for p in pairs for h in heads]

    @pl.when(pl.program_id(1) == 0)
    def _():
        s_ref[...] = jnp.zeros_like(s_ref)

    row = lax.broadcasted_iota(jnp.int32, (c, c), 0)
    col = lax.broadcasted_iota(jnp.int32, (c, c), 1)
    if reverse:
        incl, strict = col >= row, col > row
    else:
        incl, strict = col <= row, col < row
    tri = jnp.where(incl, 1.0, 0.0).astype(BF16)
    end = 0 if reverse else c - 1
    lane = lax.broadcasted_iota(jnp.int32, (1, LANES), 1)
    in_head = [(lane >= h * HEAD_DIM) & (lane < (h + 1) * HEAD_DIM) for h in heads]
    same_head = (lax.broadcasted_iota(jnp.int32, (LANES, LANES), 0) // HEAD_DIM
                 == lax.broadcasted_iota(jnp.int32, (LANES, LANES), 1) // HEAD_DIM)

    per_row = r_ref.shape[2] // LANES

    def tile(ref, p):
        return ref[p // per_row, :, (p % per_row) * LANES:(p % per_row + 1) * LANES]

    tri2 = jnp.concatenate([tri, tri], axis=1)

    def running_sum(e):
        return jnp.dot(tri2, jnp.concatenate(_split_bf16(e, 2), axis=0), preferred_element_type=F32)

    cum = [running_sum(tile(e_ref, p)) for p in pairs]
    total = [x[end:end + 1, :] for x in cum]
    ref = [x[c // 2:c // 2 + 1, :] for x in cum]
    dl = [cum[p] - ref[p] for p in pairs]
    g_inv = [jnp.exp(dl[p]) for p in pairs]
    at = [tile(a_ref, p).astype(F32) * jnp.exp(tile(e_ref, p) - dl[p]) for p in pairs]
    rt = [tile(r_ref, p).astype(F32) * jnp.exp(-dl[p]) for p in pairs]
    bt = [tile(b_ref, p).astype(F32) * g_inv[p] for p in pairs]
    kt = [tile(k_ref, p).astype(F32) * g_inv[p] for p in pairs]
    g_end = [jnp.exp(ref[p] - total[p]) for p in pairs]
    bh = [(bt[p] * g_end[p]).astype(BF16) for p in pairs]
    kh = [(kt[p] * g_end[p]).astype(BF16) for p in pairs]
    bk = [jnp.concatenate([bt[p], kt[p]], axis=0).astype(BF16) for p in pairs]
    ar = [jnp.concatenate([at[p], rt[p]], axis=0) for p in pairs]
    vb = [tile(v_ref, p) for p in pairs]

    prod = [_dot_nt(jnp.where(in_head[h], ar[p], 0.0), bk[p]) for p, h in chains]
    l_ab = [jnp.where(strict, x[:c, :c], 0.0) for x in prod]
    l_ak = [jnp.where(strict, x[:c, c:], 0.0) for x in prod]
    m_rb = [jnp.where(incl, x[c:, :c], 0.0).astype(BF16) for x in prod]
    m_rk = [jnp.where(incl, x[c:, c:], 0.0).astype(BF16) for x in prod]
    del prod
    eye = jnp.where(row == col, 1.0, 0.0)
    t = [eye + jnp.where((row >> 1) == (col >> 1), x, 0.0) for x in l_ab]
    nch = range(len(chains))
    for sh in range(1, (c - 1).bit_length()):
        m = 1 << sh
        joins = ((row >> (sh + 1)) == (col >> (sh + 1))) & ((row >> sh) != (col >> sh))
        e_k = [jnp.where(joins, l_ab[i], 0.0) for i in nch]
        if m < 8:
            et = [_dot(e_k[i], t[i]) for i in nch]
            t = [t[i] + _dot(t[i], et[i]) for i in nch]
            continue
        live = [slice(b0 + (0 if reverse else m), b0 + (m if reverse else 2 * m))
                for b0 in range(0, c, 2 * m)]

        def take(v):
            return jnp.concatenate([v[rs] for rs in live], axis=0)

        def spread(v):
            zero = jnp.zeros((m, v.shape[1]), v.dtype)
            parts = []
            for j in range(len(live)):
                blk = v[j * m:(j + 1) * m]
                parts += [blk, zero] if reverse else [zero, blk]
            return jnp.concatenate(parts, axis=0)

        et = [_dot(take(e_k[i]), t[i]) for i in nch]
        upd = [_dot(take(t[i]), spread(et[i])) for i in nch]
        t = [t[i] + spread(upd[i]) for i in nch]
    lakv = [jnp.dot(l_ak[i].astype(BF16), vb[p], preferred_element_type=F32)
            for i, (p, h) in enumerate(chains)]
    x = [_dot(t[i], jnp.concatenate([at[p], lakv[i]], axis=1)) for i, (p, h) in enumerate(chains)]
    x_a = [v[:, :LANES] for v in x]
    x_u = [v[:, LANES:] for v in x]
    y_a = [jnp.dot(m_rb[i], x_a[i].astype(BF16), preferred_element_type=F32) for i in nch]
    y_o = [jnp.dot(jnp.concatenate([m_rb[i], m_rk[i]], axis=1),
                   jnp.concatenate([x_u[i].astype(BF16), vb[p]], axis=0),
                   preferred_element_type=F32) for i, (p, h) in enumerate(chains)]

    def pick(vals, p):
        out = vals[p * len(heads)]
        for h in heads[1:]:
            out = jnp.where(in_head[h], vals[p * len(heads) + h], out)
        return out

    a_solved = [pick(x_a, p) for p in pairs]
    u = [pick(x_u, p) for p in pairs]
    r_hat = [rt[p] + pick(y_a, p) for p in pairs]
    o_intra = [pick(y_o, p) for p in pairs]
    s = [s_ref[p] for p in pairs]
    s_hat = [(s[p] * jnp.exp(-ref[p])).astype(BF16) for p in pairs]
    out = [_dot_nt(r_hat[p], s_hat[p]) + o_intra[p] for p in pairs]
    g_t = [jnp.where(same_head, _dot_tn(a_solved[p], bh[p]), 0.0) for p in pairs]
    h_t = [jnp.where(same_head,
                     _dot_tn(jnp.concatenate([u[p].astype(BF16), vb[p]], axis=0),
                             jnp.concatenate([bh[p], kh[p]], axis=0)), 0.0) for p in pairs]
    s_new = [s[p] * jnp.exp(-total[p]) + _dot(s_hat[p], g_t[p]) + h_t[p] for p in pairs]
    for p in pairs:
        if add_ref is not None:
            out[p] = out[p] + tile(add_ref, p).astype(F32)
        o_ref[p // per_row, :, (p % per_row) * LANES:(p % per_row + 1) * LANES] = (
            out[p].astype(o_ref.dtype))
        s_ref[p] = s_new[p]


def _wkv(r, k, v, a, b, e, reverse, add=None):
    bsz, t, w = r.shape
    c = min(WKV_CHUNK, t)
    nc = t // c
    rows = math.gcd(bsz, 2)
    if reverse:
        idx = lambda i, j: (i, nc - 1 - j, 0)
    else:
        idx = lambda i, j: (i, j, 0)
    spec = pl.BlockSpec((rows, c, w), idx)
    return pl.pallas_call(
        functools.partial(_wkv_kernel, reverse=reverse),
        grid=(bsz // rows, nc),
        in_specs=[spec] * (6 if add is None else 7),
        out_specs=spec,
        out_shape=jax.ShapeDtypeStruct((bsz, t, w), BF16),
        scratch_shapes=[pltpu.VMEM((rows * (w // LANES), LANES, LANES), F32)],
        compiler_params=_params(("parallel", "arbitrary")),
        name="wkv_bwd" if reverse else "wkv_fwd",
    )(r, k, v, a, b, e, *(() if add is None else (add,)))


def _ffn_rows(x, mod_ref, ln_ref, lnf_ref, wg_ref, wu_ref, wd_ref, final_norm):
    h = _norm_mod(x, ln_ref[...], mod_ref[0, 3:4, :], mod_ref[0, 4:5, :]).astype(BF16)
    acc = None
    for cols in _col_chunks(wg_ref.shape[1], 768):
        gate = jnp.dot(h, wg_ref[:, cols], preferred_element_type=F32)
        up = jnp.dot(h, wu_ref[:, cols], preferred_element_type=F32)
        part = _dot(_silu(gate) * up, wd_ref[cols, :])
        acc = part if acc is None else acc + part
    out = x + mod_ref[0, 5:6, :] * acc
    if final_norm:
        out = _norm_mod(out, lnf_ref[...], 0.0, 0.0)
    return out


def _mixout_kernel(o_ref, bonus_ref, g_ref, yf_ref, gates_ref, x_ref, mod_ref,
                   lnw_ref, lnb_ref, hm_ref, wfo_ref, wro_ref, wo_ref, *rest, ffn_final):
    *ffn_refs, out_ref = rest
    d = x_ref.shape[2]
    for rows in _sub_blocks(x_ref.shape[1]):
        o = o_ref[0, rows, :].astype(F32)
        dev = o - _head_reduce(o, hm_ref[...])
        var = _head_reduce(dev * dev, hm_ref[...])
        o = dev * lax.rsqrt(var + GN_EPS) * lnw_ref[...] + lnb_ref[...]
        y = (o + bonus_ref[0, rows, :].astype(F32)) * g_ref[0, rows, :].astype(F32)
        y_b = _dot(y, wro_ref[...])
        y_a = _dot(yf_ref[0, rows, :], wfo_ref[...])
        merged = (gates_ref[0, rows, :d].astype(F32) * y_a
                  + gates_ref[0, rows, d:].astype(F32) * y_b)
        x = x_ref[0, rows, :] + mod_ref[0, 2:3, :] * _dot(merged, wo_ref[...])
        if ffn_refs:
            x = _ffn_rows(x, mod_ref, *ffn_refs, ffn_final)
        out_ref[0, rows, :] = x


def _mixout(o, bonus, g, yf, gates, x, mod, p, ffn=None):
    b, t, d = x.shape
    tm = _row_tile(t, 1024 if ffn is None else 512)
    row = lambda a: pl.BlockSpec((1, tm, a.shape[2]), lambda i, j: (i, j, 0))
    consts = [p["lnx_w"], p["lnx_b"], p["head_mean"], p["w_fnet_out"], p["w_rwkv_out"], p["w_o"]]
    consts += [] if ffn is None else list(ffn[:5])
    acts = [o, bonus, g, yf, gates, x]
    return pl.pallas_call(
        functools.partial(_mixout_kernel, ffn_final=ffn is not None and ffn[5]),
        grid=(b, t // tm),
        in_specs=[row(a) for a in acts] + [pl.BlockSpec((1, N_MOD, d), lambda i, j: (i, 0, 0))]
                 + [_const_spec(c.shape) for c in consts],
        out_specs=pl.BlockSpec((1, tm, d), lambda i, j: (i, j, 0)),
        out_shape=jax.ShapeDtypeStruct((b, t, d), F32),
        compiler_params=_params(("parallel", "parallel")),
        name="mixout" if ffn is None else "mixout_ffn",
    )(*acts, mod, *consts)


def _route_kernel(x_ref, mod_ref, ln_ref, rw_ref, h_ref, route_ref, cnt_ref, *, ne):
    h = _norm_mod(x_ref[0], ln_ref[...], mod_ref[0, 3:4, :], mod_ref[0, 4:5, :])
    h_ref[0] = h
    tm = h.shape[0]
    logits = lax.dot_general(rw_ref[...], h, (((1,), (1,)), ((), ())), precision=HIGHEST,
                             preferred_element_type=F32)
    ei = lax.broadcasted_iota(jnp.int32, logits.shape, 0)
    m1 = jnp.max(logits, axis=0, keepdims=True)
    i1 = jnp.min(jnp.where(logits == m1, ei, ne), axis=0, keepdims=True)
    rest = jnp.where(ei == i1, -jnp.inf, logits)
    m2 = jnp.max(rest, axis=0, keepdims=True)
    i2 = jnp.min(jnp.where(rest == m2, ei, ne), axis=0, keepdims=True)
    t2 = jnp.exp(m2 - m1)
    w1 = 1.0 / (1.0 + t2)
    w2 = t2 / (1.0 + t2)
    oh1 = jnp.where(ei == i1, 1.0, 0.0)
    oh2 = jnp.where(ei == i2, 1.0, 0.0)
    before = (lax.broadcasted_iota(jnp.int32, (tm, tm), 0)
              < lax.broadcasted_iota(jnp.int32, (tm, tm), 1))
    before = jnp.where(before, 1.0, 0.0).astype(BF16)
    cs1 = jnp.dot(oh1.astype(BF16), before, preferred_element_type=F32)
    cs2 = jnp.dot(oh2.astype(BF16), before, preferred_element_type=F32)
    n1 = jnp.sum(oh1, axis=1, keepdims=True)
    n2 = jnp.sum(oh2, axis=1, keepdims=True)
    size = jnp.broadcast_to(jnp.floor((n1 + n2 + (GROUP_ALIGN - 1)) * (1.0 / GROUP_ALIGN))
                            * GROUP_ALIGN, (ne, LANES))
    sub = lax.broadcasted_iota(jnp.int32, size.shape, 0)
    start = size
    for sh in (1, 2, 4):
        start = start + jnp.where(sub >= sh, pltpu.roll(start, sh, 0), 0.0)
    start = start - size
    pos1 = jnp.sum(oh1 * (cs1 + start[:, 0:1]), axis=0, keepdims=True)
    pos2 = jnp.sum(oh2 * (cs2 + n1 + start[:, 0:1]), axis=0, keepdims=True)
    zero = jnp.zeros_like(w1)
    route_ref[0, 0] = jnp.concatenate(
        [i1.astype(F32), i2.astype(F32), w1, w2, pos1, pos2, zero, zero], axis=0)
    cnt_ref[0, 0] = jnp.concatenate([size, start], axis=0)


def _route(x, mod, ln, router):
    b, t, d = x.shape
    tm = _row_tile(t, 512)
    nt = t // tm
    ne = router.shape[1]
    assert ne == 8
    router = router.T
    return pl.pallas_call(
        functools.partial(_route_kernel, ne=ne),
        grid=(b, nt),
        in_specs=[pl.BlockSpec((1, tm, d), lambda i, j: (i, j, 0)),
                  pl.BlockSpec((1, N_MOD, d), lambda i, j: (i, 0, 0)),
                  _const_spec(ln.shape), _const_spec(router.shape)],
        out_specs=[pl.BlockSpec((1, tm, d), lambda i, j: (i, j, 0)),
                   pl.BlockSpec((1, 1, 8, tm), lambda i, j: (i, j, 0, 0)),
                   pl.BlockSpec((1, 1, 2 * ne, LANES), lambda i, j: (i, j, 0, 0))],
        out_shape=[jax.ShapeDtypeStruct((b, t, d), F32),
                   jax.ShapeDtypeStruct((b, nt, 8, tm), F32),
                   jax.ShapeDtypeStruct((b, nt, 2 * ne, LANES), F32)],
        compiler_params=_params(("parallel", "parallel")),
        name="moe_route",
    )(x, mod, ln, router)


def _row_copy(src, dst, sem):
    return pltpu.make_async_copy(src, dst, sem)


def _sorted_rows(tm):
    return 2 * tm + LANES


def _group_copies(tab_ref, ne, make_copy, issue):
    per = GROUP_CHUNK // GROUP_ALIGN
    for e in range(ne):
        local, units, glob = tab_ref[0, 0, e], tab_ref[0, 0, ne + e], tab_ref[0, 0, 2 * ne + e]
        big = units // per

        def chunk(c, _, rows, first):
            off = pl.multiple_of(first + c * rows, GROUP_ALIGN)
            cp = make_copy(pl.ds(pl.multiple_of(local + off, GROUP_ALIGN), rows),
                           pl.ds(pl.multiple_of(glob + off, GROUP_ALIGN), rows))
            cp.start() if issue else cp.wait()
            return 0

        lax.fori_loop(0, big, functools.partial(chunk, rows=GROUP_CHUNK, first=0), 0)
        lax.fori_loop(0, units - big * per,
                      functools.partial(chunk, rows=GROUP_ALIGN, first=big * GROUP_CHUNK), 0)


def _dispatch_kernel(tab_ref, prev_tab_ref, route_ref, h_ref, rows_in_ref, rows_ref, sorted_ref, sem):
    del rows_in_ref
    i = pl.program_id(0)
    slot = i % 2
    ne = tab_ref.shape[2] // 3
    make = lambda s: (lambda loc, glob: _row_copy(sorted_ref.at[s, loc], rows_ref.at[glob],
                                                  sem.at[s]))
    pos = route_ref[0, 4:6, :].astype(jnp.int32)
    j = lax.broadcasted_iota(jnp.int32, (sorted_ref.shape[1], h_ref.shape[0]), 0)
    pick = jnp.where((j == pos[0:1, :]) | (j == pos[1:2, :]), 1.0, 0.0).astype(BF16)
    sorted_ref[slot] = jnp.dot(pick, h_ref[...].astype(BF16), preferred_element_type=F32)
    _group_copies(tab_ref, ne, make(slot), True)

    @pl.when(i > 0)
    def _():
        _group_copies(prev_tab_ref, ne, make(1 - slot), False)

    @pl.when(i == pl.num_programs(0) - 1)
    def _():
        _group_copies(tab_ref, ne, make(slot), False)


def _dispatch(h, route, table, n_rows):
    n, d = h.shape
    tm = route.shape[2]
    rows = jnp.zeros((n_rows, d), h.dtype)
    return pl.pallas_call(
        _dispatch_kernel,
        grid=(n // tm,),
        in_specs=[pl.BlockSpec((1, 1, table.shape[2]), lambda i: (i, 0, 0),
                               memory_space=pltpu.SMEM),
                  pl.BlockSpec((1, 1, table.shape[2]), lambda i: (jnp.maximum(i - 1, 0), 0, 0),
                               memory_space=pltpu.SMEM),
                  pl.BlockSpec((1, 8, tm), lambda i: (i, 0, 0)),
                  pl.BlockSpec((tm, d), lambda i: (i, 0)),
                  pl.BlockSpec(memory_space=pl.ANY)],
        out_specs=pl.BlockSpec(memory_space=pl.ANY),
        out_shape=jax.ShapeDtypeStruct((n_rows, d), h.dtype),
        scratch_shapes=[pltpu.VMEM((2, _sorted_rows(tm), d), F32), pltpu.SemaphoreType.DMA((2,))],
        input_output_aliases={4: 0},
        compiler_params=pltpu.CompilerParams(dimension_semantics=("arbitrary",),
                                             vmem_limit_bytes=VMEM_LIMIT, has_side_effects=True),
        name="moe_dispatch",
    )(table, table, route, h, rows)


def _experts_kernel(be_ref, nb_ref, x_ref, wg_ref, wu_ref, wd_ref, y_ref):
    del be_ref

    @pl.when(pl.program_id(0) < nb_ref[0])
    def _():
        x = x_ref[...].astype(BF16)
        f = wg_ref.shape[2]
        step = f // 7
        acc = None
        for c0 in range(0, f, step):
            gate = jnp.dot(x, wg_ref[0, :, c0:c0 + step], preferred_element_type=F32)
            up = jnp.dot(x, wu_ref[0, :, c0:c0 + step], preferred_element_type=F32)
            part = _dot(_silu(gate) * up, wd_ref[0, c0:c0 + step, :])
            acc = part if acc is None else acc + part
        y_ref[...] = acc

    @pl.when(pl.program_id(0) >= nb_ref[0])
    def _():
        y_ref[...] = jnp.zeros_like(y_ref)


def _experts(rows, block_e, n_used, wg, wu, wd):
    n_rows, d = rows.shape
    rb = MOE_ROW_BLOCK
    f = wg.shape[2]
    wspec = lambda shape: pl.BlockSpec((1,) + shape, lambda i, be, nb: (be[i], 0, 0),
                                       pipeline_mode=pl.Buffered(1))
    return pl.pallas_call(
        _experts_kernel,
        grid_spec=pltpu.PrefetchScalarGridSpec(
            num_scalar_prefetch=2,
            grid=(n_rows // rb,),
            in_specs=[pl.BlockSpec((rb, d), lambda i, be, nb: (i, 0)),
                      wspec((d, f)), wspec((d, f)), wspec((f, d))],
            out_specs=pl.BlockSpec((rb, d), lambda i, be, nb: (i, 0))),
        out_shape=jax.ShapeDtypeStruct((n_rows, d), F32),
        compiler_params=_params(("arbitrary",)),
        name="moe_experts",
    )(block_e, n_used, rows, wg, wu, wd)


def _combine_kernel(tab_ref, next_tab_ref, route_ref, y_hbm, x_ref, mod_ref, lnf_ref, out_ref,
                    sorted_ref, sem, *, final_norm):
    step = pl.program_id(0) * pl.num_programs(1) + pl.program_id(1)
    slot = step % 2
    ne = tab_ref.shape[2] // 3
    make = lambda s: (lambda loc, glob: _row_copy(y_hbm.at[glob], sorted_ref.at[s, loc],
                                                  sem.at[s]))

    @pl.when(step == 0)
    def _():
        sorted_ref[...] = jnp.zeros_like(sorted_ref)
        _group_copies(tab_ref, ne, make(slot), True)

    @pl.when(step + 1 < pl.num_programs(0) * pl.num_programs(1))
    def _():
        _group_copies(next_tab_ref, ne, make(1 - slot), True)

    _group_copies(tab_ref, ne, make(slot), False)
    pos = route_ref[0, 4:6, :].astype(jnp.int32)
    wts = route_ref[0, 2:4, :]
    j = lax.broadcasted_iota(jnp.int32, (sorted_ref.shape[1], x_ref.shape[1]), 0)
    q_t = (jnp.where(j == pos[0:1, :], wts[0:1, :], 0.0)
           + jnp.where(j == pos[1:2, :], wts[1:2, :], 0.0))
    q_hi, q_lo = _split_bf16(q_t, 2)
    z = sorted_ref[slot].astype(BF16)
    f = _dot_tn(q_hi, z) + _dot_tn(q_lo, z)
    out = x_ref[0] + mod_ref[0, 5:6, :] * f
    if final_norm:
        out = _norm_mod(out, lnf_ref[...], 0.0, 0.0)
    out_ref[0] = out


def _combine(y_rows, route, table, x, mod, ln_final, final_norm):
    b, t, d = x.shape
    tm = route.shape[2]
    nt = t // tm
    return pl.pallas_call(
        functools.partial(_combine_kernel, final_norm=final_norm),
        grid=(b, nt),
        in_specs=[pl.BlockSpec((1, 1, table.shape[2]), lambda i, j: (i * nt + j, 0, 0),
                               memory_space=pltpu.SMEM),
                  pl.BlockSpec((1, 1, table.shape[2]),
                               lambda i, j: (jnp.minimum(i * nt + j + 1, b * nt - 1), 0, 0),
                               memory_space=pltpu.SMEM),
                  pl.BlockSpec((1, 8, tm), lambda i, j: (i * nt + j, 0, 0)),
                  pl.BlockSpec(memory_space=pl.ANY),
                  pl.BlockSpec((1, tm, d), lambda i, j: (i, j, 0)),
                  pl.BlockSpec((1, N_MOD, d), lambda i, j: (i, 0, 0)),
                  _const_spec(ln_final.shape)],
        out_specs=pl.BlockSpec((1, tm, d), lambda i, j: (i, j, 0)),
        out_shape=jax.ShapeDtypeStruct((b, t, d), F32),
        scratch_shapes=[pltpu.VMEM((2, _sorted_rows(tm), d), F32), pltpu.SemaphoreType.DMA((2,))],
        compiler_params=_params(("arbitrary", "arbitrary")),
        name="moe_combine",
    )(table, table, route, y_rows, x, mod, ln_final)


def _ffn_moe(x, mod, ln, ln_final, router, wg, wu, wd, final_norm):
    b, t, d = x.shape
    n = b * t
    ne = router.shape[1]
    rb = MOE_ROW_BLOCK
    h, route, cnt = _route(x, mod, ln, router)
    nt, tm = route.shape[1], route.shape[3]
    tiles = b * nt
    cnt = cnt[..., 0].reshape(tiles, 2 * ne).astype(jnp.int32)
    size, local = cnt[:, :ne], cnt[:, ne:]
    totals = jnp.sum(size, axis=0)
    padded = ((totals + rb - 1) // rb) * rb
    padded_end = jnp.cumsum(padded)
    base = (padded_end - padded)[None, :] + jnp.cumsum(size, axis=0) - size
    table = jnp.concatenate([local, size // GROUP_ALIGN, base], axis=1).reshape(tiles, 1, 3 * ne)
    route = route.reshape(tiles, 8, tm)
    n_blocks = -(-(n * 2 + tiles * ne * (GROUP_ALIGN - 1)) // rb) + ne
    block_e = jnp.minimum(
        jnp.searchsorted(padded_end, jnp.arange(n_blocks, dtype=jnp.int32) * rb, side="right"),
        ne - 1).astype(jnp.int32)
    n_used = (padded_end[-1:] // rb).astype(jnp.int32)
    rows = _dispatch(h.reshape(n, d), route, table, n_blocks * rb)
    y_rows = _experts(rows, block_e, n_used, wg, wu, wd)
    return _combine(y_rows, route, table, x, mod, ln_final, final_norm)


def _block_diag(m, n):
    return jnp.kron(jnp.eye(n, dtype=m.dtype), m)


def _layer_params(i, w_in, conv_w, decay_w0, decay_w2, iclr_a0, iclr_a2, gate_g2, k_k, k_a, r_k,
                  lnx_w, lnx_b, w_fnet_out, w_rwkv_out, w_o):
    w = RWKV_WIDTH
    lora = decay_w2.shape[2]
    tail0 = FNET_WIDTH + 3 * w

    def reorder_tail(m):
        pad = jnp.zeros(m.shape[:-1] + (LANES - lora,), m.dtype)
        return jnp.concatenate([m[..., :3 * lora], pad, m[..., 3 * lora:]], axis=-1)

    row = lambda v: v.reshape(1, -1)
    hi_lo = lambda m: jnp.stack(_split_bf16(m, 2))
    wi = w_in[i]
    rwkv_cols = conv_w.shape[2]
    zeros = jnp.zeros((lora, w), F32)
    return {
        "w_a": wi[:, :FNET_WIDTH].astype(BF16),
        "w_main": wi[:, FNET_WIDTH:tail0].astype(BF16),
        "w_tail": reorder_tail(wi[:, tail0:FNET_WIDTH + rwkv_cols]).astype(BF16),
        "w_gates": wi[:, FNET_WIDTH + rwkv_cols:].astype(BF16),
        "conv_main": conv_w[i][:, :3 * w],
        "conv_tail": reorder_tail(conv_w[i][:, 3 * w:]),
        "decay_w0": decay_w0[i].reshape(1, 2 * w),
        "decay_w2": hi_lo(jnp.concatenate(
            [jnp.concatenate([decay_w2[i, 0], zeros], axis=1),
             jnp.concatenate([zeros, decay_w2[i, 1]], axis=1)], axis=0)),
        "iclr_a0": row(iclr_a0[i]),
        "iclr_a2": hi_lo(jnp.concatenate([iclr_a2[i], jnp.zeros((LANES - lora, w), F32)], axis=0)),
        "gate_g2": hi_lo(gate_g2[i]),
        "k_k": row(k_k[i]), "k_a": row(k_a[i]), "r_k": row(r_k[i]),
        "lnx_w": row(lnx_w[i]), "lnx_b": row(lnx_b[i]),
        "head_sum": _block_diag(jnp.ones((HEAD_DIM, HEAD_DIM), BF16), LANES // HEAD_DIM),
        "head_mean": _block_diag(jnp.full((HEAD_DIM, HEAD_DIM), 1.0 / HEAD_DIM, BF16),
                                 LANES // HEAD_DIM),
        "w_fnet_out": w_fnet_out[i].astype(BF16),
        "w_rwkv_out": w_rwkv_out[i].astype(BF16),
        "w_o": w_o[i].astype(BF16),
    }


def _channel_dft():
    k = jnp.arange(HEAD_DIM, dtype=jnp.int32)
    ang = ((k[:, None] * k[None, :]) % HEAD_DIM).astype(F32) * (2.0 * math.pi / HEAD_DIM)
    groups = FNET_WIDTH // HEAD_DIM
    return jnp.concatenate([_block_diag(jnp.cos(ang), groups), _block_diag(jnp.sin(ang), groups)],
                           axis=1).astype(BF16)


def _token_mixer(x, mod, ln, p, cs, dft_tab, ffn=None):
    zc, zs, um, ut, gates = _inproj(x, mod, ln, p["w_a"], p["w_main"], p["w_tail"], p["w_gates"], cs)
    yf = _fnet(zc, zs, dft_tab)
    r, k, v, a, b, e_f, e_b, g, bonus = _prep(um, ut, p)
    o = _wkv(r, k, v, a, b, e_b, True, add=_wkv(r, k, v, a, b, e_f, False))
    return _mixout(o, bonus, g, yf, gates, x, mod, p, ffn)


def kernel(x_prompt, x_sample, c_prompt, c_sample, w_ada, b_ada, ln_mix, w_in, conv_w, decay_w0, decay_w2, iclr_a0, iclr_a2, gate_g2, k_k, k_a, r_k, lnx_w, lnx_b, w_fnet_out, w_rwkv_out, w_o, ln_ffn, ff_w_gate, ff_w_up, ff_w_down, router_w, moe_w_gate, moe_w_up, moe_w_down, ln_final):
    depth, d = ln_mix.shape
    streams = [x_prompt, x_sample]
    nb = [x.shape[0] for x in streams]
    mod_all = _ada(jnp.concatenate([c_prompt, c_sample], axis=0), w_ada, b_ada)
    cs = _channel_dft()
    dfts = {}
    for x in streams:
        if x.shape[1] not in dfts:
            dfts[x.shape[1]] = _dft_tables(x.shape[1])
    lnf = ln_final.reshape(1, d)
    for i in range(depth):
        p = _layer_params(i, w_in, conv_w, decay_w0, decay_w2, iclr_a0, iclr_a2, gate_g2, k_k, k_a,
                          r_k, lnx_w, lnx_b, w_fnet_out, w_rwkv_out, w_o)
        j = i // 2
        final = i == depth - 1
        if i % 2 == 0:
            ffw = (ff_w_gate[j].astype(BF16), ff_w_up[j].astype(BF16), ff_w_down[j].astype(BF16))
        else:
            ffw = (moe_w_gate[j].astype(BF16), moe_w_up[j].astype(BF16), moe_w_down[j].astype(BF16))
            router = router_w[j]
        off = 0
        for s, x in enumerate(streams):
            mod = mod_all[i, off:off + nb[s]].reshape(nb[s], N_MOD, d)
            off += nb[s]
            ln1, ln2 = ln_mix[i].reshape(1, d), ln_ffn[i].reshape(1, d)
            if i % 2 == 0:
                x = _token_mixer(x, mod, ln1, p, cs, dfts[x.shape[1]], (ln2, lnf, *ffw, final))
            else:
                x = _token_mixer(x, mod, ln1, p, cs, dfts[x.shape[1]])
                x = _ffn_moe(x, mod, ln2, lnf, router, *ffw, final)
            streams[s] = x
    return tuple(streams)
```

```python
import functools
import math

import jax
import jax.numpy as jnp
from jax import lax
from jax.experimental import pallas as pl
from jax.experimental.pallas import tpu as pltpu

F32 = jnp.float32
BF16 = jnp.bfloat16
HIGHEST = lax.Precision.HIGHEST

HEAD_DIM = 64
LANES = 128
FNET_WIDTH = 256
RWKV_WIDTH = 768
LORA_TAIL = 384
N_MOD = 6
N_EXPERTS = 8
RMS_EPS = 1e-6
GN_EPS = 64e-5
WKV_CHUNK = 128
MOE_ROW_BLOCK = 512
GROUP_ALIGN = 8
GROUP_CHUNK = 64
VMEM_LIMIT = 56 * 1024 * 1024


def _params(sem, vmem=VMEM_LIMIT):
    return pltpu.CompilerParams(dimension_semantics=sem, vmem_limit_bytes=vmem)


def _const_spec(shape):
    nd = len(shape)
    return pl.BlockSpec(shape, lambda *_: (0,) * nd, pipeline_mode=pl.Buffered(1))


def _dot(a, b):
    return jnp.dot(a.astype(BF16), b.astype(BF16), preferred_element_type=F32)


def _dot_nt(a, b):
    return lax.dot_general(a.astype(BF16), b.astype(BF16), (((1,), (1,)), ((), ())),
                           preferred_element_type=F32)


def _dot_tn(a, b):
    return lax.dot_general(a.astype(BF16), b.astype(BF16), (((0,), (0,)), ((), ())),
                           preferred_element_type=F32)


def _split_bf16(x, parts):
    out, rem = [], x
    for _ in range(parts):
        p = rem.astype(BF16)
        out.append(p)
        rem = rem - p.astype(F32)
    return out


def _dot_exact_rhs(x, m, parts):
    acc = None
    for p in _split_bf16(x, parts):
        t = jnp.dot(p, m, preferred_element_type=F32)
        acc = t if acc is None else acc + t
    return acc


def _dot_f32(a, b):
    return jnp.dot(a, b, precision=HIGHEST, preferred_element_type=F32)


def _dot_3pass(a, w_ref):
    ah, al = _split_bf16(a, 2)
    d = lambda x, y: jnp.dot(x, y, preferred_element_type=F32)
    return d(ah, w_ref[0]) + (d(al, w_ref[0]) + d(ah, w_ref[1]))


def _norm_mod(x, gain, shift, scale):
    ms = jnp.mean(x * x, axis=-1, keepdims=True)
    return x * lax.rsqrt(ms + RMS_EPS) * gain * (1.0 + scale) + shift


def _silu(x):
    return x * jax.nn.sigmoid(x)


def _head_reduce(x, m, parts=2):
    cols = [_dot_exact_rhs(x[:, j:j + LANES], m, parts) for j in range(0, x.shape[1], LANES)]
    return jnp.concatenate(cols, axis=1)


def _row_tile(t, want):
    tm = min(t, want)
    assert t % tm == 0
    return tm


def _sub_blocks(rows, size=512):
    size = min(size, rows)
    assert rows % size == 0
    return [slice(r, r + size) for r in range(0, rows, size)]


def _col_chunks(width, size):
    return [slice(c, min(c + size, width)) for c in range(0, width, size)]


def _ada_kernel(c_ref, w_ref, b_ref, o_ref):
    o_ref[0] = _dot_f32(_silu(c_ref[...]), w_ref[0]) + b_ref[0]


def _ada(c_all, w_ada, b_ada):
    depth, d, n = w_ada.shape
    bc = c_all.shape[0]
    tn = n // 4
    return pl.pallas_call(
        _ada_kernel,
        grid=(depth, n // tn),
        in_specs=[pl.BlockSpec((bc, d), lambda l, j: (0, 0)),
                  pl.BlockSpec((1, d, tn), lambda l, j: (l, 0, j)),
                  pl.BlockSpec((1, 1, tn), lambda l, j: (l, 0, j))],
        out_specs=pl.BlockSpec((1, bc, tn), lambda l, j: (l, 0, j)),
        out_shape=jax.ShapeDtypeStruct((depth, bc, n), F32),
        compiler_params=_params(("parallel", "parallel")),
        name="ada",
    )(c_all, w_ada, b_ada.reshape(depth, 1, n))


def _inproj_kernel(x_ref, mod_ref, ln_ref, wa_ref, wm_ref, wt_ref, wg_ref, cs_ref,
                   zc_ref, zs_ref, um_ref, ut_ref, g_ref):
    for rows in _sub_blocks(x_ref.shape[1]):
        h = _norm_mod(x_ref[0, rows, :], ln_ref[...], mod_ref[0, 0:1, :],
                      mod_ref[0, 1:2, :]).astype(BF16)
        ua = jnp.dot(h, wa_ref[...], preferred_element_type=F32)
        zz = jnp.dot(ua.astype(BF16), cs_ref[...], preferred_element_type=F32)
        zc_ref[0, rows, :] = zz[:, :FNET_WIDTH]
        zs_ref[0, rows, :] = zz[:, FNET_WIDTH:]
        for cols in _col_chunks(wm_ref.shape[1], 1280):
            um_ref[0, rows, cols] = jnp.dot(
                h, wm_ref[:, cols], preferred_element_type=F32).astype(BF16)
        ut_ref[0, rows, :] = jnp.dot(h, wt_ref[...], preferred_element_type=F32)
        for cols in _col_chunks(wg_ref.shape[1], 1024):
            g_ref[0, rows, cols] = jax.nn.sigmoid(
                jnp.dot(h, wg_ref[:, cols], preferred_element_type=F32)).astype(BF16)


def _inproj(x, mod, ln, wa, wm, wt, wg, cs):
    b, t, d = x.shape
    tm = _row_tile(t, 1024)
    row = lambda w: pl.BlockSpec((1, tm, w), lambda i, j: (i, j, 0))
    outs = [(FNET_WIDTH, F32), (FNET_WIDTH, F32), (wm.shape[1], BF16), (wt.shape[1], F32),
            (wg.shape[1], BF16)]
    return pl.pallas_call(
        _inproj_kernel,
        grid=(b, t // tm),
        in_specs=[row(d), pl.BlockSpec((1, N_MOD, d), lambda i, j: (i, 0, 0)),
                  _const_spec(ln.shape), _const_spec(wa.shape), _const_spec(wm.shape),
                  _const_spec(wt.shape), _const_spec(wg.shape), _const_spec(cs.shape)],
        out_specs=[row(w) for w, _ in outs],
        out_shape=[jax.ShapeDtypeStruct((b, t, w), dt) for w, dt in outs],
        compiler_params=_params(("parallel", "parallel")),
        name="inproj",
    )(x, mod, ln, wa, wm, wt, wg, cs)


FNET_T2 = 64
FNET_ROWS = 32


def _rows_at(ref, j):
    return ref[0, :, j:j + 1, :].reshape(ref.shape[1], ref.shape[3])


def _fnet_stage1_kernel(d_ref, zc_ref, zs_ref, twc_ref, tws_ref, gr_ref, gi_ref):
    t2, w = zc_ref.shape[1], zc_ref.shape[3]
    for j in range(zc_ref.shape[2]):
        data = jnp.concatenate([_rows_at(zc_ref, j), _rows_at(zs_ref, j)], axis=0)
        g = _dot(d_ref[...], data)
        g_r, g_i = g[:t2], g[t2:]
        c = jnp.concatenate([twc_ref[j]] * (w // LANES), axis=1)
        s = jnp.concatenate([tws_ref[j]] * (w // LANES), axis=1)
        gr_ref[0, j] = g_r * c + g_i * s
        gi_ref[0, j] = g_i * c - g_r * s


def _fnet_stage2_kernel(d_ref, gr_ref, gi_ref, o_ref, *, scale):
    t1, w = gr_ref.shape[1], gr_ref.shape[3]
    for j in range(gr_ref.shape[2]):
        data = jnp.concatenate([_rows_at(gr_ref, j), _rows_at(gi_ref, j)], axis=0)
        o_ref[0, :, j:j + 1, :] = (_dot(d_ref[...], data) * scale).reshape(t1, 1, w)


def _dft_tables(t):
    t2 = FNET_T2
    t1 = t // t2
    assert t1 * t2 == t

    def cos_sin(rows, cols, n):
        k = (lax.iota(jnp.int32, rows)[:, None] * lax.iota(jnp.int32, cols)[None, :]) % n
        ang = k.astype(F32) * (2.0 * math.pi / n)
        return jnp.cos(ang), jnp.sin(ang)

    c2, s2 = cos_sin(t2, t2, t2)
    c1, s1 = cos_sin(t1, t1, t1)
    twc, tws = cos_sin(t1, t2, t)
    lanes = lambda m: jnp.broadcast_to(m[:, :, None], (t1, t2, LANES))
    return {
        "stage1": jnp.block([[c2, -s2], [-s2, -c2]]).astype(BF16),
        "stage2": jnp.concatenate([c1, s1], axis=1).astype(BF16),
        "twc": lanes(twc), "tws": lanes(tws),
    }


def _fnet(zc, zs, tab):
    b, t, w = zc.shape
    t2 = FNET_T2
    t1 = t // t2
    n1 = min(FNET_ROWS, t1)
    in_spec = pl.BlockSpec((1, t2, n1, w), lambda i, j: (i, 0, j, 0))
    tw_spec = pl.BlockSpec((n1, t2, LANES), lambda i, j: (j, 0, 0))
    g_shape = jax.ShapeDtypeStruct((b, t1, t2, w), F32)
    g_r, g_i = pl.pallas_call(
        _fnet_stage1_kernel,
        grid=(b, t1 // n1),
        in_specs=[_const_spec(tab["stage1"].shape), in_spec, in_spec, tw_spec, tw_spec],
        out_specs=[pl.BlockSpec((1, n1, t2, w), lambda i, j: (i, j, 0, 0))] * 2,
        out_shape=[g_shape, g_shape],
        compiler_params=_params(("parallel", "parallel")),
        name="fnet_stage1",
    )(tab["stage1"], zc.reshape(b, t2, t1, w), zs.reshape(b, t2, t1, w), tab["twc"], tab["tws"])
    n2 = min(FNET_ROWS, t2)
    spec = pl.BlockSpec((1, t1, n2, w), lambda i, j: (i, 0, j, 0))
    y = pl.pallas_call(
        functools.partial(_fnet_stage2_kernel, scale=1.0 / math.sqrt(t * HEAD_DIM)),
        grid=(b, t2 // n2),
        in_specs=[_const_spec(tab["stage2"].shape), spec, spec],
        out_specs=spec,
        out_shape=jax.ShapeDtypeStruct((b, t1, t2, w), F32),
        compiler_params=_params(("parallel", "parallel")),
        name="fnet_stage2",
    )(tab["stage2"], g_r, g_i)
    return y.reshape(b, t, w)


def _conv3(main, prev_row, next_row, w):
    tm = main.shape[0]
    ri = lax.broadcasted_iota(jnp.int32, main.shape, 0)
    up = jnp.where(ri == 0, prev_row, pltpu.roll(main, 1, 0))
    dn = jnp.where(ri == tm - 1, next_row, pltpu.roll(main, tm - 1, 0))
    return w[0:1] * up + w[1:2] * main + w[2:3] * dn


def _prep_kernel(um_ref, ump_ref, umn_ref, ut_ref, utp_ref, utn_ref, cwm_ref, cwt_ref,
                 w0_ref, w2_ref, a0_ref, a2_ref, g2_ref, kk_ref, ka_ref, rk_ref, hs_ref, shift_ref,
                 r_ref, k_ref, v_ref, a_ref, b_ref, ef_ref, eb_ref, g_ref, bonus_ref):
    j = pl.program_id(1)
    first = j == 0
    last = j == pl.num_programs(1) - 1
    hp = ump_ref.shape[1]

    def conv_main(c0, c1):
        main = um_ref[0, :, c0:c1]
        tm = main.shape[0]
        shifted = jnp.dot(shift_ref[...], main, preferred_element_type=F32)
        sub = 8
        ri = lax.broadcasted_iota(jnp.int32, (sub, main.shape[1]), 0)
        prev = jnp.where(first, 0.0, ump_ref[0, hp - 1:hp, c0:c1].astype(F32))
        nxt = jnp.where(last, 0.0, umn_ref[0, 0:1, c0:c1].astype(F32))
        up = jnp.concatenate([jnp.where(ri == 0, prev, shifted[:sub]), shifted[sub:tm]], axis=0)
        dn = jnp.concatenate([shifted[tm:2 * tm - sub],
                              jnp.where(ri == sub - 1, nxt, shifted[2 * tm - sub:])], axis=0)
        w = cwm_ref[:, c0:c1]
        return w[0:1] * up + w[1:2] * main.astype(F32) + w[2:3] * dn

    w = RWKV_WIDTH
    r = conv_main(0, w)
    k = conv_main(w, 2 * w)
    v = conv_main(2 * w, 3 * w)
    tp = utp_ref.shape[1]
    tail = _conv3(ut_ref[0], jnp.where(first, 0.0, utp_ref[0, tp - 1:tp, :]),
                  jnp.where(last, 0.0, utn_ref[0, 0:1, :]), cwt_ref[...])
    xw, xa, xg = tail[:, 0:LANES], tail[:, LANES:2 * LANES], tail[:, 2 * LANES:3 * LANES]

    e = jax.nn.sigmoid(w0_ref[...] + _dot_3pass(jnp.tanh(xw), w2_ref)) * math.exp(-0.5)
    ef_ref[0] = e[:, :w]
    eb_ref[0] = e[:, w:]
    a = jax.nn.sigmoid(a0_ref[...] + _dot_3pass(xa, a2_ref))
    g_ref[0] = _dot_3pass(jax.nn.sigmoid(xg), g2_ref).astype(BF16)

    kk = k * kk_ref[...]
    kk = kk * lax.rsqrt(_head_reduce(kk * kk, hs_ref[...], parts=1) + 1e-12)
    k = k * (1.0 + (a - 1.0) * ka_ref[...])
    r_ref[0] = r.astype(BF16)
    k_ref[0] = k.astype(BF16)
    v_ref[0] = v.astype(BF16)
    a_ref[0] = (-kk).astype(BF16)
    b_ref[0] = (kk * a).astype(BF16)
    bonus_ref[0] = (_head_reduce(r * k * rk_ref[...], hs_ref[...]) * v).astype(BF16)


def _prep(um, ut, p):
    b, t, wm = um.shape
    wt = ut.shape[2]
    tm = _row_tile(t, 256)
    hm, ht = 16, 8
    nm, nt = tm // hm, tm // ht
    main = lambda w: pl.BlockSpec((1, tm, w), lambda i, j: (i, j, 0))
    prev = lambda h, n, w: pl.BlockSpec((1, h, w), lambda i, j: (i, jnp.maximum(j * n - 1, 0), 0))
    nxt = lambda h, n, w: pl.BlockSpec(
        (1, h, w), lambda i, j: (i, jnp.minimum((j + 1) * n, t // h - 1), 0))
    consts = [p["conv_main"], p["conv_tail"], p["decay_w0"], p["decay_w2"], p["iclr_a0"],
              p["iclr_a2"], p["gate_g2"], p["k_k"], p["k_a"], p["r_k"], p["head_sum"],
              jnp.concatenate([jnp.eye(tm, k=-1, dtype=BF16), jnp.eye(tm, k=1, dtype=BF16)], axis=0)]
    outs = [BF16] * 5 + [F32, F32, BF16, BF16]
    return pl.pallas_call(
        _prep_kernel,
        grid=(b, t // tm),
        in_specs=[main(wm), prev(hm, nm, wm), nxt(hm, nm, wm),
                  main(wt), prev(ht, nt, wt), nxt(ht, nt, wt)]
                 + [_const_spec(c.shape) for c in consts],
        out_specs=[main(RWKV_WIDTH) for _ in outs],
        out_shape=[jax.ShapeDtypeStruct((b, t, RWKV_WIDTH), dt) for dt in outs],
        compiler_params=_params(("parallel", "parallel")),
        name="rwkv_prep",
    )(um, um, um, ut, ut, ut, *consts)


def _wkv_kernel(r_ref, k_ref, v_ref, a_ref, b_ref, e_ref, *rest, reverse):
    *add_refs, o_ref, s_ref = rest
    add_ref = add_refs[0] if add_refs else None
    c = r_ref.shape[1]
    pairs = range(s_ref.shape[0])
    heads = range(LANES // HEAD_DIM)
    chains = [(p, h) for p in pairs for h in heads]

    @pl.when(pl.program_id(1) == 0)
    def _():
        s_ref[...] = jnp.zeros_like(s_ref)

    row = lax.broadcasted_iota(jnp.int32, (c, c), 0)
    col = lax.broadcasted_iota(jnp.int32, (c, c), 1)
    if reverse:
        incl, strict = col >= row, col > row
    else:
        incl, strict = col <= row, col < row
    tri = jnp.where(incl, 1.0, 0.0).astype(BF16)
    end = 0 if reverse else c - 1
    lane = lax.broadcasted_iota(jnp.int32, (1, LANES), 1)
    in_head = [(lane >= h * HEAD_DIM) & (lane < (h + 1) * HEAD_DIM) for h in heads]
    same_head = (lax.broadcasted_iota(jnp.int32, (LANES, LANES), 0) // HEAD_DIM
                 == lax.broadcasted_iota(jnp.int32, (LANES, LANES), 1) // HEAD_DIM)

    per_row = r_ref.shape[2] // LANES

    def tile(ref, p):
        return ref[p // per_row, :, (p % per_row) * LANES:(p % per_row + 1) * LANES]

    tri2 = jnp.concatenate([tri, tri], axis=1)

    def running_sum(e):
        return jnp.dot(tri2, jnp.concatenate(_split_bf16(e, 2), axis=0), preferred_element_type=F32)

    cum = [running_sum(tile(e_ref, p)) for p in pairs]
    total = [x[end:end + 1, :] for x in cum]
    ref = [x[c // 2:c // 2 + 1, :] for x in cum]
    dl = [cum[p] - ref[p] for p in pairs]
    g_inv = [jnp.exp(dl[p]) for p in pairs]
    at = [tile(a_ref, p).astype(F32) * jnp.exp(tile(e_ref, p) - dl[p]) for p in pairs]
    rt = [tile(r_ref, p).astype(F32) * jnp.exp(-dl[p]) for p in pairs]
    bt = [tile(b_ref, p).astype(F32) * g_inv[p] for p in pairs]
    kt = [tile(k_ref, p).astype(F32) * g_inv[p] for p in pairs]
    g_end = [jnp.exp(ref[p] - total[p]) for p in pairs]
    bh = [(bt[p] * g_end[p]).astype(BF16) for p in pairs]
    kh = [(kt[p] * g_end[p]).astype(BF16) for p in pairs]
    bk = [jnp.concatenate([bt[p], kt[p]], axis=0).astype(BF16) for p in pairs]
    ar = [jnp.concatenate([at[p], rt[p]], axis=0) for p in pairs]
    vb = [tile(v_ref, p) for p in pairs]

    prod = [_dot_nt(jnp.where(in_head[h], ar[p], 0.0), bk[p]) for p, h in chains]
    l_ab = [jnp.where(strict, x[:c, :c], 0.0) for x in prod]
    l_ak = [jnp.where(strict, x[:c, c:], 0.0) for x in prod]
    m_rb = [jnp.where(incl, x[c:, :c], 0.0).astype(BF16) for x in prod]
    m_rk = [jnp.where(incl, x[c:, c:], 0.0).astype(BF16) for x in prod]
    del prod
    eye = jnp.where(row == col, 1.0, 0.0)
    t = [eye + jnp.where((row >> 1) == (col >> 1), x, 0.0) for x in l_ab]
    nch = range(len(chains))
    for sh in range(1, (c - 1).bit_length()):
        m = 1 << sh
        joins = ((row >> (sh + 1)) == (col >> (sh + 1))) & ((row >> sh) != (col >> sh))
        e_k = [jnp.where(joins, l_ab[i], 0.0) for i in nch]
        if m < 8:
            et = [_dot(e_k[i], t[i]) for i in nch]
            t = [t[i] + _dot(t[i], et[i]) for i in nch]
            continue
        live = [slice(b0 + (0 if reverse else m), b0 + (m if reverse else 2 * m))
                for b0 in range(0, c, 2 * m)]

        def take(v):
            return jnp.concatenate([v[rs] for rs in live], axis=0)

        def spread(v):
            zero = jnp.zeros((m, v.shape[1]), v.dtype)
            parts = []
            for j in range(len(live)):
                blk = v[j * m:(j + 1) * m]
                parts += [blk, zero] if reverse else [zero, blk]
            return jnp.concatenate(parts, axis=0)

        et = [_dot(take(e_k[i]), t[i]) for i in nch]
        upd = [_dot(take(t[i]), spread(et[i])) for i in nch]
        t = [t[i] + spread(upd[i]) for i in nch]
    lakv = [jnp.dot(l_ak[i].astype(BF16), vb[p], preferred_element_type=F32)
            for i, (p, h) in enumerate(chains)]
    x = [_dot(t[i], jnp.concatenate([at[p], lakv[i]], axis=1)) for i, (p, h) in enumerate(chains)]
    x_a = [v[:, :LANES] for v in x]
    x_u = [v[:, LANES:] for v in x]
    y_a = [jnp.dot(m_rb[i], x_a[i].astype(BF16), preferred_element_type=F32) for i in nch]
    y_o = [jnp.dot(jnp.concatenate([m_rb[i], m_rk[i]], axis=1),
                   jnp.concatenate([x_u[i].astype(BF16), vb[p]], axis=0),
                   preferred_element_type=F32) for i, (p, h) in enumerate(chains)]

    def pick(vals, p):
        out = vals[p * len(heads)]
        for h in heads[1:]:
            out = jnp.where(in_head[h], vals[p * len(heads) + h], out)
        return out

    a_solved = [pick(x_a, p) for p in pairs]
    u = [pick(x_u, p) for p in pairs]
    r_hat = [rt[p] + pick(y_a, p) for p in pairs]
    o_intra = [pick(y_o, p) for p in pairs]
    s = [s_ref[p] for p in pairs]
    s_hat = [(s[p] * jnp.exp(-ref[p])).astype(BF16) for p in pairs]
    out = [_dot_nt(r_hat[p], s_hat[p]) + o_intra[p] for p in pairs]
    g_t = [jnp.where(same_head, _dot_tn(a_solved[p], bh[p]), 0.0) for p in pairs]
    h_t = [jnp.where(same_head,
                     _dot_tn(jnp.concatenate([u[p].astype(BF16), vb[p]], axis=0),
                             jnp.concatenate([bh[p], kh[p]], axis=0)), 0.0) for p in pairs]
    s_new = [s[p] * jnp.exp(-total[p]) + _dot(s_hat[p], g_t[p]) + h_t[p] for p in pairs]
    for p in pairs:
        if add_ref is not None:
            out[p] = out[p] + tile(add_ref, p).astype(F32)
        o_ref[p // per_row, :, (p % per_row) * LANES:(p % per_row + 1) * LANES] = (
            out[p].astype(o_ref.dtype))
        s_ref[p] = s_new[p]


def _wkv(r, k, v, a, b, e, reverse, add=None):
    bsz, t, w = r.shape
    c = min(WKV_CHUNK, t)
    nc = t // c
    rows = math.gcd(bsz, 2)
    if reverse:
        idx = lambda i, j: (i, nc - 1 - j, 0)
    else:
        idx = lambda i, j: (i, j, 0)
    spec = pl.BlockSpec((rows, c, w), idx)
    return pl.pallas_call(
        functools.partial(_wkv_kernel, reverse=reverse),
        grid=(bsz // rows, nc),
        in_specs=[spec] * (6 if add is None else 7),
        out_specs=spec,
        out_shape=jax.ShapeDtypeStruct((bsz, t, w), BF16),
        scratch_shapes=[pltpu.VMEM((rows * (w // LANES), LANES, LANES), F32)],
        compiler_params=_params(("parallel", "arbitrary")),
        name="wkv_bwd" if reverse else "wkv_fwd",
    )(r, k, v, a, b, e, *(() if add is None else (add,)))


def _ffn_rows(x, mod_ref, ln_ref, lnf_ref, wg_ref, wu_ref, wd_ref, final_norm):
    h = _norm_mod(x, ln_ref[...], mod_ref[0, 3:4, :], mod_ref[0, 4:5, :]).astype(BF16)
    acc = None
    for cols in _col_chunks(wg_ref.shape[1], 768):
        gate = jnp.dot(h, wg_ref[:, cols], preferred_element_type=F32)
        up = jnp.dot(h, wu_ref[:, cols], preferred_element_type=F32)
        part = _dot(_silu(gate) * up, wd_ref[cols, :])
        acc = part if acc is None else acc + part
    out = x + mod_ref[0, 5:6, :] * acc
    if final_norm:
        out = _norm_mod(out, lnf_ref[...], 0.0, 0.0)
    return out


def _mixout_kernel(o_ref, bonus_ref, g_ref, yf_ref, gates_ref, x_ref, mod_ref,
                   lnw_ref, lnb_ref, hm_ref, wfo_ref, wro_ref, wo_ref, *rest, ffn_final):
    *ffn_refs, out_ref = rest
    d = x_ref.shape[2]
    for rows in _sub_blocks(x_ref.shape[1]):
        o = o_ref[0, rows, :].astype(F32)
        dev = o - _head_reduce(o, hm_ref[...])
        var = _head_reduce(dev * dev, hm_ref[...])
        o = dev * lax.rsqrt(var + GN_EPS) * lnw_ref[...] + lnb_ref[...]
        y = (o + bonus_ref[0, rows, :].astype(F32)) * g_ref[0, rows, :].astype(F32)
        y_b = _dot(y, wro_ref[...])
        y_a = _dot(yf_ref[0, rows, :], wfo_ref[...])
        merged = (gates_ref[0, rows, :d].astype(F32) * y_a
                  + gates_ref[0, rows, d:].astype(F32) * y_b)
        x = x_ref[0, rows, :] + mod_ref[0, 2:3, :] * _dot(merged, wo_ref[...])
        if ffn_refs:
            x = _ffn_rows(x, mod_ref, *ffn_refs, ffn_final)
        out_ref[0, rows, :] = x


def _mixout(o, bonus, g, yf, gates, x, mod, p, ffn=None):
    b, t, d = x.shape
    tm = _row_tile(t, 1024 if ffn is None else 512)
    row = lambda a: pl.BlockSpec((1, tm, a.shape[2]), lambda i, j: (i, j, 0))
    consts = [p["lnx_w"], p["lnx_b"], p["head_mean"], p["w_fnet_out"], p["w_rwkv_out"], p["w_o"]]
    consts += [] if ffn is None else list(ffn[:5])
    acts = [o, bonus, g, yf, gates, x]
    return pl.pallas_call(
        functools.partial(_mixout_kernel, ffn_final=ffn is not None and ffn[5]),
        grid=(b, t // tm),
        in_specs=[row(a) for a in acts] + [pl.BlockSpec((1, N_MOD, d), lambda i, j: (i, 0, 0))]
                 + [_const_spec(c.shape) for c in consts],
        out_specs=pl.BlockSpec((1, tm, d), lambda i, j: (i, j, 0)),
        out_shape=jax.ShapeDtypeStruct((b, t, d), F32),
        compiler_params=_params(("parallel", "parallel")),
        name="mixout" if ffn is None else "mixout_ffn",
    )(*acts, mod, *consts)


def _route_kernel(x_ref, mod_ref, ln_ref, rw_ref, h_ref, route_ref, cnt_ref, *, ne):
    h = _norm_mod(x_ref[0], ln_ref[...], mod_ref[0, 3:4, :], mod_ref[0, 4:5, :])
    h_ref[0] = h
    tm = h.shape[0]
    logits = lax.dot_general(rw_ref[...], h, (((1,), (1,)), ((), ())), precision=HIGHEST,
                             preferred_element_type=F32)
    ei = lax.broadcasted_iota(jnp.int32, logits.shape, 0)
    m1 = jnp.max(logits, axis=0, keepdims=True)
    i1 = jnp.min(jnp.where(logits == m1, ei, ne), axis=0, keepdims=True)
    rest = jnp.where(ei == i1, -jnp.inf, logits)
    m2 = jnp.max(rest, axis=0, keepdims=True)
    i2 = jnp.min(jnp.where(rest == m2, ei, ne), axis=0, keepdims=True)
    t2 = jnp.exp(m2 - m1)
    w1 = 1.0 / (1.0 + t2)
    w2 = t2 / (1.0 + t2)
    oh1 = jnp.where(ei == i1, 1.0, 0.0)
    oh2 = jnp.where(ei == i2, 1.0, 0.0)
    before = (lax.broadcasted_iota(jnp.int32, (tm, tm), 0)
              < lax.broadcasted_iota(jnp.int32, (tm, tm), 1))
    before = jnp.where(before, 1.0, 0.0).astype(BF16)
    cs1 = jnp.dot(oh1.astype(BF16), before, preferred_element_type=F32)
    cs2 = jnp.dot(oh2.astype(BF16), before, preferred_element_type=F32)
    n1 = jnp.sum(oh1, axis=1, keepdims=True)
    n2 = jnp.sum(oh2, axis=1, keepdims=True)
    size = jnp.broadcast_to(jnp.floor((n1 + n2 + (GROUP_ALIGN - 1)) * (1.0 / GROUP_ALIGN))
                            * GROUP_ALIGN, (ne, LANES))
    sub = lax.broadcasted_iota(jnp.int32, size.shape, 0)
    start = size
    for sh in (1, 2, 4):
        start = start + jnp.where(sub >= sh, pltpu.roll(start, sh, 0), 0.0)
    start = start - size
    pos1 = jnp.sum(oh1 * (cs1 + start[:, 0:1]), axis=0, keepdims=True)
    pos2 = jnp.sum(oh2 * (cs2 + n1 + start[:, 0:1]), axis=0, keepdims=True)
    zero = jnp.zeros_like(w1)
    route_ref[0, 0] = jnp.concatenate(
        [i1.astype(F32), i2.astype(F32), w1, w2, pos1, pos2, zero, zero], axis=0)
    cnt_ref[0, 0] = jnp.concatenate([size, start], axis=0)


def _route(x, mod, ln, router):
    b, t, d = x.shape
    tm = _row_tile(t, 512)
    nt = t // tm
    ne = router.shape[1]
    assert ne == 8
    router = router.T
    return pl.pallas_call(
        functools.partial(_route_kernel, ne=ne),
        grid=(b, nt),
        in_specs=[pl.BlockSpec((1, tm, d), lambda i, j: (i, j, 0)),
                  pl.BlockSpec((1, N_MOD, d), lambda i, j: (i, 0, 0)),
                  _const_spec(ln.shape), _const_spec(router.shape)],
        out_specs=[pl.BlockSpec((1, tm, d), lambda i, j: (i, j, 0)),
                   pl.BlockSpec((1, 1, 8, tm), lambda i, j: (i, j, 0, 0)),
                   pl.BlockSpec((1, 1, 2 * ne, LANES), lambda i, j: (i, j, 0, 0))],
        out_shape=[jax.ShapeDtypeStruct((b, t, d), F32),
                   jax.ShapeDtypeStruct((b, nt, 8, tm), F32),
                   jax.ShapeDtypeStruct((b, nt, 2 * ne, LANES), F32)],
        compiler_params=_params(("parallel", "parallel")),
        name="moe_route",
    )(x, mod, ln, router)


def _row_copy(src, dst, sem):
    return pltpu.make_async_copy(src, dst, sem)


def _sorted_rows(tm):
    return 2 * tm + LANES


def _group_copies(tab_ref, ne, make_copy, issue, local_step=1):
    per = GROUP_CHUNK // GROUP_ALIGN
    for e in range(ne):
        local, units, glob = tab_ref[0, 0, e], tab_ref[0, 0, ne + e], tab_ref[0, 0, 2 * ne + e]
        big = units // per

        def chunk(c, _, rows, first):
            off = pl.multiple_of(first + c * rows, GROUP_ALIGN)
            cp = make_copy(pl.ds(pl.multiple_of(local + off * local_step, GROUP_ALIGN), rows),
                           pl.ds(pl.multiple_of(glob + off, GROUP_ALIGN), rows))
            cp.start() if issue else cp.wait()
            return 0

        lax.fori_loop(0, big, functools.partial(chunk, rows=GROUP_CHUNK, first=0), 0)
        lax.fori_loop(0, units - big * per,
                      functools.partial(chunk, rows=GROUP_ALIGN, first=big * GROUP_CHUNK), 0)


def _dispatch_kernel(tab_ref, prev_tab_ref, tail_ref, route_ref, h_ref, rows_ref, sorted_ref,
                     zero_ref, sem):
    i = pl.program_id(0)
    slot = i % 2
    ne = tab_ref.shape[2] // 3
    make = lambda s: (lambda loc, glob: _row_copy(sorted_ref.at[s, loc], rows_ref.at[glob],
                                                  sem.at[s]))
    pos = route_ref[0, 4:6, :].astype(jnp.int32)
    j = lax.broadcasted_iota(jnp.int32, (sorted_ref.shape[1], h_ref.shape[0]), 0)
    pick = jnp.where((j == pos[0:1, :]) | (j == pos[1:2, :]), 1.0, 0.0).astype(BF16)
    sorted_ref[slot] = jnp.dot(pick, h_ref[...].astype(BF16), preferred_element_type=F32)
    _group_copies(tab_ref, ne, make(slot), True)

    @pl.when(i > 0)
    def _():
        _group_copies(prev_tab_ref, ne, make(1 - slot), False)

    @pl.when(i == pl.num_programs(0) - 1)
    def _():
        _group_copies(tab_ref, ne, make(slot), False)
        zero_ref[...] = jnp.zeros_like(zero_ref)
        fill = lambda loc, glob: _row_copy(zero_ref.at[loc], rows_ref.at[glob], sem.at[0])
        _group_copies(tail_ref, tail_ref.shape[2] // 3, fill, True, local_step=0)
        _group_copies(tail_ref, tail_ref.shape[2] // 3, fill, False, local_step=0)


def _dispatch(h, route, table, tails, n_rows):
    n, d = h.shape
    tm = route.shape[2]
    tab_spec = lambda index: pl.BlockSpec((1, 1, table.shape[2]), index, memory_space=pltpu.SMEM)
    return pl.pallas_call(
        _dispatch_kernel,
        grid=(n // tm,),
        in_specs=[tab_spec(lambda i: (i, 0, 0)),
                  tab_spec(lambda i: (jnp.maximum(i - 1, 0), 0, 0)),
                  pl.BlockSpec(tails.shape, lambda i: (0, 0, 0), memory_space=pltpu.SMEM),
                  pl.BlockSpec((1, 8, tm), lambda i: (i, 0, 0)),
                  pl.BlockSpec((tm, d), lambda i: (i, 0))],
        out_specs=pl.BlockSpec(memory_space=pl.ANY),
        out_shape=jax.ShapeDtypeStruct((n_rows, d), h.dtype),
        scratch_shapes=[pltpu.VMEM((2, _sorted_rows(tm), d), F32),
                        pltpu.VMEM((GROUP_CHUNK, d), F32), pltpu.SemaphoreType.DMA((2,))],
        compiler_params=pltpu.CompilerParams(dimension_semantics=("arbitrary",),
                                             vmem_limit_bytes=VMEM_LIMIT, has_side_effects=True),
        name="moe_dispatch",
    )(table, table, tails, route, h)


def _experts_kernel(be_ref, nb_ref, x_ref, wg_ref, wu_ref, wd_ref, y_ref):
    del be_ref

    @pl.when(pl.program_id(0) < nb_ref[0])
    def _():
        x = x_ref[...].astype(BF16)
        f = wg_ref.shape[2]
        step = f // 7
        acc = None
        for c0 in range(0, f, step):
            gate = jnp.dot(x, wg_ref[0, :, c0:c0 + step], preferred_element_type=F32)
            up = jnp.dot(x, wu_ref[0, :, c0:c0 + step], preferred_element_type=F32)
            part = _dot(_silu(gate) * up, wd_ref[0, c0:c0 + step, :])
            acc = part if acc is None else acc + part
        y_ref[...] = acc

    @pl.when(pl.program_id(0) >= nb_ref[0])
    def _():
        y_ref[...] = jnp.zeros_like(y_ref)


def _experts(rows, block_e, n_used, wg, wu, wd):
    n_rows, d = rows.shape
    rb = MOE_ROW_BLOCK
    f = wg.shape[2]
    wspec = lambda shape: pl.BlockSpec((1,) + shape, lambda i, be, nb: (be[i], 0, 0),
                                       pipeline_mode=pl.Buffered(1))
    return pl.pallas_call(
        _experts_kernel,
        grid_spec=pltpu.PrefetchScalarGridSpec(
            num_scalar_prefetch=2,
            grid=(n_rows // rb,),
            in_specs=[pl.BlockSpec((rb, d), lambda i, be, nb: (i, 0)),
                      wspec((d, f)), wspec((d, f)), wspec((f, d))],
            out_specs=pl.BlockSpec((rb, d), lambda i, be, nb: (i, 0))),
        out_shape=jax.ShapeDtypeStruct((n_rows, d), F32),
        compiler_params=_params(("arbitrary",)),
        name="moe_experts",
    )(block_e, n_used, rows, wg, wu, wd)


def _combine_kernel(tab_ref, next_tab_ref, route_ref, y_hbm, x_ref, mod_ref, lnf_ref, out_ref,
                    sorted_ref, sem, *, final_norm):
    step = pl.program_id(0) * pl.num_programs(1) + pl.program_id(1)
    slot = step % 2
    ne = tab_ref.shape[2] // 3
    make = lambda s: (lambda loc, glob: _row_copy(y_hbm.at[glob], sorted_ref.at[s, loc],
                                                  sem.at[s]))

    @pl.when(step == 0)
    def _():
        sorted_ref[...] = jnp.zeros_like(sorted_ref)
        _group_copies(tab_ref, ne, make(slot), True)

    @pl.when(step + 1 < pl.num_programs(0) * pl.num_programs(1))
    def _():
        _group_copies(next_tab_ref, ne, make(1 - slot), True)

    _group_copies(tab_ref, ne, make(slot), False)
    pos = route_ref[0, 4:6, :].astype(jnp.int32)
    wts = route_ref[0, 2:4, :]
    j = lax.broadcasted_iota(jnp.int32, (sorted_ref.shape[1], x_ref.shape[1]), 0)
    q_t = (jnp.where(j == pos[0:1, :], wts[0:1, :], 0.0)
           + jnp.where(j == pos[1:2, :], wts[1:2, :], 0.0))
    q_hi, q_lo = _split_bf16(q_t, 2)
    z = sorted_ref[slot].astype(BF16)
    f = _dot_tn(q_hi, z) + _dot_tn(q_lo, z)
    out = x_ref[0] + mod_ref[0, 5:6, :] * f
    if final_norm:
        out = _norm_mod(out, lnf_ref[...], 0.0, 0.0)
    out_ref[0] = out


def _combine(y_rows, route, table, x, mod, ln_final, final_norm):
    b, t, d = x.shape
    tm = route.shape[2]
    nt = t // tm
    return pl.pallas_call(
        functools.partial(_combine_kernel, final_norm=final_norm),
        grid=(b, nt),
        in_specs=[pl.BlockSpec((1, 1, table.shape[2]), lambda i, j: (i * nt + j, 0, 0),
                               memory_space=pltpu.SMEM),
                  pl.BlockSpec((1, 1, table.shape[2]),
                               lambda i, j: (jnp.minimum(i * nt + j + 1, b * nt - 1), 0, 0),
                               memory_space=pltpu.SMEM),
                  pl.BlockSpec((1, 8, tm), lambda i, j: (i * nt + j, 0, 0)),
                  pl.BlockSpec(memory_space=pl.ANY),
                  pl.BlockSpec((1, tm, d), lambda i, j: (i, j, 0)),
                  pl.BlockSpec((1, N_MOD, d), lambda i, j: (i, 0, 0)),
                  _const_spec(ln_final.shape)],
        out_specs=pl.BlockSpec((1, tm, d), lambda i, j: (i, j, 0)),
        out_shape=jax.ShapeDtypeStruct((b, t, d), F32),
        scratch_shapes=[pltpu.VMEM((2, _sorted_rows(tm), d), F32), pltpu.SemaphoreType.DMA((2,))],
        compiler_params=_params(("arbitrary", "arbitrary")),
        name="moe_combine",
    )(table, table, route, y_rows, x, mod, ln_final)


def _ffn_moe(x, mod, ln, ln_final, router, wg, wu, wd, final_norm):
    b, t, d = x.shape
    n = b * t
    ne = router.shape[1]
    rb = MOE_ROW_BLOCK
    h, route, cnt = _route(x, mod, ln, router)
    nt, tm = route.shape[1], route.shape[3]
    tiles = b * nt
    cnt = cnt[..., 0].reshape(tiles, 2 * ne).astype(jnp.int32)
    size, local = cnt[:, :ne], cnt[:, ne:]
    totals = jnp.sum(size, axis=0)
    padded = ((totals + rb - 1) // rb) * rb
    padded_end = jnp.cumsum(padded)
    base = (padded_end - padded)[None, :] + jnp.cumsum(size, axis=0) - size
    table = jnp.concatenate([local, size // GROUP_ALIGN, base], axis=1).reshape(tiles, 1, 3 * ne)
    route = route.reshape(tiles, 8, tm)
    n_blocks = -(-(n * 2 + tiles * ne * (GROUP_ALIGN - 1)) // rb) + ne
    block_e = jnp.minimum(
        jnp.searchsorted(padded_end, jnp.arange(n_blocks, dtype=jnp.int32) * rb, side="right"),
        ne - 1).astype(jnp.int32)
    n_used = (padded_end[-1:] // rb).astype(jnp.int32)
    n_rows = n_blocks * rb
    gap_start = jnp.concatenate([padded_end - padded + totals, padded_end[-1:]])
    gap_end = jnp.concatenate([padded_end, jnp.full((1,), n_rows, jnp.int32)])
    tails = jnp.concatenate([jnp.zeros_like(gap_start), (gap_end - gap_start) // GROUP_ALIGN,
                             gap_start]).astype(jnp.int32).reshape(1, 1, 3 * (ne + 1))
    rows = _dispatch(h.reshape(n, d), route, table, tails, n_rows)
    y_rows = _experts(rows, block_e, n_used, wg, wu, wd)
    return _combine(y_rows, route, table, x, mod, ln_final, final_norm)


def _block_diag(m, n):
    return jnp.kron(jnp.eye(n, dtype=m.dtype), m)


def _layer_params(i, w_in, conv_w, decay_w0, decay_w2, iclr_a0, iclr_a2, gate_g2, k_k, k_a, r_k,
                  lnx_w, lnx_b, w_fnet_out, w_rwkv_out, w_o):
    w = RWKV_WIDTH
    lora = decay_w2.shape[2]
    tail0 = FNET_WIDTH + 3 * w

    def reorder_tail(m):
        pad = jnp.zeros(m.shape[:-1] + (LANES - lora,), m.dtype)
        return jnp.concatenate([m[..., :3 * lora], pad, m[..., 3 * lora:]], axis=-1)

    row = lambda v: v.reshape(1, -1)
    hi_lo = lambda m: jnp.stack(_split_bf16(m, 2))
    wi = w_in[i]
    rwkv_cols = conv_w.shape[2]
    zeros = jnp.zeros((lora, w), F32)
    return {
        "w_a": wi[:, :FNET_WIDTH].astype(BF16),
        "w_main": wi[:, FNET_WIDTH:tail0].astype(BF16),
        "w_tail": reorder_tail(wi[:, tail0:FNET_WIDTH + rwkv_cols]).astype(BF16),
        "w_gates": wi[:, FNET_WIDTH + rwkv_cols:].astype(BF16),
        "conv_main": conv_w[i][:, :3 * w],
        "conv_tail": reorder_tail(conv_w[i][:, 3 * w:]),
        "decay_w0": decay_w0[i].reshape(1, 2 * w),
        "decay_w2": hi_lo(jnp.concatenate(
            [jnp.concatenate([decay_w2[i, 0], zeros], axis=1),
             jnp.concatenate([zeros, decay_w2[i, 1]], axis=1)], axis=0)),
        "iclr_a0": row(iclr_a0[i]),
        "iclr_a2": hi_lo(jnp.concatenate([iclr_a2[i], jnp.zeros((LANES - lora, w), F32)], axis=0)),
        "gate_g2": hi_lo(gate_g2[i]),
        "k_k": row(k_k[i]), "k_a": row(k_a[i]), "r_k": row(r_k[i]),
        "lnx_w": row(lnx_w[i]), "lnx_b": row(lnx_b[i]),
        "head_sum": _block_diag(jnp.ones((HEAD_DIM, HEAD_DIM), BF16), LANES // HEAD_DIM),
        "head_mean": _block_diag(jnp.full((HEAD_DIM, HEAD_DIM), 1.0 / HEAD_DIM, BF16),
                                 LANES // HEAD_DIM),
        "w_fnet_out": w_fnet_out[i].astype(BF16),
        "w_rwkv_out": w_rwkv_out[i].astype(BF16),
        "w_o": w_o[i].astype(BF16),
    }


def _channel_dft():
    k = jnp.arange(HEAD_DIM, dtype=jnp.int32)
    ang = ((k[:, None] * k[None, :]) % HEAD_DIM).astype(F32) * (2.0 * math.pi / HEAD_DIM)
    groups = FNET_WIDTH // HEAD_DIM
    return jnp.concatenate([_block_diag(jnp.cos(ang), groups), _block_diag(jnp.sin(ang), groups)],
                           axis=1).astype(BF16)


def _token_mixer(x, mod, ln, p, cs, dft_tab, ffn=None):
    zc, zs, um, ut, gates = _inproj(x, mod, ln, p["w_a"], p["w_main"], p["w_tail"], p["w_gates"], cs)
    yf = _fnet(zc, zs, dft_tab)
    r, k, v, a, b, e_f, e_b, g, bonus = _prep(um, ut, p)
    o = _wkv(r, k, v, a, b, e_b, True, add=_wkv(r, k, v, a, b, e_f, False))
    return _mixout(o, bonus, g, yf, gates, x, mod, p, ffn)


def kernel(x_prompt, x_sample, c_prompt, c_sample, w_ada, b_ada, ln_mix, w_in, conv_w, decay_w0, decay_w2, iclr_a0, iclr_a2, gate_g2, k_k, k_a, r_k, lnx_w, lnx_b, w_fnet_out, w_rwkv_out, w_o, ln_ffn, ff_w_gate, ff_w_up, ff_w_down, router_w, moe_w_gate, moe_w_up, moe_w_down, ln_final):
    depth, d = ln_mix.shape
    streams = [x_prompt, x_sample]
    nb = [x.shape[0] for x in streams]
    mod_all = _ada(jnp.concatenate([c_prompt, c_sample], axis=0), w_ada, b_ada)
    cs = _channel_dft()
    dfts = {}
    for x in streams:
        if x.shape[1] not in dfts:
            dfts[x.shape[1]] = _dft_tables(x.shape[1])
    lnf = ln_final.reshape(1, d)
    for i in range(depth):
        p = _layer_params(i, w_in, conv_w, decay_w0, decay_w2, iclr_a0, iclr_a2, gate_g2, k_k, k_a,
                          r_k, lnx_w, lnx_b, w_fnet_out, w_rwkv_out, w_o)
        j = i // 2
        final = i == depth - 1
        if i % 2 == 0:
            ffw = (ff_w_gate[j].astype(BF16), ff_w_up[j].astype(BF16), ff_w_down[j].astype(BF16))
        else:
            ffw = (moe_w_gate[j].astype(BF16), moe_w_up[j].astype(BF16), moe_w_down[j].astype(BF16))
            router = router_w[j]
        off = 0
        for s, x in enumerate(streams):
            mod = mod_all[i, off:off + nb[s]].reshape(nb[s], N_MOD, d)
            off += nb[s]
            ln1, ln2 = ln_mix[i].reshape(1, d), ln_ffn[i].reshape(1, d)
            if i % 2 == 0:
                x = _token_mixer(x, mod, ln1, p, cs, dfts[x.shape[1]], (ln2, lnf, *ffw, final))
            else:
                x = _token_mixer(x, mod, ln1, p, cs, dfts[x.shape[1]])
                x = _ffn_moe(x, mod, ln2, lnf, router, *ffw, final)
            streams[s] = x
    return tuple(streams)
```

```python
import functools
import math

import jax
import jax.numpy as jnp
from jax import lax
from jax.experimental import pallas as pl
from jax.experimental.pallas import tpu as pltpu

F32 = jnp.float32
BF16 = jnp.bfloat16
HIGHEST = lax.Precision.HIGHEST

HEAD_DIM = 64
LANES = 128
FNET_WIDTH = 256
RWKV_WIDTH = 768
LORA_TAIL = 384
N_MOD = 6
N_EXPERTS = 8
RMS_EPS = 1e-6
GN_EPS = 64e-5
WKV_CHUNK = 128
MOE_ROW_BLOCK = 512
GROUP_ALIGN = 8
GROUP_CHUNK = 64
VMEM_LIMIT = 56 * 1024 * 1024


def _params(sem, vmem=VMEM_LIMIT):
    return pltpu.CompilerParams(dimension_semantics=sem, vmem_limit_bytes=vmem)


def _const_spec(shape):
    nd = len(shape)
    return pl.BlockSpec(shape, lambda *_: (0,) * nd, pipeline_mode=pl.Buffered(1))


def _dot(a, b):
    return jnp.dot(a.astype(BF16), b.astype(BF16), preferred_element_type=F32)


def _dot_nt(a, b):
    return lax.dot_general(a.astype(BF16), b.astype(BF16), (((1,), (1,)), ((), ())),
                           preferred_element_type=F32)


def _dot_tn(a, b):
    return lax.dot_general(a.astype(BF16), b.astype(BF16), (((0,), (0,)), ((), ())),
                           preferred_element_type=F32)


def _split_bf16(x, parts):
    out, rem = [], x
    for _ in range(parts):
        p = rem.astype(BF16)
        out.append(p)
        rem = rem - p.astype(F32)
    return out


def _dot_exact_rhs(x, m, parts):
    acc = None
    for p in _split_bf16(x, parts):
        t = jnp.dot(p, m, preferred_element_type=F32)
        acc = t if acc is None else acc + t
    return acc


def _dot_f32(a, b):
    return jnp.dot(a, b, precision=HIGHEST, preferred_element_type=F32)


def _dot_3pass(a, w_ref):
    ah, al = _split_bf16(a, 2)
    d = lambda x, y: jnp.dot(x, y, preferred_element_type=F32)
    return d(ah, w_ref[0]) + (d(al, w_ref[0]) + d(ah, w_ref[1]))


def _norm_mod(x, gain, shift, scale):
    ms = jnp.mean(x * x, axis=-1, keepdims=True)
    return x * lax.rsqrt(ms + RMS_EPS) * gain * (1.0 + scale) + shift


def _silu(x):
    return x * jax.nn.sigmoid(x)


def _head_reduce(x, m, parts=2):
    cols = [_dot_exact_rhs(x[:, j:j + LANES], m, parts) for j in range(0, x.shape[1], LANES)]
    return jnp.concatenate(cols, axis=1)


def _row_tile(t, want):
    tm = min(t, want)
    assert t % tm == 0
    return tm


def _sub_blocks(rows, size=512):
    size = min(size, rows)
    assert rows % size == 0
    return [slice(r, r + size) for r in range(0, rows, size)]


def _col_chunks(width, size):
    return [slice(c, min(c + size, width)) for c in range(0, width, size)]


def _ada_kernel(c_ref, w_ref, b_ref, o_ref):
    o_ref[0] = _dot_f32(_silu(c_ref[...]), w_ref[0]) + b_ref[0]


def _ada(c_all, w_ada, b_ada):
    depth, d, n = w_ada.shape
    bc = c_all.shape[0]
    tn = n // 4
    return pl.pallas_call(
        _ada_kernel,
        grid=(depth, n // tn),
        in_specs=[pl.BlockSpec((bc, d), lambda l, j: (0, 0)),
                  pl.BlockSpec((1, d, tn), lambda l, j: (l, 0, j)),
                  pl.BlockSpec((1, 1, tn), lambda l, j: (l, 0, j))],
        out_specs=pl.BlockSpec((1, bc, tn), lambda l, j: (l, 0, j)),
        out_shape=jax.ShapeDtypeStruct((depth, bc, n), F32),
        compiler_params=_params(("parallel", "parallel")),
        name="ada",
    )(c_all, w_ada, b_ada.reshape(depth, 1, n))


def _inproj_kernel(x_ref, mod_ref, ln_ref, wa_ref, wm_ref, wt_ref, wg_ref, cs_ref,
                   zc_ref, zs_ref, um_ref, ut_ref, g_ref):
    for rows in _sub_blocks(x_ref.shape[1]):
        h = _norm_mod(x_ref[0, rows, :], ln_ref[...], mod_ref[0, 0:1, :],
                      mod_ref[0, 1:2, :]).astype(BF16)
        ua = jnp.dot(h, wa_ref[...], preferred_element_type=F32)
        zz = jnp.dot(ua.astype(BF16), cs_ref[...], preferred_element_type=F32)
        zc_ref[0, rows, :] = zz[:, :FNET_WIDTH]
        zs_ref[0, rows, :] = zz[:, FNET_WIDTH:]
        for cols in _col_chunks(wm_ref.shape[1], 1280):
            um_ref[0, rows, cols] = jnp.dot(
                h, wm_ref[:, cols], preferred_element_type=F32).astype(BF16)
        ut_ref[0, rows, :] = jnp.dot(h, wt_ref[...], preferred_element_type=F32)
        for cols in _col_chunks(wg_ref.shape[1], 1024):
            g_ref[0, rows, cols] = jax.nn.sigmoid(
                jnp.dot(h, wg_ref[:, cols], preferred_element_type=F32)).astype(BF16)


def _inproj(x, mod, ln, wa, wm, wt, wg, cs):
    b, t, d = x.shape
    tm = _row_tile(t, 1024)
    row = lambda w: pl.BlockSpec((1, tm, w), lambda i, j: (i, j, 0))
    outs = [(FNET_WIDTH, F32), (FNET_WIDTH, F32), (wm.shape[1], BF16), (wt.shape[1], F32),
            (wg.shape[1], BF16)]
    return pl.pallas_call(
        _inproj_kernel,
        grid=(b, t // tm),
        in_specs=[row(d), pl.BlockSpec((1, N_MOD, d), lambda i, j: (i, 0, 0)),
                  _const_spec(ln.shape), _const_spec(wa.shape), _const_spec(wm.shape),
                  _const_spec(wt.shape), _const_spec(wg.shape), _const_spec(cs.shape)],
        out_specs=[row(w) for w, _ in outs],
        out_shape=[jax.ShapeDtypeStruct((b, t, w), dt) for w, dt in outs],
        compiler_params=_params(("parallel", "parallel")),
        name="inproj",
    )(x, mod, ln, wa, wm, wt, wg, cs)


FNET_T2 = 64
FNET_ROWS = 32


def _rows_at(ref, j):
    return ref[0, :, j:j + 1, :].reshape(ref.shape[1], ref.shape[3])


def _fnet_stage1_kernel(d_ref, zc_ref, zs_ref, twc_ref, tws_ref, gr_ref, gi_ref):
    t2, w = zc_ref.shape[1], zc_ref.shape[3]
    for j in range(zc_ref.shape[2]):
        data = jnp.concatenate([_rows_at(zc_ref, j), _rows_at(zs_ref, j)], axis=0)
        g = _dot(d_ref[...], data)
        g_r, g_i = g[:t2], g[t2:]
        c = jnp.concatenate([twc_ref[j]] * (w // LANES), axis=1)
        s = jnp.concatenate([tws_ref[j]] * (w // LANES), axis=1)
        gr_ref[0, j] = g_r * c + g_i * s
        gi_ref[0, j] = g_i * c - g_r * s


def _fnet_stage2_kernel(d_ref, gr_ref, gi_ref, o_ref, *, scale):
    t1, w = gr_ref.shape[1], gr_ref.shape[3]
    for j in range(gr_ref.shape[2]):
        data = jnp.concatenate([_rows_at(gr_ref, j), _rows_at(gi_ref, j)], axis=0)
        o_ref[0, :, j:j + 1, :] = (_dot(d_ref[...], data) * scale).reshape(t1, 1, w)


def _dft_tables(t):
    t2 = FNET_T2
    t1 = t // t2
    assert t1 * t2 == t

    def cos_sin(rows, cols, n):
        k = (lax.iota(jnp.int32, rows)[:, None] * lax.iota(jnp.int32, cols)[None, :]) % n
        ang = k.astype(F32) * (2.0 * math.pi / n)
        return jnp.cos(ang), jnp.sin(ang)

    c2, s2 = cos_sin(t2, t2, t2)
    c1, s1 = cos_sin(t1, t1, t1)
    twc, tws = cos_sin(t1, t2, t)
    lanes = lambda m: jnp.broadcast_to(m[:, :, None], (t1, t2, LANES))
    return {
        "stage1": jnp.block([[c2, -s2], [-s2, -c2]]).astype(BF16),
        "stage2": jnp.concatenate([c1, s1], axis=1).astype(BF16),
        "twc": lanes(twc), "tws": lanes(tws),
    }


def _fnet(zc, zs, tab):
    b, t, w = zc.shape
    t2 = FNET_T2
    t1 = t // t2
    n1 = min(FNET_ROWS, t1)
    in_spec = pl.BlockSpec((1, t2, n1, w), lambda i, j: (i, 0, j, 0))
    tw_spec = pl.BlockSpec((n1, t2, LANES), lambda i, j: (j, 0, 0))
    g_shape = jax.ShapeDtypeStruct((b, t1, t2, w), F32)
    g_r, g_i = pl.pallas_call(
        _fnet_stage1_kernel,
        grid=(b, t1 // n1),
        in_specs=[_const_spec(tab["stage1"].shape), in_spec, in_spec, tw_spec, tw_spec],
        out_specs=[pl.BlockSpec((1, n1, t2, w), lambda i, j: (i, j, 0, 0))] * 2,
        out_shape=[g_shape, g_shape],
        compiler_params=_params(("parallel", "parallel")),
        name="fnet_stage1",
    )(tab["stage1"], zc.reshape(b, t2, t1, w), zs.reshape(b, t2, t1, w), tab["twc"], tab["tws"])
    n2 = min(FNET_ROWS, t2)
    spec = pl.BlockSpec((1, t1, n2, w), lambda i, j: (i, 0, j, 0))
    y = pl.pallas_call(
        functools.partial(_fnet_stage2_kernel, scale=1.0 / math.sqrt(t * HEAD_DIM)),
        grid=(b, t2 // n2),
        in_specs=[_const_spec(tab["stage2"].shape), spec, spec],
        out_specs=spec,
        out_shape=jax.ShapeDtypeStruct((b, t1, t2, w), F32),
        compiler_params=_params(("parallel", "parallel")),
        name="fnet_stage2",
    )(tab["stage2"], g_r, g_i)
    return y.reshape(b, t, w)


def _conv3(main, prev_row, next_row, w):
    tm = main.shape[0]
    ri = lax.broadcasted_iota(jnp.int32, main.shape, 0)
    up = jnp.where(ri == 0, prev_row, pltpu.roll(main, 1, 0))
    dn = jnp.where(ri == tm - 1, next_row, pltpu.roll(main, tm - 1, 0))
    return w[0:1] * up + w[1:2] * main + w[2:3] * dn


def _prep_kernel(um_ref, ump_ref, umn_ref, ut_ref, utp_ref, utn_ref, cwm_ref, cwt_ref,
                 w0_ref, w2_ref, a0_ref, a2_ref, g2_ref, kk_ref, ka_ref, rk_ref, hs_ref, shift_ref,
                 r_ref, k_ref, v_ref, a_ref, b_ref, ef_ref, eb_ref, g_ref, bonus_ref):
    j = pl.program_id(1)
    first = j == 0
    last = j == pl.num_programs(1) - 1
    hp = ump_ref.shape[1]

    def conv_main(c0, c1):
        main = um_ref[0, :, c0:c1]
        tm = main.shape[0]
        shifted = jnp.dot(shift_ref[...], main, preferred_element_type=F32)
        sub = 8
        ri = lax.broadcasted_iota(jnp.int32, (sub, main.shape[1]), 0)
        prev = jnp.where(first, 0.0, ump_ref[0, hp - 1:hp, c0:c1].astype(F32))
        nxt = jnp.where(last, 0.0, umn_ref[0, 0:1, c0:c1].astype(F32))
        up = jnp.concatenate([jnp.where(ri == 0, prev, shifted[:sub]), shifted[sub:tm]], axis=0)
        dn = jnp.concatenate([shifted[tm:2 * tm - sub],
                              jnp.where(ri == sub - 1, nxt, shifted[2 * tm - sub:])], axis=0)
        w = cwm_ref[:, c0:c1]
        return w[0:1] * up + w[1:2] * main.astype(F32) + w[2:3] * dn

    w = RWKV_WIDTH
    r = conv_main(0, w)
    k = conv_main(w, 2 * w)
    v = conv_main(2 * w, 3 * w)
    tp = utp_ref.shape[1]
    tail = _conv3(ut_ref[0], jnp.where(first, 0.0, utp_ref[0, tp - 1:tp, :]),
                  jnp.where(last, 0.0, utn_ref[0, 0:1, :]), cwt_ref[...])
    xw, xa, xg = tail[:, 0:LANES], tail[:, LANES:2 * LANES], tail[:, 2 * LANES:3 * LANES]

    e = jax.nn.sigmoid(w0_ref[...] + _dot_3pass(jnp.tanh(xw), w2_ref)) * math.exp(-0.5)
    ef_ref[0] = e[:, :w]
    eb_ref[0] = e[:, w:]
    a = jax.nn.sigmoid(a0_ref[...] + _dot_3pass(xa, a2_ref))
    g_ref[0] = _dot_3pass(jax.nn.sigmoid(xg), g2_ref).astype(BF16)

    kk = k * kk_ref[...]
    kk = kk * lax.rsqrt(_head_reduce(kk * kk, hs_ref[...], parts=1) + 1e-12)
    k = k * (1.0 + (a - 1.0) * ka_ref[...])
    r_ref[0] = r.astype(BF16)
    k_ref[0] = k.astype(BF16)
    v_ref[0] = v.astype(BF16)
    a_ref[0] = (-kk).astype(BF16)
    b_ref[0] = (kk * a).astype(BF16)
    bonus_ref[0] = (_head_reduce(r * k * rk_ref[...], hs_ref[...]) * v).astype(BF16)


def _prep(um, ut, p):
    b, t, wm = um.shape
    wt = ut.shape[2]
    tm = _row_tile(t, 256)
    hm, ht = 16, 8
    nm, nt = tm // hm, tm // ht
    main = lambda w: pl.BlockSpec((1, tm, w), lambda i, j: (i, j, 0))
    prev = lambda h, n, w: pl.BlockSpec((1, h, w), lambda i, j: (i, jnp.maximum(j * n - 1, 0), 0))
    nxt = lambda h, n, w: pl.BlockSpec(
        (1, h, w), lambda i, j: (i, jnp.minimum((j + 1) * n, t // h - 1), 0))
    consts = [p["conv_main"], p["conv_tail"], p["decay_w0"], p["decay_w2"], p["iclr_a0"],
              p["iclr_a2"], p["gate_g2"], p["k_k"], p["k_a"], p["r_k"], p["head_sum"],
              jnp.concatenate([jnp.eye(tm, k=-1, dtype=BF16), jnp.eye(tm, k=1, dtype=BF16)], axis=0)]
    outs = [BF16] * 5 + [F32, F32, BF16, BF16]
    return pl.pallas_call(
        _prep_kernel,
        grid=(b, t // tm),
        in_specs=[main(wm), prev(hm, nm, wm), nxt(hm, nm, wm),
                  main(wt), prev(ht, nt, wt), nxt(ht, nt, wt)]
                 + [_const_spec(c.shape) for c in consts],
        out_specs=[main(RWKV_WIDTH) for _ in outs],
        out_shape=[jax.ShapeDtypeStruct((b, t, RWKV_WIDTH), dt) for dt in outs],
        compiler_params=_params(("parallel", "parallel")),
        name="rwkv_prep",
    )(um, um, um, ut, ut, ut, *consts)


def _wkv_kernel(r_ref, k_ref, v_ref, a_ref, b_ref, e_ref, *rest, reverse):
    *add_refs, o_ref, s_ref = rest
    add_ref = add_refs[0] if add_refs else None
    c = r_ref.shape[1]
    pairs = range(s_ref.shape[0])
    heads = range(LANES // HEAD_DIM)
    chains = [(p, h) for p in pairs for h in heads]

    @pl.when(pl.program_id(1) == 0)
    def _():
        s_ref[...] = jnp.zeros_like(s_ref)

    row = lax.broadcasted_iota(jnp.int32, (c, c), 0)
    col = lax.broadcasted_iota(jnp.int32, (c, c), 1)
    if reverse:
        incl, strict = col >= row, col > row
    else:
        incl, strict = col <= row, col < row
    tri = jnp.where(incl, 1.0, 0.0).astype(BF16)
    end = 0 if reverse else c - 1
    lane = lax.broadcasted_iota(jnp.int32, (1, LANES), 1)
    in_head = [(lane >= h * HEAD_DIM) & (lane < (h + 1) * HEAD_DIM) for h in heads]
    same_head = (lax.broadcasted_iota(jnp.int32, (LANES, LANES), 0) // HEAD_DIM
                 == lax.broadcasted_iota(jnp.int32, (LANES, LANES), 1) // HEAD_DIM)

    per_row = r_ref.shape[2] // LANES

    def tile(ref, p):
        return ref[p // per_row, :, (p % per_row) * LANES:(p % per_row + 1) * LANES]

    tri2 = jnp.concatenate([tri, tri], axis=1)

    def running_sum(e):
        return jnp.dot(tri2, jnp.concatenate(_split_bf16(e, 2), axis=0), preferred_element_type=F32)

    cum = [running_sum(tile(e_ref, p)) for p in pairs]
    total = [x[end:end + 1, :] for x in cum]
    ref = [x[c // 2:c // 2 + 1, :] for x in cum]
    dl = [cum[p] - ref[p] for p in pairs]
    g_inv = [jnp.exp(dl[p]) for p in pairs]
    at = [tile(a_ref, p).astype(F32) * jnp.exp(tile(e_ref, p) - dl[p]) for p in pairs]
    rt = [tile(r_ref, p).astype(F32) * jnp.exp(-dl[p]) for p in pairs]
    bt = [tile(b_ref, p).astype(F32) * g_inv[p] for p in pairs]
    kt = [tile(k_ref, p).astype(F32) * g_inv[p] for p in pairs]
    g_end = [jnp.exp(ref[p] - total[p]) for p in pairs]
    bh = [(bt[p] * g_end[p]).astype(BF16) for p in pairs]
    kh = [(kt[p] * g_end[p]).astype(BF16) for p in pairs]
    bk = [jnp.concatenate([bt[p], kt[p]], axis=0).astype(BF16) for p in pairs]
    ar = [jnp.concatenate([at[p], rt[p]], axis=0) for p in pairs]
    vb = [tile(v_ref, p) for p in pairs]

    prod = [_dot_nt(jnp.where(in_head[h], ar[p], 0.0), bk[p]) for p, h in chains]
    l_ab = [jnp.where(strict, x[:c, :c], 0.0) for x in prod]
    l_ak = [jnp.where(strict, x[:c, c:], 0.0) for x in prod]
    m_rb = [jnp.where(incl, x[c:, :c], 0.0).astype(BF16) for x in prod]
    m_rk = [jnp.where(incl, x[c:, c:], 0.0).astype(BF16) for x in prod]
    del prod
    eye = jnp.where(row == col, 1.0, 0.0)
    t = [eye + jnp.where((row >> 1) == (col >> 1), x, 0.0) for x in l_ab]
    nch = range(len(chains))
    for sh in range(1, (c - 1).bit_length()):
        m = 1 << sh
        joins = ((row >> (sh + 1)) == (col >> (sh + 1))) & ((row >> sh) != (col >> sh))
        e_k = [jnp.where(joins, l_ab[i], 0.0) for i in nch]
        if m < 8:
            et = [_dot(e_k[i], t[i]) for i in nch]
            t = [t[i] + _dot(t[i], et[i]) for i in nch]
            continue
        live = [slice(b0 + (0 if reverse else m), b0 + (m if reverse else 2 * m))
                for b0 in range(0, c, 2 * m)]

        def take(v):
            return jnp.concatenate([v[rs] for rs in live], axis=0)

        def spread(v):
            zero = jnp.zeros((m, v.shape[1]), v.dtype)
            parts = []
            for j in range(len(live)):
                blk = v[j * m:(j + 1) * m]
                parts += [blk, zero] if reverse else [zero, blk]
            return jnp.concatenate(parts, axis=0)

        et = [_dot(take(e_k[i]), t[i]) for i in nch]
        upd = [_dot(take(t[i]), spread(et[i])) for i in nch]
        t = [t[i] + spread(upd[i]) for i in nch]
    lakv = [jnp.dot(l_ak[i].astype(BF16), vb[p], preferred_element_type=F32)
            for i, (p, h) in enumerate(chains)]
    x = [_dot(t[i], jnp.concatenate([at[p], lakv[i]], axis=1)) for i, (p, h) in enumerate(chains)]
    x_a = [v[:, :LANES] for v in x]
    x_u = [v[:, LANES:] for v in x]
    y_a = [jnp.dot(m_rb[i], x_a[i].astype(BF16), preferred_element_type=F32) for i in nch]
    y_o = [jnp.dot(jnp.concatenate([m_rb[i], m_rk[i]], axis=1),
                   jnp.concatenate([x_u[i].astype(BF16), vb[p]], axis=0),
                   preferred_element_type=F32) for i, (p, h) in enumerate(chains)]

    def pick(vals, p):
        out = vals[p * len(heads)]
        for h in heads[1:]:
            out = jnp.where(in_head[h], vals[p * len(heads) + h], out)
        return out

    a_solved = [pick(x_a, p) for p in pairs]
    u = [pick(x_u, p) for p in pairs]
    r_hat = [rt[p] + pick(y_a, p) for p in pairs]
    o_intra = [pick(y_o, p) for p in pairs]
    s = [s_ref[p] for p in pairs]
    s_hat = [(s[p] * jnp.exp(-ref[p])).astype(BF16) for p in pairs]
    out = [_dot_nt(r_hat[p], s_hat[p]) + o_intra[p] for p in pairs]
    g_t = [jnp.where(same_head, _dot_tn(a_solved[p], bh[p]), 0.0) for p in pairs]
    h_t = [jnp.where(same_head,
                     _dot_tn(jnp.concatenate([u[p].astype(BF16), vb[p]], axis=0),
                             jnp.concatenate([bh[p], kh[p]], axis=0)), 0.0) for p in pairs]
    s_new = [s[p] * jnp.exp(-total[p]) + _dot(s_hat[p], g_t[p]) + h_t[p] for p in pairs]
    for p in pairs:
        if add_ref is not None:
            out[p] = out[p] + tile(add_ref, p).astype(F32)
        o_ref[p // per_row, :, (p % per_row) * LANES:(p % per_row + 1) * LANES] = (
            out[p].astype(o_ref.dtype))
        s_ref[p] = s_new[p]


def _wkv(r, k, v, a, b, e, reverse, add=None):
    bsz, t, w = r.shape
    c = min(WKV_CHUNK, t)
    nc = t // c
    rows = math.gcd(bsz, 2)
    if reverse:
        idx = lambda i, j: (i, nc - 1 - j, 0)
    else:
        idx = lambda i, j: (i, j, 0)
    spec = pl.BlockSpec((rows, c, w), idx)
    return pl.pallas_call(
        functools.partial(_wkv_kernel, reverse=reverse),
        grid=(bsz // rows, nc),
        in_specs=[spec] * (6 if add is None else 7),
        out_specs=spec,
        out_shape=jax.ShapeDtypeStruct((bsz, t, w), BF16),
        scratch_shapes=[pltpu.VMEM((rows * (w // LANES), LANES, LANES), F32)],
        compiler_params=_params(("parallel", "arbitrary")),
        name="wkv_bwd" if reverse else "wkv_fwd",
    )(r, k, v, a, b, e, *(() if add is None else (add,)))


def _ffn_rows(x, mod_ref, ln_ref, lnf_ref, wg_ref, wu_ref, wd_ref, final_norm):
    h = _norm_mod(x, ln_ref[...], mod_ref[0, 3:4, :], mod_ref[0, 4:5, :]).astype(BF16)
    acc = None
    for cols in _col_chunks(wg_ref.shape[1], 768):
        gate = jnp.dot(h, wg_ref[:, cols], preferred_element_type=F32)
        up = jnp.dot(h, wu_ref[:, cols], preferred_element_type=F32)
        part = _dot(_silu(gate) * up, wd_ref[cols, :])
        acc = part if acc is None else acc + part
    out = x + mod_ref[0, 5:6, :] * acc
    if final_norm:
        out = _norm_mod(out, lnf_ref[...], 0.0, 0.0)
    return out


def _mixout_kernel(o_ref, bonus_ref, g_ref, yf_ref, gates_ref, x_ref, mod_ref,
                   lnw_ref, lnb_ref, hm_ref, wfo_ref, wro_ref, wo_ref, *rest, ffn_final):
    *ffn_refs, out_ref = rest
    d = x_ref.shape[2]
    for rows in _sub_blocks(x_ref.shape[1]):
        o = o_ref[0, rows, :].astype(F32)
        dev = o - _head_reduce(o, hm_ref[...])
        var = _head_reduce(dev * dev, hm_ref[...])
        o = dev * lax.rsqrt(var + GN_EPS) * lnw_ref[...] + lnb_ref[...]
        y = (o + bonus_ref[0, rows, :].astype(F32)) * g_ref[0, rows, :].astype(F32)
        y_b = _dot(y, wro_ref[...])
        y_a = _dot(yf_ref[0, rows, :], wfo_ref[...])
        merged = (gates_ref[0, rows, :d].astype(F32) * y_a
                  + gates_ref[0, rows, d:].astype(F32) * y_b)
        x = x_ref[0, rows, :] + mod_ref[0, 2:3, :] * _dot(merged, wo_ref[...])
        if ffn_refs:
            x = _ffn_rows(x, mod_ref, *ffn_refs, ffn_final)
        out_ref[0, rows, :] = x


def _mixout(o, bonus, g, yf, gates, x, mod, p, ffn=None):
    b, t, d = x.shape
    tm = _row_tile(t, 1024 if ffn is None else 512)
    row = lambda a: pl.BlockSpec((1, tm, a.shape[2]), lambda i, j: (i, j, 0))
    consts = [p["lnx_w"], p["lnx_b"], p["head_mean"], p["w_fnet_out"], p["w_rwkv_out"], p["w_o"]]
    consts += [] if ffn is None else list(ffn[:5])
    acts = [o, bonus, g, yf, gates, x]
    return pl.pallas_call(
        functools.partial(_mixout_kernel, ffn_final=ffn is not None and ffn[5]),
        grid=(b, t // tm),
        in_specs=[row(a) for a in acts] + [pl.BlockSpec((1, N_MOD, d), lambda i, j: (i, 0, 0))]
                 + [_const_spec(c.shape) for c in consts],
        out_specs=pl.BlockSpec((1, tm, d), lambda i, j: (i, j, 0)),
        out_shape=jax.ShapeDtypeStruct((b, t, d), F32),
        compiler_params=_params(("parallel", "parallel")),
        name="mixout" if ffn is None else "mixout_ffn",
    )(*acts, mod, *consts)


def _route_kernel(x_ref, mod_ref, ln_ref, rw_ref, h_ref, route_ref, cnt_ref, *, ne):
    h = _norm_mod(x_ref[0], ln_ref[...], mod_ref[0, 3:4, :], mod_ref[0, 4:5, :])
    h_ref[0] = h
    tm = h.shape[0]
    logits = lax.dot_general(rw_ref[...], h, (((1,), (1,)), ((), ())), precision=HIGHEST,
                             preferred_element_type=F32)
    ei = lax.broadcasted_iota(jnp.int32, logits.shape, 0)
    m1 = jnp.max(logits, axis=0, keepdims=True)
    i1 = jnp.min(jnp.where(logits == m1, ei, ne), axis=0, keepdims=True)
    rest = jnp.where(ei == i1, -jnp.inf, logits)
    m2 = jnp.max(rest, axis=0, keepdims=True)
    i2 = jnp.min(jnp.where(rest == m2, ei, ne), axis=0, keepdims=True)
    t2 = jnp.exp(m2 - m1)
    w1 = 1.0 / (1.0 + t2)
    w2 = t2 / (1.0 + t2)
    oh1 = jnp.where(ei == i1, 1.0, 0.0)
    oh2 = jnp.where(ei == i2, 1.0, 0.0)
    before = (lax.broadcasted_iota(jnp.int32, (tm, tm), 0)
              < lax.broadcasted_iota(jnp.int32, (tm, tm), 1))
    before = jnp.where(before, 1.0, 0.0).astype(BF16)
    cs1 = jnp.dot(oh1.astype(BF16), before, preferred_element_type=F32)
    cs2 = jnp.dot(oh2.astype(BF16), before, preferred_element_type=F32)
    n1 = jnp.sum(oh1, axis=1, keepdims=True)
    n2 = jnp.sum(oh2, axis=1, keepdims=True)
    size = jnp.broadcast_to(jnp.floor((n1 + n2 + (GROUP_ALIGN - 1)) * (1.0 / GROUP_ALIGN))
                            * GROUP_ALIGN, (ne, LANES))
    sub = lax.broadcasted_iota(jnp.int32, size.shape, 0)
    start = size
    for sh in (1, 2, 4):
        start = start + jnp.where(sub >= sh, pltpu.roll(start, sh, 0), 0.0)
    start = start - size
    pos1 = jnp.sum(oh1 * (cs1 + start[:, 0:1]), axis=0, keepdims=True)
    pos2 = jnp.sum(oh2 * (cs2 + n1 + start[:, 0:1]), axis=0, keepdims=True)
    zero = jnp.zeros_like(w1)
    route_ref[0, 0] = jnp.concatenate(
        [i1.astype(F32), i2.astype(F32), w1, w2, pos1, pos2, zero, zero], axis=0)
    cnt_ref[0, 0] = jnp.concatenate([size, start], axis=0)


def _route(x, mod, ln, router):
    b, t, d = x.shape
    tm = _row_tile(t, 512)
    nt = t // tm
    ne = router.shape[1]
    assert ne == 8
    router = router.T
    return pl.pallas_call(
        functools.partial(_route_kernel, ne=ne),
        grid=(b, nt),
        in_specs=[pl.BlockSpec((1, tm, d), lambda i, j: (i, j, 0)),
                  pl.BlockSpec((1, N_MOD, d), lambda i, j: (i, 0, 0)),
                  _const_spec(ln.shape), _const_spec(router.shape)],
        out_specs=[pl.BlockSpec((1, tm, d), lambda i, j: (i, j, 0)),
                   pl.BlockSpec((1, 1, 8, tm), lambda i, j: (i, j, 0, 0)),
                   pl.BlockSpec((1, 1, 2 * ne, LANES), lambda i, j: (i, j, 0, 0))],
        out_shape=[jax.ShapeDtypeStruct((b, t, d), F32),
                   jax.ShapeDtypeStruct((b, nt, 8, tm), F32),
                   jax.ShapeDtypeStruct((b, nt, 2 * ne, LANES), F32)],
        compiler_params=_params(("parallel", "parallel")),
        name="moe_route",
    )(x, mod, ln, router)


def _row_copy(src, dst, sem):
    return pltpu.make_async_copy(src, dst, sem)


def _sorted_rows(tm):
    return 2 * tm + LANES


def _group_copies(tab_ref, ne, make_copy, issue, local_step=1):
    per = GROUP_CHUNK // GROUP_ALIGN
    for e in range(ne):
        local, units, glob = tab_ref[0, 0, e], tab_ref[0, 0, ne + e], tab_ref[0, 0, 2 * ne + e]
        big = units // per

        def chunk(c, _, rows, first):
            off = pl.multiple_of(first + c * rows, GROUP_ALIGN)
            cp = make_copy(pl.ds(pl.multiple_of(local + off * local_step, GROUP_ALIGN), rows),
                           pl.ds(pl.multiple_of(glob + off, GROUP_ALIGN), rows))
            cp.start() if issue else cp.wait()
            return 0

        lax.fori_loop(0, big, functools.partial(chunk, rows=GROUP_CHUNK, first=0), 0)
        lax.fori_loop(0, units - big * per,
                      functools.partial(chunk, rows=GROUP_ALIGN, first=big * GROUP_CHUNK), 0)


def _dispatch_kernel(tab_ref, prev_tab_ref, tail_ref, route_ref, h_ref, rows_ref, sorted_ref,
                     zero_ref, sem):
    i = pl.program_id(0)
    slot = i % 2
    ne = tab_ref.shape[2] // 3
    make = lambda s: (lambda loc, glob: _row_copy(sorted_ref.at[s, loc], rows_ref.at[glob],
                                                  sem.at[s]))
    pos = route_ref[0, 4:6, :].astype(jnp.int32)
    j = lax.broadcasted_iota(jnp.int32, (sorted_ref.shape[1], h_ref.shape[0]), 0)
    pick = jnp.where((j == pos[0:1, :]) | (j == pos[1:2, :]), 1.0, 0.0).astype(BF16)
    sorted_ref[slot] = jnp.dot(pick, h_ref[...].astype(BF16), preferred_element_type=F32)
    _group_copies(tab_ref, ne, make(slot), True)

    @pl.when(i > 0)
    def _():
        _group_copies(prev_tab_ref, ne, make(1 - slot), False)

    @pl.when(i == pl.num_programs(0) - 1)
    def _():
        _group_copies(tab_ref, ne, make(slot), False)
        zero_ref[...] = jnp.zeros_like(zero_ref)
        fill = lambda loc, glob: _row_copy(zero_ref.at[loc], rows_ref.at[glob], sem.at[0])
        _group_copies(tail_ref, tail_ref.shape[2] // 3, fill, True, local_step=0)
        _group_copies(tail_ref, tail_ref.shape[2] // 3, fill, False, local_step=0)


def _dispatch(h, route, table, tails, n_rows):
    n, d = h.shape
    tm = route.shape[2]
    tab_spec = lambda index: pl.BlockSpec((1, 1, table.shape[2]), index, memory_space=pltpu.SMEM)
    return pl.pallas_call(
        _dispatch_kernel,
        grid=(n // tm,),
        in_specs=[tab_spec(lambda i: (i, 0, 0)),
                  tab_spec(lambda i: (jnp.maximum(i - 1, 0), 0, 0)),
                  pl.BlockSpec(tails.shape, lambda i: (0, 0, 0), memory_space=pltpu.SMEM),
                  pl.BlockSpec((1, 8, tm), lambda i: (i, 0, 0)),
                  pl.BlockSpec((tm, d), lambda i: (i, 0))],
        out_specs=pl.BlockSpec(memory_space=pl.ANY),
        out_shape=jax.ShapeDtypeStruct((n_rows, d), h.dtype),
        scratch_shapes=[pltpu.VMEM((2, _sorted_rows(tm), d), F32),
                        pltpu.VMEM((GROUP_CHUNK, d), F32), pltpu.SemaphoreType.DMA((2,))],
        compiler_params=pltpu.CompilerParams(dimension_semantics=("arbitrary",),
                                             vmem_limit_bytes=VMEM_LIMIT, has_side_effects=True),
        name="moe_dispatch",
    )(table, table, tails, route, h)


def _experts_kernel(be_ref, nb_ref, x_ref, wg_ref, wu_ref, wd_ref, y_ref):
    del be_ref

    @pl.when(pl.program_id(0) < nb_ref[0])
    def _():
        x = x_ref[...].astype(BF16)
        f = wg_ref.shape[2]
        step = f // 7
        acc = None
        for c0 in range(0, f, step):
            gate = jnp.dot(x, wg_ref[0, :, c0:c0 + step], preferred_element_type=F32)
            up = jnp.dot(x, wu_ref[0, :, c0:c0 + step], preferred_element_type=F32)
            part = _dot(_silu(gate) * up, wd_ref[0, c0:c0 + step, :])
            acc = part if acc is None else acc + part
        y_ref[...] = acc

    @pl.when(pl.program_id(0) >= nb_ref[0])
    def _():
        y_ref[...] = jnp.zeros_like(y_ref)


def _experts(rows, block_e, n_used, wg, wu, wd):
    n_rows, d = rows.shape
    rb = MOE_ROW_BLOCK
    f = wg.shape[2]
    wspec = lambda shape: pl.BlockSpec((1,) + shape, lambda i, be, nb: (be[i], 0, 0),
                                       pipeline_mode=pl.Buffered(1))
    return pl.pallas_call(
        _experts_kernel,
        grid_spec=pltpu.PrefetchScalarGridSpec(
            num_scalar_prefetch=2,
            grid=(n_rows // rb,),
            in_specs=[pl.BlockSpec((rb, d), lambda i, be, nb: (i, 0)),
                      wspec((d, f)), wspec((d, f)), wspec((f, d))],
            out_specs=pl.BlockSpec((rb, d), lambda i, be, nb: (i, 0))),
        out_shape=jax.ShapeDtypeStruct((n_rows, d), F32),
        compiler_params=_params(("arbitrary",)),
        name="moe_experts",
    )(block_e, n_used, rows, wg, wu, wd)


def _combine_kernel(tab_ref, next_tab_ref, route_ref, y_hbm, x_ref, mod_ref, lnf_ref, out_ref,
                    sorted_ref, sem, *, final_norm):
    step = pl.program_id(0) * pl.num_programs(1) + pl.program_id(1)
    slot = step % 2
    ne = tab_ref.shape[2] // 3
    make = lambda s: (lambda loc, glob: _row_copy(y_hbm.at[glob], sorted_ref.at[s, loc],
                                                  sem.at[s]))

    @pl.when(step == 0)
    def _():
        sorted_ref[...] = jnp.zeros_like(sorted_ref)
        _group_copies(tab_ref, ne, make(slot), True)

    @pl.when(step + 1 < pl.num_programs(0) * pl.num_programs(1))
    def _():
        _group_copies(next_tab_ref, ne, make(1 - slot), True)

    _group_copies(tab_ref, ne, make(slot), False)
    pos = route_ref[0, 4:6, :].astype(jnp.int32)
    wts = route_ref[0, 2:4, :]
    j = lax.broadcasted_iota(jnp.int32, (sorted_ref.shape[1], x_ref.shape[1]), 0)
    hit1, hit2 = j == pos[0:1, :], j == pos[1:2, :]
    row_w = jnp.sum(jnp.where(hit1, wts[0:1, :], 0.0) + jnp.where(hit2, wts[1:2, :], 0.0),
                    axis=1, keepdims=True)
    pick = jnp.where(hit1 | hit2, 1.0, 0.0).astype(BF16)
    z = (sorted_ref[slot] * row_w).astype(BF16)
    f = _dot_tn(pick, z)
    out = x_ref[0] + mod_ref[0, 5:6, :] * f
    if final_norm:
        out = _norm_mod(out, lnf_ref[...], 0.0, 0.0)
    out_ref[0] = out


def _combine(y_rows, route, table, x, mod, ln_final, final_norm):
    b, t, d = x.shape
    tm = route.shape[2]
    nt = t // tm
    return pl.pallas_call(
        functools.partial(_combine_kernel, final_norm=final_norm),
        grid=(b, nt),
        in_specs=[pl.BlockSpec((1, 1, table.shape[2]), lambda i, j: (i * nt + j, 0, 0),
                               memory_space=pltpu.SMEM),
                  pl.BlockSpec((1, 1, table.shape[2]),
                               lambda i, j: (jnp.minimum(i * nt + j + 1, b * nt - 1), 0, 0),
                               memory_space=pltpu.SMEM),
                  pl.BlockSpec((1, 8, tm), lambda i, j: (i * nt + j, 0, 0)),
                  pl.BlockSpec(memory_space=pl.ANY),
                  pl.BlockSpec((1, tm, d), lambda i, j: (i, j, 0)),
                  pl.BlockSpec((1, N_MOD, d), lambda i, j: (i, 0, 0)),
                  _const_spec(ln_final.shape)],
        out_specs=pl.BlockSpec((1, tm, d), lambda i, j: (i, j, 0)),
        out_shape=jax.ShapeDtypeStruct((b, t, d), F32),
        scratch_shapes=[pltpu.VMEM((2, _sorted_rows(tm), d), F32), pltpu.SemaphoreType.DMA((2,))],
        compiler_params=_params(("arbitrary", "arbitrary")),
        name="moe_combine",
    )(table, table, route, y_rows, x, mod, ln_final)


def _ffn_moe(x, mod, ln, ln_final, router, wg, wu, wd, final_norm):
    b, t, d = x.shape
    n = b * t
    ne = router.shape[1]
    rb = MOE_ROW_BLOCK
    h, route, cnt = _route(x, mod, ln, router)
    nt, tm = route.shape[1], route.shape[3]
    tiles = b * nt
    cnt = cnt[..., 0].reshape(tiles, 2 * ne).astype(jnp.int32)
    size, local = cnt[:, :ne], cnt[:, ne:]
    totals = jnp.sum(size, axis=0)
    padded = ((totals + rb - 1) // rb) * rb
    padded_end = jnp.cumsum(padded)
    base = (padded_end - padded)[None, :] + jnp.cumsum(size, axis=0) - size
    table = jnp.concatenate([local, size // GROUP_ALIGN, base], axis=1).reshape(tiles, 1, 3 * ne)
    route = route.reshape(tiles, 8, tm)
    n_blocks = -(-(n * 2 + tiles * ne * (GROUP_ALIGN - 1)) // rb) + ne
    block_e = jnp.minimum(
        jnp.searchsorted(padded_end, jnp.arange(n_blocks, dtype=jnp.int32) * rb, side="right"),
        ne - 1).astype(jnp.int32)
    n_used = (padded_end[-1:] // rb).astype(jnp.int32)
    n_rows = n_blocks * rb
    gap_start = jnp.concatenate([padded_end - padded + totals, padded_end[-1:]])
    gap_end = jnp.concatenate([padded_end, jnp.full((1,), n_rows, jnp.int32)])
    tails = jnp.concatenate([jnp.zeros_like(gap_start), (gap_end - gap_start) // GROUP_ALIGN,
                             gap_start]).astype(jnp.int32).reshape(1, 1, 3 * (ne + 1))
    rows = _dispatch(h.reshape(n, d), route, table, tails, n_rows)
    y_rows = _experts(rows, block_e, n_used, wg, wu, wd)
    return _combine(y_rows, route, table, x, mod, ln_final, final_norm)


def _block_diag(m, n):
    return jnp.kron(jnp.eye(n, dtype=m.dtype), m)


def _layer_params(i, w_in, conv_w, decay_w0, decay_w2, iclr_a0, iclr_a2, gate_g2, k_k, k_a, r_k,
                  lnx_w, lnx_b, w_fnet_out, w_rwkv_out, w_o):
    w = RWKV_WIDTH
    lora = decay_w2.shape[2]
    tail0 = FNET_WIDTH + 3 * w

    def reorder_tail(m):
        pad = jnp.zeros(m.shape[:-1] + (LANES - lora,), m.dtype)
        return jnp.concatenate([m[..., :3 * lora], pad, m[..., 3 * lora:]], axis=-1)

    row = lambda v: v.reshape(1, -1)
    hi_lo = lambda m: jnp.stack(_split_bf16(m, 2))
    wi = w_in[i]
    rwkv_cols = conv_w.shape[2]
    zeros = jnp.zeros((lora, w), F32)
    return {
        "w_a": wi[:, :FNET_WIDTH].astype(BF16),
        "w_main": wi[:, FNET_WIDTH:tail0].astype(BF16),
        "w_tail": reorder_tail(wi[:, tail0:FNET_WIDTH + rwkv_cols]).astype(BF16),
        "w_gates": wi[:, FNET_WIDTH + rwkv_cols:].astype(BF16),
        "conv_main": conv_w[i][:, :3 * w],
        "conv_tail": reorder_tail(conv_w[i][:, 3 * w:]),
        "decay_w0": decay_w0[i].reshape(1, 2 * w),
        "decay_w2": hi_lo(jnp.concatenate(
            [jnp.concatenate([decay_w2[i, 0], zeros], axis=1),
             jnp.concatenate([zeros, decay_w2[i, 1]], axis=1)], axis=0)),
        "iclr_a0": row(iclr_a0[i]),
        "iclr_a2": hi_lo(jnp.concatenate([iclr_a2[i], jnp.zeros((LANES - lora, w), F32)], axis=0)),
        "gate_g2": hi_lo(gate_g2[i]),
        "k_k": row(k_k[i]), "k_a": row(k_a[i]), "r_k": row(r_k[i]),
        "lnx_w": row(lnx_w[i]), "lnx_b": row(lnx_b[i]),
        "head_sum": _block_diag(jnp.ones((HEAD_DIM, HEAD_DIM), BF16), LANES // HEAD_DIM),
        "head_mean": _block_diag(jnp.full((HEAD_DIM, HEAD_DIM), 1.0 / HEAD_DIM, BF16),
                                 LANES // HEAD_DIM),
        "w_fnet_out": w_fnet_out[i].astype(BF16),
        "w_rwkv_out": w_rwkv_out[i].astype(BF16),
        "w_o": w_o[i].astype(BF16),
    }


def _channel_dft():
    k = jnp.arange(HEAD_DIM, dtype=jnp.int32)
    ang = ((k[:, None] * k[None, :]) % HEAD_DIM).astype(F32) * (2.0 * math.pi / HEAD_DIM)
    groups = FNET_WIDTH // HEAD_DIM
    return jnp.concatenate([_block_diag(jnp.cos(ang), groups), _block_diag(jnp.sin(ang), groups)],
                           axis=1).astype(BF16)


def _token_mixer(x, mod, ln, p, cs, dft_tab, ffn=None):
    zc, zs, um, ut, gates = _inproj(x, mod, ln, p["w_a"], p["w_main"], p["w_tail"], p["w_gates"], cs)
    yf = _fnet(zc, zs, dft_tab)
    r, k, v, a, b, e_f, e_b, g, bonus = _prep(um, ut, p)
    o = _wkv(r, k, v, a, b, e_b, True, add=_wkv(r, k, v, a, b, e_f, False))
    return _mixout(o, bonus, g, yf, gates, x, mod, p, ffn)


def kernel(x_prompt, x_sample, c_prompt, c_sample, w_ada, b_ada, ln_mix, w_in, conv_w, decay_w0, decay_w2, iclr_a0, iclr_a2, gate_g2, k_k, k_a, r_k, lnx_w, lnx_b, w_fnet_out, w_rwkv_out, w_o, ln_ffn, ff_w_gate, ff_w_up, ff_w_down, router_w, moe_w_gate, moe_w_up, moe_w_down, ln_final):
    depth, d = ln_mix.shape
    streams = [x_prompt, x_sample]
    nb = [x.shape[0] for x in streams]
    mod_all = _ada(jnp.concatenate([c_prompt, c_sample], axis=0), w_ada, b_ada)
    cs = _channel_dft()
    dfts = {}
    for x in streams:
        if x.shape[1] not in dfts:
            dfts[x.shape[1]] = _dft_tables(x.shape[1])
    lnf = ln_final.reshape(1, d)
    for i in range(depth):
        p = _layer_params(i, w_in, conv_w, decay_w0, decay_w2, iclr_a0, iclr_a2, gate_g2, k_k, k_a,
                          r_k, lnx_w, lnx_b, w_fnet_out, w_rwkv_out, w_o)
        j = i // 2
        final = i == depth - 1
        if i % 2 == 0:
            ffw = (ff_w_gate[j].astype(BF16), ff_w_up[j].astype(BF16), ff_w_down[j].astype(BF16))
        else:
            ffw = (moe_w_gate[j].astype(BF16), moe_w_up[j].astype(BF16), moe_w_down[j].astype(BF16))
            router = router_w[j]
        off = 0
        for s, x in enumerate(streams):
            mod = mod_all[i, off:off + nb[s]].reshape(nb[s], N_MOD, d)
            off += nb[s]
            ln1, ln2 = ln_mix[i].reshape(1, d), ln_ffn[i].reshape(1, d)
            if i % 2 == 0:
                x = _token_mixer(x, mod, ln1, p, cs, dfts[x.shape[1]], (ln2, lnf, *ffw, final))
            else:
                x = _token_mixer(x, mod, ln1, p, cs, dfts[x.shape[1]])
                x = _ffn_moe(x, mod, ln2, lnf, router, *ffw, final)
            streams[s] = x
    return tuple(streams)
```
